```python
import math
import jax, jax.numpy as jnp
from jax import lax
import numpy as np

D_MODEL = 1024
BATCH = 32
SEQ = 2048
DEPTH = 2
DEC_BATCH = 32
DEC_SEQ = 32
PAST_LEN = 1024

CHUNK = 64
N_HEADS = D_MODEL // 128
HEAD_DIM = 64
QK_DIM = 2 * HEAD_DIM
V_DIM = 2 * HEAD_DIM
QK_WIDTH = N_HEADS * QK_DIM
ATTN_WIDTH = N_HEADS * V_DIM
POOL_WINDOWS = (2, 4, 8, 16)
N_POOL_GROUPS = len(POOL_WINDOWS)
POOL_GROUP_DIM = D_MODEL // 8
POOL_WIDTH = N_POOL_GROUPS * POOL_GROUP_DIM
POOL_HIST = max(POOL_WINDOWS) - 1
IN_WIDTH = POOL_WIDTH + 2 * QK_WIDTH + ATTN_WIDTH + 2 * D_MODEL
N_BUCKETS = 32
MAX_DISTANCE = 128
Q_BLOCK = 128
D_FF = ((8 * D_MODEL // 3 + 127) // 128) * 128
N_EXPERTS = 8
TOP_K = 2
D_FF_EXPERT = 7 * D_MODEL // 2
N_DENSE = (DEPTH + 1) // 2
N_MOE = DEPTH // 2
ALPHA = (2.0 * DEPTH) ** 0.25
BETA = (8.0 * DEPTH) ** -0.25
LN_EPS = 1e-5
RMS_EPS = 1e-5

kernel_name = "chunk_causal_pool_diffattn_moe_stream"


def layer_norm(x, g, b):
    xf = x.astype(jnp.float32)
    mu = jnp.mean(xf, axis=-1, keepdims=True)
    var = jnp.mean(jnp.square(xf - mu), axis=-1, keepdims=True)
    return ((xf - mu) * lax.rsqrt(var + LN_EPS) * g + b).astype(x.dtype)


def rel_bucket(rel):
    nb = N_BUCKETS // 2
    max_exact = nb // 2
    n = jnp.abs(rel)
    large = max_exact + (jnp.log(jnp.maximum(n, 1).astype(jnp.float32) / max_exact)
                         / math.log(MAX_DISTANCE / max_exact) * (nb - max_exact)).astype(jnp.int32)
    large = jnp.minimum(large, nb - 1)
    return jnp.where(rel > 0, nb, 0) + jnp.where(n < max_exact, n, large)


def diff_attn_block(q, k, v, q_pos, k_pos, rel_bias, lam):
    rel = k_pos[None, :] - q_pos[:, None]
    bias = jnp.transpose(rel_bias[rel_bucket(rel)], (2, 0, 1)).astype(jnp.float32)
    allowed = (k_pos[None, :] // CHUNK) <= (q_pos[:, None] // CHUNK)
    logits = jnp.einsum('bqhcd,bkhcd->bchqk', q, k, preferred_element_type=jnp.float32) * (HEAD_DIM ** -0.5)
    logits = jnp.where(allowed, logits + bias, -jnp.inf)
    p = jax.nn.softmax(logits, axis=-1)
    a = p[:, 0] - lam * p[:, 1]
    return jnp.einsum('bhqk,bkhd->bqhd', a.astype(v.dtype), v)


def prompt_attention(q, k, v, pos, rel_bias, lam):
    B, S = q.shape[0], q.shape[1]
    nblk = S // Q_BLOCK
    qb = jnp.moveaxis(q.reshape(B, nblk, Q_BLOCK, N_HEADS, 2, HEAD_DIM), 1, 0)
    pb = pos.reshape(nblk, Q_BLOCK)
    out = lax.map(lambda args: diff_attn_block(args[0], k, v, args[1], pos, rel_bias, lam), (qb, pb))
    return jnp.moveaxis(out, 0, 1).reshape(B, S, N_HEADS, V_DIM)


def pool_mixer(u, hist, pos, pool_w, pool_scale):
    B, T, _ = u.shape
    full = jnp.concatenate([hist, u], axis=1)
    cs = jnp.pad(jnp.cumsum(full.astype(jnp.float32), axis=1), ((0, 0), (1, 0), (0, 0)))
    hi = cs[:, POOL_HIST + 1:POOL_HIST + 1 + T]
    means = []
    for gi, w in enumerate(POOL_WINDOWS):
        sl = slice(gi * POOL_GROUP_DIM, (gi + 1) * POOL_GROUP_DIM)
        lo = cs[:, POOL_HIST + 1 - w:POOL_HIST + 1 - w + T, sl]
        cnt = jnp.minimum(pos + 1, w).astype(jnp.float32)[None, :, None]
        means.append((hi[..., sl] - lo) / cnt)
    mean = jnp.concatenate(means, axis=-1)
    d = (mean - u.astype(jnp.float32)).astype(u.dtype).reshape(B, T, N_POOL_GROUPS, POOL_GROUP_DIM)
    out = jnp.einsum('btgc,gcd->btgd', d, pool_w).reshape(B, T, POOL_WIDTH) * pool_scale
    return out, full[:, -POOL_HIST:]


def token_mixer(h, pos, pool_hist, k_past, v_past, w_in, b_gate, pool_w, pool_scale, lam_qk, subln_g,
                w_pool_up, w_attn_up, w_o, rel_bias, layer):
    B, T, _ = h.shape
    proj = h @ w_in
    o1 = POOL_WIDTH
    o2 = o1 + QK_WIDTH
    o3 = o2 + QK_WIDTH
    o4 = o3 + ATTN_WIDTH
    u = proj[..., :o1]
    q = proj[..., o1:o2].reshape(B, T, N_HEADS, 2, HEAD_DIM)
    k = proj[..., o2:o3].reshape(B, T, N_HEADS, 2, HEAD_DIM)
    v = proj[..., o3:o4].reshape(B, T, N_HEADS, V_DIM)
    gates = jax.nn.sigmoid((proj[..., o4:] + b_gate).astype(jnp.float32)).astype(h.dtype)

    pool_out, pool_state = pool_mixer(u, pool_hist, pos, pool_w, pool_scale)

    lam_init = 0.8 - 0.6 * math.exp(-0.3 * layer)
    lf = lam_qk.astype(jnp.float32)
    lam = jnp.exp(jnp.sum(lf[0] * lf[1])) - jnp.exp(jnp.sum(lf[2] * lf[3])) + lam_init
    if k_past is None:
        o = prompt_attention(q, k, v, pos, rel_bias, lam)
    else:
        P = k_past.shape[1]
        k_all = jnp.concatenate([k_past.reshape(B, P, N_HEADS, 2, HEAD_DIM), k], axis=1)
        v_all = jnp.concatenate([v_past, v], axis=1)
        k_pos = jnp.arange(P + T, dtype=jnp.int32)
        o = diff_attn_block(q, k_all, v_all, pos, k_pos, rel_bias, lam)
    of = o.astype(jnp.float32)
    of = of * lax.rsqrt(jnp.mean(of * of, axis=-1, keepdims=True) + RMS_EPS) * subln_g * (1.0 - lam_init)
    attn_out = of.astype(h.dtype).reshape(B, T, ATTN_WIDTH)

    merged = gates[..., :D_MODEL] * (pool_out @ w_pool_up) + gates[..., D_MODEL:] * (attn_out @ w_attn_up)
    return merged @ w_o, pool_state, k.reshape(B, T, N_HEADS, QK_DIM), v


def swiglu(h, wg, wu, wd):
    return (jax.nn.silu(h @ wg) * (h @ wu)) @ wd


def moe_swiglu(h, w_router, wg, wu, wd):
    logits = (h @ w_router).astype(jnp.float32)
    top_val, top_idx = lax.top_k(logits, TOP_K)
    top_w = jax.nn.softmax(top_val, axis=-1)
    combine = jnp.sum(jax.nn.one_hot(top_idx, N_EXPERTS, dtype=jnp.float32) * top_w[..., None], axis=-2)
    combine = combine.astype(h.dtype)
    out = jnp.zeros_like(h)
    for e in range(N_EXPERTS):
        out = out + combine[..., e:e + 1] * swiglu(h, wg[e], wu[e], wd[e])
    return out


def trunk(x, c, pos, pool_hist, k_past, v_past, W):
    ks, vs, ps = [], [], []
    for l in range(DEPTH):
        mod = (jax.nn.silu(c) @ W['w_ada'][l] + W['b_ada'][l])[:, None, :]
        sh1, sc1, g1, sh2, sc2, g2 = jnp.split(mod, 6, axis=-1)
        h = x * (1 + sc1) + sh1
        mix, pst, kn, vn = token_mixer(
            h, pos, pool_hist[l],
            None if k_past is None else k_past[l], None if v_past is None else v_past[l],
            W['w_in'][l], W['b_gate'][l], W['pool_w'][l], W['pool_scale'][l], W['lam_qk'][l], W['subln_g'][l],
            W['w_pool_up'][l], W['w_attn_up'][l], W['w_o'][l], W['rel_bias'], l)
        x = layer_norm(ALPHA * x + g1 * mix, W['ln_g'][l, 0], W['ln_b'][l, 0])
        h = x * (1 + sc2) + sh2
        if l % 2 == 0:
            i = l // 2
            f = swiglu(h, W['w_ffn_gate'][i], W['w_ffn_up'][i], W['w_ffn_down'][i])
        else:
            i = l // 2
            f = moe_swiglu(h, W['w_router'][i], W['w_exp_gate'][i], W['w_exp_up'][i], W['w_exp_down'][i])
        x = layer_norm(ALPHA * x + g2 * f, W['ln_g'][l, 1], W['ln_b'][l, 1])
        ks.append(kn)
        vs.append(vn)
        ps.append(pst)
    return x, jnp.stack(ks), jnp.stack(vs), jnp.stack(ps)


def setup_inputs(seed: int = 0) -> dict:
    key = jax.random.key(seed)
    ks = jax.random.split(key, 32)
    f32 = jnp.float32
    D = D_MODEL

    def nrm(k, shape, s):
        return jax.random.normal(k, shape, f32) * s

    return {
        'x_prompt': nrm(ks[0], (BATCH, SEQ, D), 1.0),
        'x_sample': nrm(ks[1], (DEC_BATCH, DEC_SEQ, D), 1.0),
        'cache_k': nrm(ks[2], (DEPTH, DEC_BATCH, PAST_LEN, N_HEADS, QK_DIM), 1.0),
        'cache_v': nrm(ks[3], (DEPTH, DEC_BATCH, PAST_LEN, N_HEADS, V_DIM), 1.0),
        'state_pool': nrm(ks[4], (DEPTH, DEC_BATCH, POOL_HIST, POOL_WIDTH), 1.0),
        'c_prompt': nrm(ks[5], (BATCH, D), 1.0),
        'c_sample': nrm(ks[6], (DEC_BATCH, D), 1.0),
        'rel_bias': nrm(ks[7], (N_BUCKETS, N_HEADS), 0.5),
        'w_ada': nrm(ks[8], (DEPTH, D, 6 * D), 0.5 * D ** -0.5),
        'b_ada': nrm(ks[9], (DEPTH, 6 * D), 0.02),
        'w_in': nrm(ks[10], (DEPTH, D, IN_WIDTH), D ** -0.5),
        'b_gate': nrm(ks[11], (DEPTH, 2 * D), 0.02),
        'pool_w': nrm(ks[12], (DEPTH, N_POOL_GROUPS, POOL_GROUP_DIM, POOL_GROUP_DIM), POOL_GROUP_DIM ** -0.5),
        'pool_scale': 1.0 + nrm(ks[13], (DEPTH, POOL_WIDTH), 0.02),
        'lam_qk': nrm(ks[14], (DEPTH, 4, HEAD_DIM), 0.1),
        'subln_g': 1.0 + nrm(ks[15], (DEPTH, V_DIM), 0.02),
        'w_pool_up': nrm(ks[16], (DEPTH, POOL_WIDTH, D), POOL_WIDTH ** -0.5),
        'w_attn_up': nrm(ks[17], (DEPTH, ATTN_WIDTH, D), ATTN_WIDTH ** -0.5),
        'w_o': nrm(ks[18], (DEPTH, D, D), BETA * D ** -0.5),
        'ln_g': 1.0 + nrm(ks[19], (DEPTH, 2, D), 0.02),
        'ln_b': nrm(ks[20], (DEPTH, 2, D), 0.02),
        'w_ffn_gate': nrm(ks[21], (N_DENSE, D, D_FF), D ** -0.5),
        'w_ffn_up': nrm(ks[22], (N_DENSE, D, D_FF), D ** -0.5),
        'w_ffn_down': nrm(ks[23], (N_DENSE, D_FF, D), BETA * D_FF ** -0.5),
        'w_router': nrm(ks[24], (N_MOE, D, N_EXPERTS), D ** -0.5),
        'w_exp_gate': nrm(ks[25], (N_MOE, N_EXPERTS, D, D_FF_EXPERT), D ** -0.5),
        'w_exp_up': nrm(ks[26], (N_MOE, N_EXPERTS, D, D_FF_EXPERT), D ** -0.5),
        'w_exp_down': nrm(ks[27], (N_MOE, N_EXPERTS, D_FF_EXPERT, D), BETA * D_FF_EXPERT ** -0.5),
    }


def reference(x_prompt, x_sample, cache_k, cache_v, state_pool, c_prompt, c_sample, rel_bias, w_ada, b_ada,
              w_in, b_gate, pool_w, pool_scale, lam_qk, subln_g, w_pool_up, w_attn_up, w_o, ln_g, ln_b,
              w_ffn_gate, w_ffn_up, w_ffn_down, w_router, w_exp_gate, w_exp_up, w_exp_down):
    W = dict(rel_bias=rel_bias, w_ada=w_ada, b_ada=b_ada, w_in=w_in, b_gate=b_gate, pool_w=pool_w,
             pool_scale=pool_scale, lam_qk=lam_qk, subln_g=subln_g, w_pool_up=w_pool_up, w_attn_up=w_attn_up,
             w_o=w_o, ln_g=ln_g, ln_b=ln_b, w_ffn_gate=w_ffn_gate, w_ffn_up=w_ffn_up, w_ffn_down=w_ffn_down,
             w_router=w_router, w_exp_gate=w_exp_gate, w_exp_up=w_exp_up, w_exp_down=w_exp_down)

    B, S = x_prompt.shape[0], x_prompt.shape[1]
    pos_p = jnp.arange(S, dtype=jnp.int32)
    pool_zero = jnp.zeros((DEPTH, B, POOL_HIST, POOL_WIDTH), x_prompt.dtype)
    y_prompt, new_k_prompt, new_v_prompt, new_pool_prompt = trunk(
        x_prompt, c_prompt, pos_p, pool_zero, None, None, W)

    P = cache_k.shape[2]
    T = x_sample.shape[1]
    pos_s = P + jnp.arange(T, dtype=jnp.int32)
    y_sample, new_k_sample, new_v_sample, new_pool_sample = trunk(
        x_sample, c_sample, pos_s, state_pool, cache_k, cache_v, W)

    return (y_prompt, y_sample, new_k_prompt, new_v_prompt, new_pool_prompt,
            new_k_sample, new_v_sample, new_pool_sample)
```

```python
import functools
import math

import numpy as np
import jax
import jax.numpy as jnp
from jax import lax
from jax.experimental import pallas as pl
from jax.experimental.pallas import tpu as pltpu

F32 = jnp.float32
BF16 = jnp.bfloat16
I32 = jnp.int32

CHUNK = 64
HEAD_DIM = 64
HEAD_WIDTH = 2 * HEAD_DIM
POOL_WINDOWS = (2, 4, 8, 16)
POOL_GROUP_DIM = 128
POOL_WIDTH = len(POOL_WINDOWS) * POOL_GROUP_DIM
POOL_HIST = max(POOL_WINDOWS) - 1
HIST_ROWS = 16
N_BUCKETS = 32
MAX_DISTANCE = 128
FAR_BUCKET = N_BUCKETS // 2 - 1
LN_EPS = 1e-5
RMS_EPS = 1e-5
MASKED = -1e30

LANES = 128
ROW_TILE = 512
ATTN_TQ = 256
ATTN_TK = 256
MOE_TM_LARGE = 1024
MOE_TM_SMALL = 256
MOE_FC = 512
VMEM_LIMIT = 56 * 1024 * 1024


def _cparams(sem):
    return pltpu.CompilerParams(dimension_semantics=sem, vmem_limit_bytes=VMEM_LIMIT)


def _resident(shape, index_map):
    return pl.BlockSpec(shape, index_map, pipeline_mode=pl.Buffered(1))


def _row_block(b, t):
    if t >= ROW_TILE:
        assert t % ROW_TILE == 0
        return 1, ROW_TILE
    nb = min(b, ROW_TILE // t)
    assert b % nb == 0 and t % 8 == 0
    return nb, t


def _layer_norm(y, g, b):
    mu = jnp.mean(y, axis=-1, keepdims=True)
    yc = y - mu
    var = jnp.mean(yc * yc, axis=-1, keepdims=True)
    return yc * lax.rsqrt(var + LN_EPS) * g + b


def _ada_kernel(c_ref, w_ref, b_ref, o_ref):
    c = c_ref[...]
    s = c * jax.nn.sigmoid(c)
    o_ref[0] = jnp.dot(s.astype(BF16), w_ref[0].astype(BF16), preferred_element_type=F32) + b_ref[0]


def _ada(c_all, w_ada, b_ada):
    depth, d, n6 = w_ada.shape
    bc = c_all.shape[0]
    tn = 1536 if n6 % 1536 == 0 else n6
    return pl.pallas_call(
        _ada_kernel,
        grid=(depth, n6 // tn),
        in_specs=[
            pl.BlockSpec((bc, d), lambda l, j: (0, 0)),
            pl.BlockSpec((1, d, tn), lambda l, j: (l, 0, j)),
            pl.BlockSpec((1, 1, tn), lambda l, j: (l, 0, j)),
        ],
        out_specs=pl.BlockSpec((1, bc, tn), lambda l, j: (l, 0, j)),
        out_shape=jax.ShapeDtypeStruct((depth, bc, n6), F32),
        compiler_params=_cparams(("arbitrary", "arbitrary")),
        name="ada",
    )(c_all, w_ada, b_ada.reshape(depth, 1, n6))


def _mod_spec(nb, d, layer, chunk):
    return pl.BlockSpec((None, nb, 1, d), lambda bi, ti: (layer, bi, 0, chunk))


def _rel_bucket_np(rel):
    nb = N_BUCKETS // 2
    max_exact = nb // 2
    n = np.abs(rel)
    large = max_exact + (
        np.log(np.maximum(n, 1).astype(np.float32) / np.float32(max_exact))
        / np.float32(math.log(MAX_DISTANCE / max_exact))
        * np.float32(nb - max_exact)
    ).astype(np.int32)
    large = np.minimum(large, nb - 1)
    return np.where(rel > 0, nb, 0) + np.where(n < max_exact, n, large)


def _bucket_table(q_pos, k_pos):
    rel = k_pos[None, :] - q_pos[:, None]
    allowed = (k_pos[None, :] // CHUNK) <= (q_pos[:, None] // CHUNK)
    return np.where(allowed, _rel_bucket_np(rel), -1).astype(np.int32)


def _bias_kernel(rb_ref, bk_ref, o_ref):
    h = pl.program_id(0)
    bk = bk_ref[...]
    far = rb_ref[FAR_BUCKET, h]
    acc = jnp.zeros(bk.shape, F32)
    for b in range(N_BUCKETS):
        acc = jnp.where(bk == b, rb_ref[b, h] - far, acc)
    o_ref[0] = jnp.where(bk < 0, MASKED, acc)


def _bias_table(rel_bias, bucket_np):
    rows, cols = bucket_np.shape
    n_heads = rel_bias.shape[1]
    return pl.pallas_call(
        _bias_kernel,
        grid=(n_heads,),
        in_specs=[
            pl.BlockSpec(memory_space=pltpu.SMEM),
            pl.BlockSpec((rows, cols), lambda h: (0, 0)),
        ],
        out_specs=pl.BlockSpec((1, rows, cols), lambda h: (h, 0, 0)),
        out_shape=jax.ShapeDtypeStruct((n_heads, rows, cols), F32),
        compiler_params=_cparams(("arbitrary",)),
        name="bias_table",
    )(rel_bias, jnp.asarray(bucket_np))


def _inproj_kernel(*refs, pos0, has_hist, has_alias, n_heads):
    it = iter(refs)
    x_ref, sh_ref, sc_ref, w_ref, pw_ref, ps_ref = (next(it) for _ in range(6))
    hist_ref = next(it) if has_hist else None
    if has_alias:
        next(it)
        next(it)
    q_ref, kb_ref, vb_ref, k_ref, v_ref, po_ref, pst_ref, ext_ref = (next(it) for _ in range(8))

    ti = pl.program_id(1)
    nb, tt, d = x_ref.shape
    rows = nb * tt
    qkw = n_heads * HEAD_WIDTH
    o1 = POOL_WIDTH
    o2 = o1 + qkw
    o3 = o2 + qkw
    o4 = o3 + qkw

    h = x_ref[...] * (1.0 + sc_ref[...]) + sh_ref[...]
    hb = h.reshape(rows, d).astype(BF16)

    u = jnp.dot(hb, w_ref[:, 0:o1], preferred_element_type=F32)
    q = jnp.dot(hb, w_ref[:, o1:o2], preferred_element_type=F32)
    q_ref[...] = (q * (HEAD_DIM ** -0.5)).astype(BF16).reshape(nb, tt, qkw)
    k = jnp.dot(hb, w_ref[:, o2:o3], preferred_element_type=F32)
    kb_ref[...] = k.astype(BF16).reshape(nb, tt, qkw)
    v = jnp.dot(hb, w_ref[:, o3:o4], preferred_element_type=F32)
    vb_ref[...] = v.astype(BF16).reshape(nb, tt, qkw)
    for hh in range(n_heads):
        sl = slice(hh * HEAD_WIDTH, (hh + 1) * HEAD_WIDTH)
        k_ref[:, :, hh, :] = k[:, sl].reshape(nb, tt, HEAD_WIDTH)
        v_ref[:, :, hh, :] = v[:, sl].reshape(nb, tt, HEAD_WIDTH)

    @pl.when(ti == 0)
    def _():
        if has_hist:
            ext_ref[:, 0:1, :] = jnp.zeros((nb, 1, POOL_WIDTH), F32)
            ext_ref[:, 1:HIST_ROWS, :] = hist_ref[...]
        else:
            ext_ref[:, 0:HIST_ROWS, :] = jnp.zeros((nb, HIST_ROWS, POOL_WIDTH), F32)

    ext_ref[:, HIST_ROWS:HIST_ROWS + tt, :] = u.reshape(nb, tt, POOL_WIDTH)
    pos = pos0 + ti * tt + lax.broadcasted_iota(I32, (1, tt, 1), 1)
    outs = []
    for gi, w in enumerate(POOL_WINDOWS):
        ls = slice(gi * POOL_GROUP_DIM, (gi + 1) * POOL_GROUP_DIM)
        tot = ext_ref[:, HIST_ROWS:HIST_ROWS + tt, ls]
        for j in range(1, w):
            tot = tot + ext_ref[:, HIST_ROWS - j:HIST_ROWS - j + tt, ls]
        cnt = jnp.minimum(pos + 1, w).astype(F32)
        dlt = (tot / cnt - ext_ref[:, HIST_ROWS:HIST_ROWS + tt, ls]).reshape(rows, POOL_GROUP_DIM)
        outs.append(jnp.dot(dlt.astype(BF16), pw_ref[gi], preferred_element_type=F32))
    po = jnp.concatenate(outs, axis=-1) * ps_ref[...]
    po_ref[...] = po.astype(BF16).reshape(nb, tt, POOL_WIDTH)
    pst_ref[...] = ext_ref[:, tt + 1:tt + HIST_ROWS, :]
    ext_ref[:, 0:HIST_ROWS, :] = ext_ref[:, tt:tt + HIST_ROWS, :]


def _inproj(x, mod, layer, w_a, pool_w, pool_scale, hist, k5, v5, depth, pos0):
    b, t, d = x.shape
    nb, tt = _row_block(b, t)
    n_heads = d // HEAD_WIDTH
    qkw = n_heads * HEAD_WIDTH
    has_hist = hist is not None
    has_alias = k5 is not None
    row3 = lambda w: pl.BlockSpec((nb, tt, w), lambda bi, ti: (bi, ti, 0))
    cache_spec = pl.BlockSpec((None, nb, tt, n_heads, HEAD_WIDTH), lambda bi, ti: (layer, bi, ti, 0, 0))
    in_specs = [
        row3(d),
        _mod_spec(nb, d, layer, 0),
        _mod_spec(nb, d, layer, 1),
        _resident(w_a.shape, lambda bi, ti: (0, 0)),
        _resident(pool_w.shape, lambda bi, ti: (0, 0, 0)),
        _resident((1, POOL_WIDTH), lambda bi, ti: (0, 0)),
    ]
    args = [x, mod, mod, w_a, pool_w, pool_scale.reshape(1, POOL_WIDTH)]
    if has_hist:
        in_specs.append(pl.BlockSpec((nb, POOL_HIST, POOL_WIDTH), lambda bi, ti: (bi, 0, 0)))
        args.append(hist)
    aliases = {}
    if has_alias:
        aliases = {len(args): 3, len(args) + 1: 4}
        in_specs += [pl.BlockSpec(memory_space=pl.ANY), pl.BlockSpec(memory_space=pl.ANY)]
        args += [k5, v5]
    cache_shape = jax.ShapeDtypeStruct((depth, b, t, n_heads, HEAD_WIDTH), F32)
    outs = pl.pallas_call(
        functools.partial(_inproj_kernel, pos0=pos0, has_hist=has_hist, has_alias=has_alias, n_heads=n_heads),
        grid=(b // nb, t // tt),
        in_specs=in_specs,
        out_specs=[
            row3(qkw), row3(qkw), row3(qkw), cache_spec, cache_spec, row3(POOL_WIDTH),
            pl.BlockSpec((nb, POOL_HIST, POOL_WIDTH), lambda bi, ti: (bi, 0, 0)),
        ],
        out_shape=[
            jax.ShapeDtypeStruct((b, t, qkw), BF16),
            jax.ShapeDtypeStruct((b, t, qkw), BF16),
            jax.ShapeDtypeStruct((b, t, qkw), BF16),
            cache_shape, cache_shape,
            jax.ShapeDtypeStruct((b, t, POOL_WIDTH), BF16),
            jax.ShapeDtypeStruct((b, POOL_HIST, POOL_WIDTH), F32),
        ],
        scratch_shapes=[pltpu.VMEM((nb, tt + HIST_ROWS, POOL_WIDTH), F32)],
        input_output_aliases=aliases,
        compiler_params=_cparams(("arbitrary", "arbitrary")),
        name="inproj",
    )(*args)
    return outs


def _lambda(lam_ref, lam_init):
    lq = lam_ref[...]
    s01 = jnp.sum(lq[0:1, :] * lq[1:2, :], axis=-1, keepdims=True)
    s23 = jnp.sum(lq[2:3, :] * lq[3:4, :], axis=-1, keepdims=True)
    return jnp.exp(s01) - jnp.exp(s23) + lam_init


def _stack_maps(q):
    lane = lax.broadcasted_iota(I32, q.shape, 1)
    zero = jnp.zeros_like(q)
    return jnp.concatenate([jnp.where(lane < HEAD_DIM, q, zero), jnp.where(lane >= HEAD_DIM, q, zero)], axis=0)


def _diff_out(acc, l, lam, g, lam_init, rows):
    o = acc / l
    o = o[:rows] - lam * o[rows:]
    o = o * lax.rsqrt(jnp.mean(o * o, axis=-1, keepdims=True) + RMS_EPS) * g * (1.0 - lam_init)
    return o.astype(BF16)


def _nt_dot(a, b):
    return lax.dot_general(a, b, (((1,), (1,)), ((), ())), preferred_element_type=F32)


def _attn_prompt_kernel(lam_ref, g_ref, q_ref, k_ref, v_ref, bprev_ref, bdiag_ref, o_ref, m_ref, l_ref, acc_ref,
                        *, lam_init):
    qi = pl.program_id(2)
    tq = q_ref.shape[1]
    tk = ATTN_TK
    qs = _stack_maps(q_ref[0])

    m_ref[...] = jnp.full(m_ref.shape, MASKED, F32)
    l_ref[...] = jnp.zeros(l_ref.shape, F32)
    acc_ref[...] = jnp.zeros(acc_ref.shape, F32)

    def update(j, bias_ref):
        start = pl.multiple_of(j * tk, tk)
        s = _nt_dot(qs, k_ref[0, pl.ds(start, tk), :])
        if bias_ref is not None:
            bias = bias_ref[0]
            s = s + jnp.concatenate([bias, bias], axis=0)
        m_prev = m_ref[...]
        m_new = jnp.maximum(m_prev, jnp.max(s, axis=-1, keepdims=True))
        alpha = jnp.exp(m_prev - m_new)
        p = jnp.exp(s - m_new)
        l_ref[...] = alpha * l_ref[...] + jnp.sum(p, axis=-1, keepdims=True)
        acc_ref[...] = alpha * acc_ref[...] + jnp.dot(
            p.astype(BF16), v_ref[0, pl.ds(start, tk), :], preferred_element_type=F32)
        m_ref[...] = m_new

    def far_body(j, carry):
        update(j, None)
        return carry

    lax.fori_loop(0, jnp.maximum(qi - 1, 0), far_body, 0)

    @pl.when(qi >= 1)
    def _():
        update(qi - 1, bprev_ref)

    update(qi, bdiag_ref)
    lam = _lambda(lam_ref, lam_init)
    o_ref[0] = _diff_out(acc_ref[...], l_ref[...], lam, g_ref[...], lam_init, tq)


def _attn_prompt(q, kb, vb, lam_qk, subln_g, bias_prev, bias_diag, lam_init):
    b, t, qkw = q.shape
    n_heads = qkw // HEAD_WIDTH
    tq = ATTN_TQ
    assert tq == ATTN_TK and t % tq == 0
    return pl.pallas_call(
        functools.partial(_attn_prompt_kernel, lam_init=lam_init),
        grid=(b, n_heads, t // tq),
        in_specs=[
            pl.BlockSpec(lam_qk.shape, lambda bi, h, qi: (0, 0)),
            pl.BlockSpec((1, HEAD_WIDTH), lambda bi, h, qi: (0, 0)),
            pl.BlockSpec((1, tq, HEAD_WIDTH), lambda bi, h, qi: (bi, qi, h)),
            pl.BlockSpec((1, t, HEAD_WIDTH), lambda bi, h, qi: (bi, 0, h)),
            pl.BlockSpec((1, t, HEAD_WIDTH), lambda bi, h, qi: (bi, 0, h)),
            pl.BlockSpec((1, tq, ATTN_TK), lambda bi, h, qi: (h, 0, 0)),
            pl.BlockSpec((1, tq, ATTN_TK), lambda bi, h, qi: (h, 0, 0)),
        ],
        out_specs=pl.BlockSpec((1, tq, HEAD_WIDTH), lambda bi, h, qi: (bi, qi, h)),
        out_shape=jax.ShapeDtypeStruct((b, t, qkw), BF16),
        scratch_shapes=[
            pltpu.VMEM((2 * tq, 1), F32),
            pltpu.VMEM((2 * tq, 1), F32),
            pltpu.VMEM((2 * tq, HEAD_WIDTH), F32),
        ],
        compiler_params=_cparams(("arbitrary", "arbitrary", "arbitrary")),
        name="attn_prompt",
    )(lam_qk, subln_g.reshape(1, HEAD_WIDTH), q, kb, vb, bias_prev, bias_diag)


def _attn_sample_kernel(lam_ref, g_ref, q_ref, kn_ref, vn_ref, kp_ref, vp_ref, bpast_ref, bnew_ref, o_ref,
                        *, lam_init, n_heads):
    ts = q_ref.shape[1]
    lam = _lambda(lam_ref, lam_init)
    for hh in range(n_heads):
        sl = slice(hh * HEAD_WIDTH, (hh + 1) * HEAD_WIDTH)
        qs = _stack_maps(q_ref[0, :, sl])
        kp = kp_ref[0, :, hh, :].astype(BF16)
        vp = vp_ref[0, :, hh, :].astype(BF16)
        bp = bpast_ref[hh]
        bn = bnew_ref[hh]
        s_p = _nt_dot(qs, kp) + jnp.concatenate([bp, bp], axis=0)
        s_n = _nt_dot(qs, kn_ref[0, :, sl]) + jnp.concatenate([bn, bn], axis=0)
        m = jnp.maximum(jnp.max(s_p, axis=-1, keepdims=True), jnp.max(s_n, axis=-1, keepdims=True))
        p_p = jnp.exp(s_p - m)
        p_n = jnp.exp(s_n - m)
        l = jnp.sum(p_p, axis=-1, keepdims=True) + jnp.sum(p_n, axis=-1, keepdims=True)
        acc = jnp.dot(p_p.astype(BF16), vp, preferred_element_type=F32) + jnp.dot(
            p_n.astype(BF16), vn_ref[0, :, sl], preferred_element_type=F32)
        o_ref[0, :, sl] = _diff_out(acc, l, lam, g_ref[...], lam_init, ts)


def _attn_sample(q, kb, vb, k_past, v_past, layer, lam_qk, subln_g, bias_past, bias_new, lam_init):
    b, ts, qkw = q.shape
    n_heads = qkw // HEAD_WIDTH
    p = k_past.shape[2]
    row = pl.BlockSpec((1, ts, qkw), lambda bi: (bi, 0, 0))
    past = pl.BlockSpec((None, 1, p, n_heads, HEAD_WIDTH), lambda bi: (layer, bi, 0, 0, 0))
    return pl.pallas_call(
        functools.partial(_attn_sample_kernel, lam_init=lam_init, n_heads=n_heads),
        grid=(b,),
        in_specs=[
            pl.BlockSpec(lam_qk.shape, lambda bi: (0, 0)),
            pl.BlockSpec((1, HEAD_WIDTH), lambda bi: (0, 0)),
            row, row, row, past, past,
            pl.BlockSpec(bias_past.shape, lambda bi: (0, 0, 0)),
            pl.BlockSpec(bias_new.shape, lambda bi: (0, 0, 0)),
        ],
        out_specs=row,
        out_shape=jax.ShapeDtypeStruct((b, ts, qkw), BF16),
        compiler_params=_cparams(("arbitrary",)),
        name="attn_sample",
    )(lam_qk, subln_g.reshape(1, HEAD_WIDTH), q, kb, vb, k_past, v_past, bias_past, bias_new)


def _post_kernel(*refs, alpha, n_experts):
    moe = n_experts > 0
    it = iter(refs)
    (x_ref, sh1_ref, sc1_ref, g1_ref, sh2_ref, sc2_ref, po_ref, ao_ref, wgate_ref, bgate_ref, wpu_ref, wau_ref,
     wo_ref, lng_ref, lnb_ref) = (next(it) for _ in range(15))
    wr_ref = next(it) if moe else None
    x1_ref = next(it)
    h2_ref = next(it)
    if moe:
        eidx_ref, rank_ref, cw_ref, cnt_ref, carry_ref = (next(it) for _ in range(5))

    nb, tt, d = x_ref.shape
    rows = nb * tt
    x = x_ref[...]
    hb = (x * (1.0 + sc1_ref[...]) + sh1_ref[...]).reshape(rows, d).astype(BF16)
    gates = jax.nn.sigmoid(jnp.dot(hb, wgate_ref[...], preferred_element_type=F32) + bgate_ref[...])
    pu = jnp.dot(po_ref[...].reshape(rows, -1), wpu_ref[...], preferred_element_type=F32)
    au = jnp.dot(ao_ref[...].reshape(rows, -1), wau_ref[...], preferred_element_type=F32)
    merged = gates[:, :d] * pu + gates[:, d:] * au
    mix = jnp.dot(merged.astype(BF16), wo_ref[...], preferred_element_type=F32)
    y = alpha * x + g1_ref[...] * mix.reshape(nb, tt, d)
    x1 = _layer_norm(y, lng_ref[...], lnb_ref[...])
    x1_ref[...] = x1
    h2 = x1 * (1.0 + sc2_ref[...]) + sh2_ref[...]
    if not moe:
        h2_ref[...] = h2.astype(BF16)
        return
    h2_ref[...] = h2

    first = jnp.logical_and(pl.program_id(0) == 0, pl.program_id(1) == 0)

    @pl.when(first)
    def _():
        carry_ref[...] = jnp.zeros(carry_ref.shape, F32)

    h2r = h2.reshape(rows, d)
    lane = lax.broadcasted_iota(I32, (rows, LANES), 1)
    lanef = lane.astype(F32)
    logits = jnp.full((rows, LANES), -jnp.inf, F32)
    for e in range(n_experts):
        col = jnp.sum(h2r * wr_ref[e:e + 1, :], axis=-1, keepdims=True)
        logits = jnp.where(lane == e, col, logits)
    m1 = jnp.max(logits, axis=-1, keepdims=True)
    i1 = jnp.min(jnp.where(logits == m1, lanef, float(LANES)), axis=-1, keepdims=True)
    rest = jnp.where(lanef == i1, -jnp.inf, logits)
    m2 = jnp.max(rest, axis=-1, keepdims=True)
    i2 = jnp.min(jnp.where(rest == m2, lanef, float(LANES)), axis=-1, keepdims=True)
    t2 = jnp.exp(m2 - m1)
    w1 = 1.0 / (1.0 + t2)
    w2 = t2 / (1.0 + t2)
    oh1 = lanef == i1
    oh2 = lanef == i2
    oh = jnp.logical_or(oh1, oh2).astype(F32)
    tri = (lax.broadcasted_iota(I32, (rows, rows), 0) >= lax.broadcasted_iota(I32, (rows, rows), 1)).astype(BF16)
    before = jnp.dot(tri, oh.astype(BF16), preferred_element_type=F32) - oh + carry_ref[...]
    r1 = jnp.sum(jnp.where(oh1, before, 0.0), axis=-1, keepdims=True)
    r2 = jnp.sum(jnp.where(oh2, before, 0.0), axis=-1, keepdims=True)
    carry = carry_ref[...] + jnp.sum(oh, axis=0, keepdims=True)
    carry_ref[...] = carry
    cnt_ref[...] = carry.astype(I32)
    slot = lax.broadcasted_iota(I32, (rows, 2), 1)
    eidx_ref[...] = jnp.where(slot == 0, i1, i2).astype(I32)
    rank_ref[...] = jnp.where(slot == 0, r1, r2).astype(I32)
    cw_ref[...] = jnp.where(slot == 0, w1, w2)


def _post(x, mod, layer, po, ao, w_gate, b_gate, w_pu, w_au, w_o, ln_g, ln_b, w_router_t, alpha):
    b, t, d = x.shape
    nb, tt = _row_block(b, t)
    rows = nb * tt
    assert nb == 1 or tt == t
    moe = w_router_t is not None
    n_experts = w_router_t.shape[0] if moe else 0
    n_tiles = (b // nb) * (t // tt)
    row3 = lambda w: pl.BlockSpec((nb, tt, w), lambda bi, ti: (bi, ti, 0))
    const2 = lambda a: _resident(a.shape, lambda bi, ti: (0, 0))
    tile2 = pl.BlockSpec((rows, 2), lambda bi, ti: (bi * (t // tt) + ti, 0))
    in_specs = [row3(d)] + [_mod_spec(nb, d, layer, c) for c in (0, 1, 2, 3, 4)] + [
        row3(po.shape[-1]), row3(ao.shape[-1]),
        const2(w_gate), _resident((1, 2 * d), lambda bi, ti: (0, 0)), const2(w_pu), const2(w_au), const2(w_o),
        _resident((1, d), lambda bi, ti: (0, 0)), _resident((1, d), lambda bi, ti: (0, 0)),
    ]
    args = [x, mod, mod, mod, mod, mod, po, ao, w_gate, b_gate.reshape(1, 2 * d), w_pu, w_au, w_o,
            ln_g.reshape(1, d), ln_b.reshape(1, d)]
    out_specs = [row3(d), row3(d)]
    out_shape = [jax.ShapeDtypeStruct((b, t, d), F32), jax.ShapeDtypeStruct((b, t, d), F32 if moe else BF16)]
    scratch = []
    if moe:
        in_specs.append(const2(w_router_t))
        args.append(w_router_t)
        out_specs += [tile2, tile2, tile2, pl.BlockSpec((1, LANES), lambda bi, ti: (0, 0))]
        out_shape += [
            jax.ShapeDtypeStruct((n_tiles * rows, 2), I32),
            jax.ShapeDtypeStruct((n_tiles * rows, 2), I32),
            jax.ShapeDtypeStruct((n_tiles * rows, 2), F32),
            jax.ShapeDtypeStruct((1, LANES), I32),
        ]
        scratch = [pltpu.VMEM((1, LANES), F32)]
    return pl.pallas_call(
        functools.partial(_post_kernel, alpha=alpha, n_experts=n_experts),
        grid=(b // nb, t // tt),
        in_specs=in_specs,
        out_specs=out_specs,
        out_shape=out_shape,
        scratch_shapes=scratch,
        compiler_params=_cparams(("arbitrary", "arbitrary")),
        name="post_moe" if moe else "post",
    )(*args)


def _ffn_kernel(x1_ref, h2_ref, g2_ref, wg_ref, wu_ref, wd_ref, lng_ref, lnb_ref, o_ref, *, alpha, n_chunks):
    nb, tt, d = x1_ref.shape
    rows = nb * tt
    hb = h2_ref[...].reshape(rows, d)
    f = wg_ref.shape[1]
    fc = f // n_chunks
    acc = None
    for c in range(n_chunks):
        sl = slice(c * fc, (c + 1) * fc)
        g = jnp.dot(hb, wg_ref[:, sl], preferred_element_type=F32)
        u = jnp.dot(hb, wu_ref[:, sl], preferred_element_type=F32)
        a = (g * jax.nn.sigmoid(g) * u).astype(BF16)
        part = jnp.dot(a, wd_ref[sl, :], preferred_element_type=F32)
        acc = part if acc is None else acc + part
    y = alpha * x1_ref[...] + g2_ref[...] * acc.reshape(nb, tt, d)
    o_ref[...] = _layer_norm(y, lng_ref[...], lnb_ref[...])


def _ffn_dense(x1, h2, mod, layer, wg, wu, wd, ln_g, ln_b, alpha):
    b, t, d = x1.shape
    nb, tt = _row_block(b, t)
    f = wg.shape[1]
    n_chunks = 2 if f % (2 * LANES) == 0 else 1
    row3 = pl.BlockSpec((nb, tt, d), lambda bi, ti: (bi, ti, 0))
    const2 = lambda a: _resident(a.shape, lambda bi, ti: (0, 0))
    vec = _resident((1, d), lambda bi, ti: (0, 0))
    return pl.pallas_call(
        functools.partial(_ffn_kernel, alpha=alpha, n_chunks=n_chunks),
        grid=(b // nb, t // tt),
        in_specs=[row3, row3, _mod_spec(nb, d, layer, 5), const2(wg), const2(wu), const2(wd), vec, vec],
        out_specs=row3,
        out_shape=jax.ShapeDtypeStruct((b, t, d), F32),
        compiler_params=_cparams(("arbitrary", "arbitrary")),
        name="ffn_dense",
    )(x1, h2, mod, wg, wu, wd, ln_g.reshape(1, d), ln_b.reshape(1, d))


def _moe_plan(counts, tm, n_tiles_max):
    tiles = (counts + tm - 1) // tm
    ends = jnp.cumsum(tiles)
    starts = ends - tiles
    n_active = ends[-1]
    tile_ids = jnp.minimum(jnp.arange(n_tiles_max, dtype=I32), jnp.maximum(n_active - 1, 0))
    tile_expert = jnp.sum((tile_ids[:, None] >= ends[None, :]).astype(I32), axis=1)
    tile_expert = jnp.minimum(tile_expert, counts.shape[0] - 1).astype(I32)
    last_tile_row = (starts + jnp.maximum(tiles - 1, 0)) * tm
    return (starts * tm).astype(I32), tile_expert, n_active.reshape(1).astype(I32), last_tile_row.astype(I32)


def _dispatch_kernel(off_ref, cnt_ref, last_ref, eidx_ref, rank_ref, h_ref, xs_ref, zero_ref, sem, zsem,
                     *, n_experts, tm, zrows):
    i = pl.program_id(0)
    npairs = eidx_ref.shape[2]
    rows = npairs // 2

    @pl.when(i == 0)
    def _():
        zero_ref[...] = jnp.zeros(zero_ref.shape, F32)
        for e in range(n_experts):
            @pl.when(cnt_ref[e] > 0)
            def _():
                for c in range(tm // zrows):
                    pltpu.make_async_copy(zero_ref, xs_ref.at[pl.ds(pl.multiple_of(last_ref[e] + c * zrows, zrows), zrows)], zsem).start()
        for e in range(n_experts):
            @pl.when(cnt_ref[e] > 0)
            def _():
                for c in range(tm // zrows):
                    pltpu.make_async_copy(zero_ref, xs_ref.at[pl.ds(pl.multiple_of(last_ref[e] + c * zrows, zrows), zrows)], zsem).wait()

    def start(j, carry):
        dst = off_ref[eidx_ref[0, 0, j]] + rank_ref[0, 0, j]
        tok = i * rows + j // 2
        pltpu.make_async_copy(h_ref.at[pl.ds(tok, 1)], xs_ref.at[pl.ds(dst, 1)], sem).start()
        return carry

    lax.fori_loop(0, npairs, start, 0)

    def wait(j, carry):
        pltpu.make_async_copy(h_ref.at[pl.ds(0, 1)], xs_ref.at[pl.ds(0, 1)], sem).wait()
        return carry

    lax.fori_loop(0, npairs, wait, 0)


def _dispatch(h2, eidx, rank, offsets, counts, last_tile_row, rows, tm, n_rows_total):
    n, d = h2.shape
    n_tiles = n // rows
    n_experts = counts.shape[0]
    zrows = min(tm, 256)
    smem = pl.BlockSpec(memory_space=pltpu.SMEM)
    pair = pl.BlockSpec((1, 1, 2 * rows), lambda i: (i, 0, 0), memory_space=pltpu.SMEM)
    return pl.pallas_call(
        functools.partial(_dispatch_kernel, n_experts=n_experts, tm=tm, zrows=zrows),
        grid=(n_tiles,),
        in_specs=[smem, smem, smem, pair, pair, pl.BlockSpec(memory_space=pl.ANY)],
        out_specs=pl.BlockSpec(memory_space=pl.ANY),
        out_shape=jax.ShapeDtypeStruct((n_rows_total, d), F32),
        scratch_shapes=[pltpu.VMEM((zrows, d), F32), pltpu.SemaphoreType.DMA(()), pltpu.SemaphoreType.DMA(())],
        compiler_params=_cparams(("arbitrary",)),
        name="moe_dispatch",
    )(offsets, counts, last_tile_row, eidx.reshape(n_tiles, 1, 2 * rows), rank.reshape(n_tiles, 1, 2 * rows), h2)


def _moe_ffn_kernel(te_ref, na_ref, xs_ref, wg_ref, wu_ref, wd_ref, y_ref, xb_ref, acc_ref):
    i = pl.program_id(0)
    j = pl.program_id(1)

    @pl.when(i < na_ref[0])
    def _():
        @pl.when(j == 0)
        def _():
            xb_ref[...] = xs_ref[...].astype(BF16)
            acc_ref[...] = jnp.zeros(acc_ref.shape, F32)

        xb = xb_ref[...]
        g = jnp.dot(xb, wg_ref[...], preferred_element_type=F32)
        u = jnp.dot(xb, wu_ref[...], preferred_element_type=F32)
        a = (g * jax.nn.sigmoid(g) * u).astype(BF16)
        acc_ref[...] += jnp.dot(a, wd_ref[...], preferred_element_type=F32)

        @pl.when(j == pl.num_programs(1) - 1)
        def _():
            y_ref[...] = acc_ref[...]


def _moe_ffn(xs, tile_expert, n_active, wg, wu, wd, tm):
    r, d = xs.shape
    f = wg.shape[2]
    fc = MOE_FC
    assert f % fc == 0 and r % tm == 0
    nj = f // fc

    def row_map(i, j, te, na):
        return (jnp.minimum(i, jnp.maximum(na[0] - 1, 0)), 0)

    def jj(i, j, na):
        return jnp.where(i < na[0], j, nj - 1)

    grid_spec = pltpu.PrefetchScalarGridSpec(
        num_scalar_prefetch=2,
        grid=(r // tm, nj),
        in_specs=[
            pl.BlockSpec((tm, d), row_map),
            pl.BlockSpec((None, d, fc), lambda i, j, te, na: (te[i], 0, jj(i, j, na))),
            pl.BlockSpec((None, d, fc), lambda i, j, te, na: (te[i], 0, jj(i, j, na))),
            pl.BlockSpec((None, fc, d), lambda i, j, te, na: (te[i], jj(i, j, na), 0)),
        ],
        out_specs=pl.BlockSpec((tm, d), row_map),
        scratch_shapes=[pltpu.VMEM((tm, d), BF16), pltpu.VMEM((tm, d), F32)],
    )
    return pl.pallas_call(
        _moe_ffn_kernel,
        grid_spec=grid_spec,
        out_shape=jax.ShapeDtypeStruct((r, d), F32),
        compiler_params=_cparams(("arbitrary", "arbitrary")),
        name="moe_ffn",
    )(tile_expert, n_active, xs, wg, wu, wd)


def _combine_kernel(off_ref, eidx_ref, rank_ref, y_ref, x1_ref, g2_ref, cw_ref, lng_ref, lnb_ref, o_ref, ybuf_ref, sem,
                    *, alpha):
    nb, tt, d = x1_ref.shape
    rows = nb * tt
    npairs = 2 * rows

    def start(j, carry):
        src = off_ref[eidx_ref[0, 0, j]] + rank_ref[0, 0, j]
        pltpu.make_async_copy(y_ref.at[pl.ds(src, 1)], ybuf_ref.at[j % 2, pl.ds(j // 2, 1)], sem).start()
        return carry

    lax.fori_loop(0, npairs, start, 0)

    def wait(j, carry):
        pltpu.make_async_copy(y_ref.at[pl.ds(0, 1)], ybuf_ref.at[0, pl.ds(0, 1)], sem).wait()
        return carry

    lax.fori_loop(0, npairs, wait, 0)
    cw = cw_ref[...]
    f = cw[:, 0:1] * ybuf_ref[0] + cw[:, 1:2] * ybuf_ref[1]
    y = alpha * x1_ref[...] + g2_ref[...] * f.reshape(nb, tt, d)
    o_ref[...] = _layer_norm(y, lng_ref[...], lnb_ref[...])


def _combine(y, x1, mod, layer, eidx, rank, cw, offsets, ln_g, ln_b, alpha):
    b, t, d = x1.shape
    nb, tt = _row_block(b, t)
    rows = nb * tt
    n_tiles = (b // nb) * (t // tt)
    tpb = t // tt
    row3 = pl.BlockSpec((nb, tt, d), lambda bi, ti: (bi, ti, 0))
    pair = pl.BlockSpec((1, 1, 2 * rows), lambda bi, ti: (bi * tpb + ti, 0, 0), memory_space=pltpu.SMEM)
    vec = _resident((1, d), lambda bi, ti: (0, 0))
    return pl.pallas_call(
        functools.partial(_combine_kernel, alpha=alpha),
        grid=(b // nb, tpb),
        in_specs=[
            pl.BlockSpec(memory_space=pltpu.SMEM), pair, pair, pl.BlockSpec(memory_space=pl.ANY), row3,
            _mod_spec(nb, d, layer, 5), pl.BlockSpec((rows, 2), lambda bi, ti: (bi * tpb + ti, 0)), vec, vec,
        ],
        out_specs=row3,
        out_shape=jax.ShapeDtypeStruct((b, t, d), F32),
        scratch_shapes=[pltpu.VMEM((2, rows, d), F32), pltpu.SemaphoreType.DMA(())],
        compiler_params=_cparams(("arbitrary", "arbitrary")),
        name="moe_combine",
    )(offsets, eidx.reshape(n_tiles, 1, 2 * rows), rank.reshape(n_tiles, 1, 2 * rows), y, x1, mod, cw,
      ln_g.reshape(1, d), ln_b.reshape(1, d))


def _moe(x1, h2, mod, layer, eidx, rank, cw, counts, wg, wu, wd, ln_g, ln_b, alpha):
    b, t, d = x1.shape
    n = b * t
    nb, tt = _row_block(b, t)
    rows = nb * tt
    n_experts = wg.shape[0]
    tm = MOE_TM_LARGE if 2 * n >= 8 * n_experts * MOE_TM_LARGE else MOE_TM_SMALL
    n_tiles_max = (2 * n) // tm + n_experts
    offsets, tile_expert, n_active, last_tile_row = _moe_plan(counts, tm, n_tiles_max)
    xs = _dispatch(h2.reshape(n, d), eidx, rank, offsets, counts, last_tile_row, rows, tm, n_tiles_max * tm)
    y = _moe_ffn(xs, tile_expert, n_active, wg, wu, wd, tm)
    return _combine(y, x1, mod, layer, eidx, rank, cw, offsets, ln_g, ln_b, alpha)


def _trunk(x, mod, pos0, pool_hist, k_past, v_past, biases, W, depth):
    b, t, d = x.shape
    n_heads = d // HEAD_WIDTH
    qkw = n_heads * HEAD_WIDTH
    alpha = (2.0 * depth) ** 0.25
    k5 = v5 = None
    pool_states = []
    o4 = POOL_WIDTH + 3 * qkw
    for l in range(depth):
        lam_init = 0.8 - 0.6 * math.exp(-0.3 * l)
        w_a = W["w_in"][l, :, :o4].astype(BF16)
        w_gate = W["w_in"][l, :, o4:].astype(BF16)
        hist = None if pool_hist is None else pool_hist[l]
        q, kb, vb, k5, v5, po, pst = _inproj(
            x, mod, l, w_a, W["pool_w"][l].astype(BF16), W["pool_scale"][l], hist, k5, v5, depth, pos0)
        pool_states.append(pst)
        if k_past is None:
            ao = _attn_prompt(q, kb, vb, W["lam_qk"][l], W["subln_g"][l], biases[0], biases[1], lam_init)
        else:
            ao = _attn_sample(q, kb, vb, k_past, v_past, l, W["lam_qk"][l], W["subln_g"][l], biases[0], biases[1],
                              lam_init)
        moe = l % 2 == 1
        i = l // 2
        w_router_t = jnp.transpose(W["w_router"][i]) if moe else None
        outs = _post(x, mod, l, po, ao, w_gate, W["b_gate"][l], W["w_pool_up"][l].astype(BF16),
                     W["w_attn_up"][l].astype(BF16), W["w_o"][l].astype(BF16), W["ln_g"][l, 0], W["ln_b"][l, 0],
                     w_router_t, alpha)
        if not moe:
            x1, h2 = outs
            x = _ffn_dense(x1, h2, mod, l, W["w_ffn_gate"][i].astype(BF16), W["w_ffn_up"][i].astype(BF16),
                           W["w_ffn_down"][i].astype(BF16), W["ln_g"][l, 1], W["ln_b"][l, 1], alpha)
        else:
            x1, h2, eidx, rank, cw, cnt = outs
            n_experts = W["w_exp_gate"].shape[1]
            x = _moe(x1, h2, mod, l, eidx, rank, cw, cnt[0, :n_experts], W["w_exp_gate"][i].astype(BF16),
                     W["w_exp_up"][i].astype(BF16), W["w_exp_down"][i].astype(BF16), W["ln_g"][l, 1], W["ln_b"][l, 1],
                     alpha)
    return x, k5, v5, jnp.stack(pool_states)


def kernel(x_prompt, x_sample, cache_k, cache_v, state_pool, c_prompt, c_sample, rel_bias, w_ada, b_ada, w_in, b_gate,
           pool_w, pool_scale, lam_qk, subln_g, w_pool_up, w_attn_up, w_o, ln_g, ln_b, w_ffn_gate, w_ffn_up,
           w_ffn_down, w_router, w_exp_gate, w_exp_up, w_exp_down):
    W = dict(w_in=w_in, b_gate=b_gate, pool_w=pool_w, pool_scale=pool_scale, lam_qk=lam_qk, subln_g=subln_g,
             w_pool_up=w_pool_up, w_attn_up=w_attn_up, w_o=w_o, ln_g=ln_g, ln_b=ln_b, w_ffn_gate=w_ffn_gate,
             w_ffn_up=w_ffn_up, w_ffn_down=w_ffn_down, w_router=w_router, w_exp_gate=w_exp_gate, w_exp_up=w_exp_up,
             w_exp_down=w_exp_down)
    depth, d, _ = w_in.shape
    bp, s, _ = x_prompt.shape
    bs, ts, _ = x_sample.shape
    p = cache_k.shape[2]

    mod = _ada(jnp.concatenate([c_prompt, c_sample], axis=0), w_ada, b_ada)
    mod_p = mod[:, :bp].reshape(depth, bp, 1, 6 * d)
    mod_s = mod[:, bp:].reshape(depth, bs, 1, 6 * d)

    tq = ATTN_TQ
    q_pos = np.arange(tq, 2 * tq)
    bias_prev = _bias_table(rel_bias, _bucket_table(q_pos, np.arange(0, tq)))
    bias_diag = _bias_table(rel_bias, _bucket_table(q_pos, q_pos))
    s_pos = p + np.arange(ts)
    bias_past = _bias_table(rel_bias, _bucket_table(s_pos, np.arange(p)))
    bias_new = _bias_table(rel_bias, _bucket_table(s_pos, s_pos))

    y_p, k_p, v_p, pool_p = _trunk(x_prompt, mod_p, 0, None, None, None, (bias_prev, bias_diag), W, depth)
    y_s, k_s, v_s, pool_s = _trunk(x_sample, mod_s, p, state_pool, cache_k, cache_v, (bias_past, bias_new), W, depth)
    return (y_p, y_s, k_p, v_p, pool_p, k_s, v_s, pool_s)
```

```python
import functools
import math

import numpy as np
import jax
import jax.numpy as jnp
from jax import lax
from jax.experimental import pallas as pl
from jax.experimental.pallas import tpu as pltpu

F32 = jnp.float32
BF16 = jnp.bfloat16
I32 = jnp.int32

CHUNK = 64
HEAD_DIM = 64
HEAD_WIDTH = 2 * HEAD_DIM
POOL_WINDOWS = (2, 4, 8, 16)
POOL_GROUP_DIM = 128
POOL_WIDTH = len(POOL_WINDOWS) * POOL_GROUP_DIM
POOL_HIST = max(POOL_WINDOWS) - 1
HIST_ROWS = 16
N_BUCKETS = 32
MAX_DISTANCE = 128
FAR_BUCKET = N_BUCKETS // 2 - 1
LN_EPS = 1e-5
RMS_EPS = 1e-5
MASKED = -1e30
LOG2E = math.log2(math.e)

LANES = 128
ROW_TILE = 512
ATTN_TQ = 256
ATTN_TK = 256
ATTN_HEADS_PER_STEP = 4
DMA_UNROLL = 8
MOE_TM_LARGE = 1024
MOE_TM_SMALL = 256
MOE_FC = 512
VMEM_LIMIT = 56 * 1024 * 1024


def _cparams(sem):
    return pltpu.CompilerParams(dimension_semantics=sem, vmem_limit_bytes=VMEM_LIMIT)


def _resident(shape, index_map):
    return pl.BlockSpec(shape, index_map, pipeline_mode=pl.Buffered(1))


def _row_block(b, t):
    if t >= ROW_TILE:
        assert t % ROW_TILE == 0
        return 1, ROW_TILE
    nb = min(b, ROW_TILE // t)
    assert b % nb == 0 and t % 8 == 0
    return nb, t


def _layer_norm(y, g, b):
    mu = jnp.mean(y, axis=-1, keepdims=True)
    yc = y - mu
    var = jnp.mean(yc * yc, axis=-1, keepdims=True)
    return yc * lax.rsqrt(var + LN_EPS) * g + b


def _ada_kernel(c_ref, w_ref, b_ref, o_ref):
    c = c_ref[...]
    s = c * jax.nn.sigmoid(c)
    o_ref[0] = jnp.dot(s.astype(BF16), w_ref[0].astype(BF16), preferred_element_type=F32) + b_ref[0]


def _ada(c_all, w_ada, b_ada):
    depth, d, n6 = w_ada.shape
    bc = c_all.shape[0]
    tn = 1536 if n6 % 1536 == 0 else n6
    return pl.pallas_call(
        _ada_kernel,
        grid=(depth, n6 // tn),
        in_specs=[
            pl.BlockSpec((bc, d), lambda l, j: (0, 0)),
            pl.BlockSpec((1, d, tn), lambda l, j: (l, 0, j)),
            pl.BlockSpec((1, 1, tn), lambda l, j: (l, 0, j)),
        ],
        out_specs=pl.BlockSpec((1, bc, tn), lambda l, j: (l, 0, j)),
        out_shape=jax.ShapeDtypeStruct((depth, bc, n6), F32),
        compiler_params=_cparams(("arbitrary", "arbitrary")),
        name="ada",
    )(c_all, w_ada, b_ada.reshape(depth, 1, n6))


def _mod_spec(nb, d, layer, chunk):
    return pl.BlockSpec((None, nb, 1, d), lambda bi, ti: (layer, bi, 0, chunk))


def _rel_bucket_np(rel):
    nb = N_BUCKETS // 2
    max_exact = nb // 2
    n = np.abs(rel)
    large = max_exact + (
        np.log(np.maximum(n, 1).astype(np.float32) / np.float32(max_exact))
        / np.float32(math.log(MAX_DISTANCE / max_exact))
        * np.float32(nb - max_exact)
    ).astype(np.int32)
    large = np.minimum(large, nb - 1)
    return np.where(rel > 0, nb, 0) + np.where(n < max_exact, n, large)


def _bucket_table(q_pos, k_pos):
    rel = k_pos[None, :] - q_pos[:, None]
    allowed = (k_pos[None, :] // CHUNK) <= (q_pos[:, None] // CHUNK)
    return np.where(allowed, _rel_bucket_np(rel), -1).astype(np.int32)


def _bias_kernel(rb_ref, bk_ref, o_ref):
    h = pl.program_id(0)
    bk = bk_ref[...]
    far = rb_ref[FAR_BUCKET, h]
    acc = jnp.zeros(bk.shape, F32)
    for b in range(N_BUCKETS):
        acc = jnp.where(bk == b, (rb_ref[b, h] - far) * LOG2E, acc)
    o_ref[0] = jnp.where(bk < 0, MASKED, acc)


def _bias_table(rel_bias, bucket_np):
    rows, cols = bucket_np.shape
    n_heads = rel_bias.shape[1]
    return pl.pallas_call(
        _bias_kernel,
        grid=(n_heads,),
        in_specs=[
            pl.BlockSpec(memory_space=pltpu.SMEM),
            pl.BlockSpec((rows, cols), lambda h: (0, 0)),
        ],
        out_specs=pl.BlockSpec((1, rows, cols), lambda h: (h, 0, 0)),
        out_shape=jax.ShapeDtypeStruct((n_heads, rows, cols), F32),
        compiler_params=_cparams(("arbitrary",)),
        name="bias_table",
    )(rel_bias, jnp.asarray(bucket_np))


def _inproj_kernel(*refs, pos0, has_hist, has_alias, n_heads):
    it = iter(refs)
    x_ref, sh_ref, sc_ref, w_ref, pw_ref, ps_ref = (next(it) for _ in range(6))
    hist_ref = next(it) if has_hist else None
    if has_alias:
        next(it)
        next(it)
    q_ref, kb_ref, vb_ref, k_ref, v_ref, po_ref, pst_ref, ext_ref = (next(it) for _ in range(8))

    ti = pl.program_id(1)
    nb, tt, d = x_ref.shape
    rows = nb * tt
    qkw = n_heads * HEAD_WIDTH
    o1 = POOL_WIDTH
    o2 = o1 + qkw
    o3 = o2 + qkw
    o4 = o3 + qkw

    h = x_ref[...] * (1.0 + sc_ref[...]) + sh_ref[...]
    hb = h.reshape(rows, d).astype(BF16)

    u = jnp.dot(hb, w_ref[:, 0:o1], preferred_element_type=F32)
    q = jnp.dot(hb, w_ref[:, o1:o2], preferred_element_type=F32)
    q_ref[...] = (q * (HEAD_DIM ** -0.5 * LOG2E)).astype(BF16).reshape(nb, tt, qkw)
    k = jnp.dot(hb, w_ref[:, o2:o3], preferred_element_type=F32)
    kb_ref[...] = k.astype(BF16).reshape(nb, tt, qkw)
    v = jnp.dot(hb, w_ref[:, o3:o4], preferred_element_type=F32)
    vb_ref[...] = v.astype(BF16).reshape(nb, tt, qkw)
    for hh in range(n_heads):
        sl = slice(hh * HEAD_WIDTH, (hh + 1) * HEAD_WIDTH)
        k_ref[:, :, hh, :] = k[:, sl].reshape(nb, tt, HEAD_WIDTH)
        v_ref[:, :, hh, :] = v[:, sl].reshape(nb, tt, HEAD_WIDTH)

    @pl.when(ti == 0)
    def _():
        if has_hist:
            ext_ref[:, 0:1, :] = jnp.zeros((nb, 1, POOL_WIDTH), F32)
            ext_ref[:, 1:HIST_ROWS, :] = hist_ref[...]
        else:
            ext_ref[:, 0:HIST_ROWS, :] = jnp.zeros((nb, HIST_ROWS, POOL_WIDTH), F32)

    ext_ref[:, HIST_ROWS:HIST_ROWS + tt, :] = u.reshape(nb, tt, POOL_WIDTH)
    pos = pos0 + ti * tt + lax.broadcasted_iota(I32, (1, tt, 1), 1)
    outs = []
    for gi, w in enumerate(POOL_WINDOWS):
        ls = slice(gi * POOL_GROUP_DIM, (gi + 1) * POOL_GROUP_DIM)
        tot = ext_ref[:, HIST_ROWS:HIST_ROWS + tt, ls]
        for j in range(1, w):
            tot = tot + ext_ref[:, HIST_ROWS - j:HIST_ROWS - j + tt, ls]
        cnt = jnp.minimum(pos + 1, w).astype(F32)
        dlt = (tot / cnt - ext_ref[:, HIST_ROWS:HIST_ROWS + tt, ls]).reshape(rows, POOL_GROUP_DIM)
        outs.append(jnp.dot(dlt.astype(BF16), pw_ref[gi], preferred_element_type=F32))
    po = jnp.concatenate(outs, axis=-1) * ps_ref[...]
    po_ref[...] = po.astype(BF16).reshape(nb, tt, POOL_WIDTH)
    pst_ref[...] = ext_ref[:, tt + 1:tt + HIST_ROWS, :]
    ext_ref[:, 0:HIST_ROWS, :] = ext_ref[:, tt:tt + HIST_ROWS, :]


def _inproj(x, mod, layer, w_a, pool_w, pool_scale, hist, k5, v5, depth, pos0):
    b, t, d = x.shape
    nb, tt = _row_block(b, t)
    n_heads = d // HEAD_WIDTH
    qkw = n_heads * HEAD_WIDTH
    has_hist = hist is not None
    has_alias = k5 is not None
    row3 = lambda w: pl.BlockSpec((nb, tt, w), lambda bi, ti: (bi, ti, 0))
    cache_spec = pl.BlockSpec((None, nb, tt, n_heads, HEAD_WIDTH), lambda bi, ti: (layer, bi, ti, 0, 0))
    in_specs = [
        row3(d),
        _mod_spec(nb, d, layer, 0),
        _mod_spec(nb, d, layer, 1),
        _resident(w_a.shape, lambda bi, ti: (0, 0)),
        _resident(pool_w.shape, lambda bi, ti: (0, 0, 0)),
        _resident((1, POOL_WIDTH), lambda bi, ti: (0, 0)),
    ]
    args = [x, mod, mod, w_a, pool_w, pool_scale.reshape(1, POOL_WIDTH)]
    if has_hist:
        in_specs.append(pl.BlockSpec((nb, POOL_HIST, POOL_WIDTH), lambda bi, ti: (bi, 0, 0)))
        args.append(hist)
    aliases = {}
    if has_alias:
        aliases = {len(args): 3, len(args) + 1: 4}
        in_specs += [pl.BlockSpec(memory_space=pl.ANY), pl.BlockSpec(memory_space=pl.ANY)]
        args += [k5, v5]
    cache_shape = jax.ShapeDtypeStruct((depth, b, t, n_heads, HEAD_WIDTH), F32)
    outs = pl.pallas_call(
        functools.partial(_inproj_kernel, pos0=pos0, has_hist=has_hist, has_alias=has_alias, n_heads=n_heads),
        grid=(b // nb, t // tt),
        in_specs=in_specs,
        out_specs=[
            row3(qkw), row3(qkw), row3(qkw), cache_spec, cache_spec, row3(POOL_WIDTH),
            pl.BlockSpec((nb, POOL_HIST, POOL_WIDTH), lambda bi, ti: (bi, 0, 0)),
        ],
        out_shape=[
            jax.ShapeDtypeStruct((b, t, qkw), BF16),
            jax.ShapeDtypeStruct((b, t, qkw), BF16),
            jax.ShapeDtypeStruct((b, t, qkw), BF16),
            cache_shape, cache_shape,
            jax.ShapeDtypeStruct((b, t, POOL_WIDTH), BF16),
            jax.ShapeDtypeStruct((b, POOL_HIST, POOL_WIDTH), F32),
        ],
        scratch_shapes=[pltpu.VMEM((nb, tt + HIST_ROWS, POOL_WIDTH), F32)],
        input_output_aliases=aliases,
        compiler_params=_cparams(("arbitrary", "arbitrary")),
        name="inproj",
    )(*args)
    return outs


def _lambda(lam_ref, lam_init):
    lq = lam_ref[...]
    s01 = jnp.sum(lq[0:1, :] * lq[1:2, :], axis=-1, keepdims=True)
    s23 = jnp.sum(lq[2:3, :] * lq[3:4, :], axis=-1, keepdims=True)
    return jnp.exp(s01) - jnp.exp(s23) + lam_init


def _stack_maps(q):
    lane = lax.broadcasted_iota(I32, q.shape, 1)
    zero = jnp.zeros_like(q)
    return jnp.concatenate([jnp.where(lane < HEAD_DIM, q, zero), jnp.where(lane >= HEAD_DIM, q, zero)], axis=0)


def _diff_out(acc, l, lam, g, lam_init, rows):
    o = acc / l
    o = o[:rows] - lam * o[rows:]
    o = o * lax.rsqrt(jnp.mean(o * o, axis=-1, keepdims=True) + RMS_EPS) * g * (1.0 - lam_init)
    return o.astype(BF16)


def _nt_dot(a, b):
    return lax.dot_general(a, b, (((1,), (1,)), ((), ())), preferred_element_type=F32)


def _attn_prompt_kernel(lam_ref, g_ref, q_ref, k_ref, v_ref, bprev_ref, bdiag_ref, o_ref, qs_ref, m_ref, l_ref,
                        acc_ref, *, lam_init, hg):
    qi = pl.program_id(2)
    tq = q_ref.shape[1]
    tk = ATTN_TK
    for h in range(hg):
        qs_ref[h] = _stack_maps(q_ref[0, :, h * HEAD_WIDTH:(h + 1) * HEAD_WIDTH])
    m_ref[...] = jnp.full(m_ref.shape, MASKED, F32)
    l_ref[...] = jnp.zeros(l_ref.shape, F32)
    acc_ref[...] = jnp.zeros(acc_ref.shape, F32)

    def update(j, bias_ref):
        start = pl.multiple_of(j * tk, tk)
        for h in range(hg):
            sl = slice(h * HEAD_WIDTH, (h + 1) * HEAD_WIDTH)
            s = _nt_dot(qs_ref[h], k_ref[0, pl.ds(start, tk), sl])
            if bias_ref is not None:
                bias = bias_ref[h]
                s = s + jnp.concatenate([bias, bias], axis=0)
            m_prev = m_ref[h]
            m_new = jnp.maximum(m_prev, jnp.max(s, axis=-1, keepdims=True))
            alpha = jnp.exp2(m_prev - m_new)
            p = [jnp.exp2(s[:, c * LANES:(c + 1) * LANES] - m_new) for c in range(tk // LANES)]
            l_ref[h] = alpha * l_ref[h] + functools.reduce(lambda a, b: a + b, p)
            pb = jnp.concatenate([x.astype(BF16) for x in p], axis=-1)
            acc_ref[h] = alpha * acc_ref[h] + jnp.dot(
                pb, v_ref[0, pl.ds(start, tk), sl], preferred_element_type=F32)
            m_ref[h] = m_new

    def far_body(j, carry):
        update(j, None)
        return carry

    lax.fori_loop(0, jnp.maximum(qi - 1, 0), far_body, 0)

    @pl.when(qi >= 1)
    def _():
        update(qi - 1, bprev_ref)

    update(qi, bdiag_ref)
    lam = _lambda(lam_ref, lam_init)
    for h in range(hg):
        l = jnp.sum(l_ref[h], axis=-1, keepdims=True)
        o_ref[0, :, h * HEAD_WIDTH:(h + 1) * HEAD_WIDTH] = _diff_out(acc_ref[h], l, lam, g_ref[...], lam_init, tq)


def _attn_prompt(q, kb, vb, lam_qk, subln_g, bias_prev, bias_diag, lam_init):
    b, t, qkw = q.shape
    n_heads = qkw // HEAD_WIDTH
    tq = ATTN_TQ
    hg = math.gcd(n_heads, ATTN_HEADS_PER_STEP)
    gw = hg * HEAD_WIDTH
    assert tq == ATTN_TK and t % tq == 0
    return pl.pallas_call(
        functools.partial(_attn_prompt_kernel, lam_init=lam_init, hg=hg),
        grid=(b, n_heads // hg, t // tq),
        in_specs=[
            pl.BlockSpec(lam_qk.shape, lambda bi, g, qi: (0, 0)),
            pl.BlockSpec((1, HEAD_WIDTH), lambda bi, g, qi: (0, 0)),
            pl.BlockSpec((1, tq, gw), lambda bi, g, qi: (bi, qi, g)),
            pl.BlockSpec((1, t, gw), lambda bi, g, qi: (bi, 0, g)),
            pl.BlockSpec((1, t, gw), lambda bi, g, qi: (bi, 0, g)),
            pl.BlockSpec((hg, tq, ATTN_TK), lambda bi, g, qi: (g, 0, 0)),
            pl.BlockSpec((hg, tq, ATTN_TK), lambda bi, g, qi: (g, 0, 0)),
        ],
        out_specs=pl.BlockSpec((1, tq, gw), lambda bi, g, qi: (bi, qi, g)),
        out_shape=jax.ShapeDtypeStruct((b, t, qkw), BF16),
        scratch_shapes=[
            pltpu.VMEM((hg, 2 * tq, HEAD_WIDTH), BF16),
            pltpu.VMEM((hg, 2 * tq, LANES), F32),
            pltpu.VMEM((hg, 2 * tq, LANES), F32),
            pltpu.VMEM((hg, 2 * tq, HEAD_WIDTH), F32),
        ],
        compiler_params=_cparams(("arbitrary", "arbitrary", "arbitrary")),
        name="attn_prompt",
    )(lam_qk, subln_g.reshape(1, HEAD_WIDTH), q, kb, vb, bias_prev, bias_diag)


def _attn_sample_kernel(lam_ref, g_ref, q_ref, kn_ref, vn_ref, kp_ref, vp_ref, bpast_ref, bnew_ref, o_ref,
                        *, lam_init, n_heads):
    ts = q_ref.shape[1]
    lam = _lambda(lam_ref, lam_init)
    for hh in range(n_heads):
        sl = slice(hh * HEAD_WIDTH, (hh + 1) * HEAD_WIDTH)
        qs = _stack_maps(q_ref[0, :, sl])
        kp = kp_ref[0, :, hh, :].astype(BF16)
        vp = vp_ref[0, :, hh, :].astype(BF16)
        bp = bpast_ref[hh]
        bn = bnew_ref[hh]
        s_p = _nt_dot(qs, kp) + jnp.concatenate([bp, bp], axis=0)
        s_n = _nt_dot(qs, kn_ref[0, :, sl]) + jnp.concatenate([bn, bn], axis=0)
        m = jnp.maximum(jnp.max(s_p, axis=-1, keepdims=True), jnp.max(s_n, axis=-1, keepdims=True))
        p_p = jnp.exp2(s_p - m)
        p_n = jnp.exp2(s_n - m)
        l = jnp.sum(p_p, axis=-1, keepdims=True) + jnp.sum(p_n, axis=-1, keepdims=True)
        acc = jnp.dot(p_p.astype(BF16), vp, preferred_element_type=F32) + jnp.dot(
            p_n.astype(BF16), vn_ref[0, :, sl], preferred_element_type=F32)
        o_ref[0, :, sl] = _diff_out(acc, l, lam, g_ref[...], lam_init, ts)


def _attn_sample(q, kb, vb, k_past, v_past, layer, lam_qk, subln_g, bias_past, bias_new, lam_init):
    b, ts, qkw = q.shape
    n_heads = qkw // HEAD_WIDTH
    p = k_past.shape[2]
    row = pl.BlockSpec((1, ts, qkw), lambda bi: (bi, 0, 0))
    past = pl.BlockSpec((None, 1, p, n_heads, HEAD_WIDTH), lambda bi: (layer, bi, 0, 0, 0))
    return pl.pallas_call(
        functools.partial(_attn_sample_kernel, lam_init=lam_init, n_heads=n_heads),
        grid=(b,),
        in_specs=[
            pl.BlockSpec(lam_qk.shape, lambda bi: (0, 0)),
            pl.BlockSpec((1, HEAD_WIDTH), lambda bi: (0, 0)),
            row, row, row, past, past,
            pl.BlockSpec(bias_past.shape, lambda bi: (0, 0, 0)),
            pl.BlockSpec(bias_new.shape, lambda bi: (0, 0, 0)),
        ],
        out_specs=row,
        out_shape=jax.ShapeDtypeStruct((b, ts, qkw), BF16),
        compiler_params=_cparams(("arbitrary",)),
        name="attn_sample",
    )(lam_qk, subln_g.reshape(1, HEAD_WIDTH), q, kb, vb, k_past, v_past, bias_past, bias_new)


def _post_kernel(*refs, alpha, n_experts):
    moe = n_experts > 0
    it = iter(refs)
    (x_ref, sh1_ref, sc1_ref, g1_ref, sh2_ref, sc2_ref, po_ref, ao_ref, wgate_ref, bgate_ref, wpu_ref, wau_ref,
     wo_ref, lng_ref, lnb_ref) = (next(it) for _ in range(15))
    wr_ref = next(it) if moe else None
    x1_ref = next(it)
    h2_ref = next(it)
    if moe:
        eidx_ref, rank_ref, cw_ref, cnt_ref, carry_ref = (next(it) for _ in range(5))

    nb, tt, d = x_ref.shape
    rows = nb * tt
    x = x_ref[...]
    hb = (x * (1.0 + sc1_ref[...]) + sh1_ref[...]).reshape(rows, d).astype(BF16)
    gates = jax.nn.sigmoid(jnp.dot(hb, wgate_ref[...], preferred_element_type=F32) + bgate_ref[...])
    pu = jnp.dot(po_ref[...].reshape(rows, -1), wpu_ref[...], preferred_element_type=F32)
    au = jnp.dot(ao_ref[...].reshape(rows, -1), wau_ref[...], preferred_element_type=F32)
    merged = gates[:, :d] * pu + gates[:, d:] * au
    mix = jnp.dot(merged.astype(BF16), wo_ref[...], preferred_element_type=F32)
    y = alpha * x + g1_ref[...] * mix.reshape(nb, tt, d)
    x1 = _layer_norm(y, lng_ref[...], lnb_ref[...])
    x1_ref[...] = x1
    h2 = x1 * (1.0 + sc2_ref[...]) + sh2_ref[...]
    if not moe:
        h2_ref[...] = h2.astype(BF16)
        return
    h2_ref[...] = h2

    first = jnp.logical_and(pl.program_id(0) == 0, pl.program_id(1) == 0)

    @pl.when(first)
    def _():
        carry_ref[...] = jnp.zeros(carry_ref.shape, F32)

    h2r = h2.reshape(rows, d)
    lane = lax.broadcasted_iota(I32, (rows, LANES), 1)
    lanef = lane.astype(F32)
    logits = jnp.full((rows, LANES), -jnp.inf, F32)
    for e in range(n_experts):
        col = jnp.sum(h2r * wr_ref[e:e + 1, :], axis=-1, keepdims=True)
        logits = jnp.where(lane == e, col, logits)
    m1 = jnp.max(logits, axis=-1, keepdims=True)
    i1 = jnp.min(jnp.where(logits == m1, lanef, float(LANES)), axis=-1, keepdims=True)
    rest = jnp.where(lanef == i1, -jnp.inf, logits)
    m2 = jnp.max(rest, axis=-1, keepdims=True)
    i2 = jnp.min(jnp.where(rest == m2, lanef, float(LANES)), axis=-1, keepdims=True)
    t2 = jnp.exp(m2 - m1)
    w1 = 1.0 / (1.0 + t2)
    w2 = t2 / (1.0 + t2)
    oh1 = lanef == i1
    oh2 = lanef == i2
    oh = jnp.logical_or(oh1, oh2).astype(F32)
    tri = (lax.broadcasted_iota(I32, (rows, rows), 0) >= lax.broadcasted_iota(I32, (rows, rows), 1)).astype(BF16)
    before = jnp.dot(tri, oh.astype(BF16), preferred_element_type=F32) - oh + carry_ref[...]
    r1 = jnp.sum(jnp.where(oh1, before, 0.0), axis=-1, keepdims=True)
    r2 = jnp.sum(jnp.where(oh2, before, 0.0), axis=-1, keepdims=True)
    carry = carry_ref[...] + jnp.sum(oh, axis=0, keepdims=True)
    carry_ref[...] = carry
    cnt_ref[...] = carry.astype(I32)
    slot = lax.broadcasted_iota(I32, (rows, 2), 1)
    eidx_ref[...] = jnp.where(slot == 0, i1, i2).astype(I32)
    rank_ref[...] = jnp.where(slot == 0, r1, r2).astype(I32)
    cw_ref[...] = jnp.where(slot == 0, w1, w2)


def _post(x, mod, layer, po, ao, w_gate, b_gate, w_pu, w_au, w_o, ln_g, ln_b, w_router_t, alpha):
    b, t, d = x.shape
    nb, tt = _row_block(b, t)
    rows = nb * tt
    assert nb == 1 or tt == t
    moe = w_router_t is not None
    n_experts = w_router_t.shape[0] if moe else 0
    n_tiles = (b // nb) * (t // tt)
    row3 = lambda w: pl.BlockSpec((nb, tt, w), lambda bi, ti: (bi, ti, 0))
    const2 = lambda a: _resident(a.shape, lambda bi, ti: (0, 0))
    tile2 = pl.BlockSpec((rows, 2), lambda bi, ti: (bi * (t // tt) + ti, 0))
    in_specs = [row3(d)] + [_mod_spec(nb, d, layer, c) for c in (0, 1, 2, 3, 4)] + [
        row3(po.shape[-1]), row3(ao.shape[-1]),
        const2(w_gate), _resident((1, 2 * d), lambda bi, ti: (0, 0)), const2(w_pu), const2(w_au), const2(w_o),
        _resident((1, d), lambda bi, ti: (0, 0)), _resident((1, d), lambda bi, ti: (0, 0)),
    ]
    args = [x, mod, mod, mod, mod, mod, po, ao, w_gate, b_gate.reshape(1, 2 * d), w_pu, w_au, w_o,
            ln_g.reshape(1, d), ln_b.reshape(1, d)]
    out_specs = [row3(d), row3(d)]
    out_shape = [jax.ShapeDtypeStruct((b, t, d), F32), jax.ShapeDtypeStruct((b, t, d), F32 if moe else BF16)]
    scratch = []
    if moe:
        in_specs.append(const2(w_router_t))
        args.append(w_router_t)
        out_specs += [tile2, tile2, tile2, pl.BlockSpec((1, LANES), lambda bi, ti: (0, 0))]
        out_shape += [
            jax.ShapeDtypeStruct((n_tiles * rows, 2), I32),
            jax.ShapeDtypeStruct((n_tiles * rows, 2), I32),
            jax.ShapeDtypeStruct((n_tiles * rows, 2), F32),
            jax.ShapeDtypeStruct((1, LANES), I32),
        ]
        scratch = [pltpu.VMEM((1, LANES), F32)]
    return pl.pallas_call(
        functools.partial(_post_kernel, alpha=alpha, n_experts=n_experts),
        grid=(b // nb, t // tt),
        in_specs=in_specs,
        out_specs=out_specs,
        out_shape=out_shape,
        scratch_shapes=scratch,
        compiler_params=_cparams(("arbitrary", "arbitrary")),
        name="post_moe" if moe else "post",
    )(*args)


def _ffn_kernel(x1_ref, h2_ref, g2_ref, wg_ref, wu_ref, wd_ref, lng_ref, lnb_ref, o_ref, *, alpha, n_chunks):
    nb, tt, d = x1_ref.shape
    rows = nb * tt
    hb = h2_ref[...].reshape(rows, d)
    f = wg_ref.shape[1]
    fc = f // n_chunks
    acc = None
    for c in range(n_chunks):
        sl = slice(c * fc, (c + 1) * fc)
        g = jnp.dot(hb, wg_ref[:, sl], preferred_element_type=F32)
        u = jnp.dot(hb, wu_ref[:, sl], preferred_element_type=F32)
        a = (g * jax.nn.sigmoid(g) * u).astype(BF16)
        part = jnp.dot(a, wd_ref[sl, :], preferred_element_type=F32)
        acc = part if acc is None else acc + part
    y = alpha * x1_ref[...] + g2_ref[...] * acc.reshape(nb, tt, d)
    o_ref[...] = _layer_norm(y, lng_ref[...], lnb_ref[...])


def _ffn_dense(x1, h2, mod, layer, wg, wu, wd, ln_g, ln_b, alpha):
    b, t, d = x1.shape
    nb, tt = _row_block(b, t)
    f = wg.shape[1]
    n_chunks = 2 if f % (2 * LANES) == 0 else 1
    row3 = pl.BlockSpec((nb, tt, d), lambda bi, ti: (bi, ti, 0))
    const2 = lambda a: _resident(a.shape, lambda bi, ti: (0, 0))
    vec = _resident((1, d), lambda bi, ti: (0, 0))
    return pl.pallas_call(
        functools.partial(_ffn_kernel, alpha=alpha, n_chunks=n_chunks),
        grid=(b // nb, t // tt),
        in_specs=[row3, row3, _mod_spec(nb, d, layer, 5), const2(wg), const2(wu), const2(wd), vec, vec],
        out_specs=row3,
        out_shape=jax.ShapeDtypeStruct((b, t, d), F32),
        compiler_params=_cparams(("arbitrary", "arbitrary")),
        name="ffn_dense",
    )(x1, h2, mod, wg, wu, wd, ln_g.reshape(1, d), ln_b.reshape(1, d))


def _moe_plan(counts, tm, n_tiles_max):
    tiles = (counts + tm - 1) // tm
    ends = jnp.cumsum(tiles)
    starts = ends - tiles
    n_active = ends[-1]
    tile_ids = jnp.minimum(jnp.arange(n_tiles_max, dtype=I32), jnp.maximum(n_active - 1, 0))
    tile_expert = jnp.sum((tile_ids[:, None] >= ends[None, :]).astype(I32), axis=1)
    tile_expert = jnp.minimum(tile_expert, counts.shape[0] - 1).astype(I32)
    last_tile_row = (starts + jnp.maximum(tiles - 1, 0)) * tm
    return (starts * tm).astype(I32), tile_expert, n_active.reshape(1).astype(I32), last_tile_row.astype(I32)


def _dispatch_kernel(cnt_ref, last_ref, pos_ref, h_ref, xs_ref, zero_ref, sem, zsem, *, n_experts, tm, zrows):
    i = pl.program_id(0)
    npairs = pos_ref.shape[2]

    @pl.when(i == 0)
    def _():
        zero_ref[...] = jnp.zeros(zero_ref.shape, F32)

        def zero_copy(e, c):
            row = pl.multiple_of(last_ref[e] + c * zrows, zrows)
            return pltpu.make_async_copy(zero_ref, xs_ref.at[pl.ds(row, zrows)], zsem)

        for e in range(n_experts):
            @pl.when(cnt_ref[e] > 0)
            def _():
                for c in range(tm // zrows):
                    zero_copy(e, c).start()
        for e in range(n_experts):
            @pl.when(cnt_ref[e] > 0)
            def _():
                for c in range(tm // zrows):
                    zero_copy(e, c).wait()

    def row_copy(j):
        tok = lax.shift_right_logical(j, 1)
        return pltpu.make_async_copy(h_ref.at[pl.ds(tok, 1)], xs_ref.at[pl.ds(pos_ref[0, 0, j], 1)], sem)

    def start(j, carry):
        row_copy(j).start()
        return carry

    lax.fori_loop(0, npairs, start, 0, unroll=DMA_UNROLL)

    def wait(j, carry):
        row_copy(j).wait()
        return carry

    lax.fori_loop(0, npairs, wait, 0, unroll=DMA_UNROLL)


def _dispatch(h2, pos, counts, last_tile_row, rows, tm, n_rows_total):
    n, d = h2.shape
    n_tiles = n // rows
    n_experts = counts.shape[0]
    zrows = min(tm, 256)
    smem = pl.BlockSpec(memory_space=pltpu.SMEM)
    pair = pl.BlockSpec((1, 1, 2 * rows), lambda i: (i, 0, 0), memory_space=pltpu.SMEM)
    return pl.pallas_call(
        functools.partial(_dispatch_kernel, n_experts=n_experts, tm=tm, zrows=zrows),
        grid=(n_tiles,),
        in_specs=[smem, smem, pair, pl.BlockSpec((rows, d), lambda i: (i, 0))],
        out_specs=pl.BlockSpec(memory_space=pl.ANY),
        out_shape=jax.ShapeDtypeStruct((n_rows_total, d), F32),
        scratch_shapes=[pltpu.VMEM((zrows, d), F32), pltpu.SemaphoreType.DMA(()), pltpu.SemaphoreType.DMA(())],
        compiler_params=_cparams(("arbitrary",)),
        name="moe_dispatch",
    )(counts, last_tile_row, pos, h2)


def _moe_ffn_kernel(te_ref, na_ref, xs_ref, wg_ref, wu_ref, wd_ref, y_ref, xb_ref, acc_ref):
    i = pl.program_id(0)
    j = pl.program_id(1)

    @pl.when(i < na_ref[0])
    def _():
        @pl.when(j == 0)
        def _():
            xb_ref[...] = xs_ref[...].astype(BF16)
            acc_ref[...] = jnp.zeros(acc_ref.shape, F32)

        xb = xb_ref[...]
        g = jnp.dot(xb, wg_ref[...], preferred_element_type=F32)
        u = jnp.dot(xb, wu_ref[...], preferred_element_type=F32)
        a = (g * jax.nn.sigmoid(g) * u).astype(BF16)
        acc_ref[...] += jnp.dot(a, wd_ref[...], preferred_element_type=F32)

        @pl.when(j == pl.num_programs(1) - 1)
        def _():
            y_ref[...] = acc_ref[...]


def _moe_ffn(xs, tile_expert, n_active, wg, wu, wd, tm):
    r, d = xs.shape
    f = wg.shape[2]
    fc = MOE_FC
    assert f % fc == 0 and r % tm == 0
    nj = f // fc

    def row_map(i, j, te, na):
        return (jnp.minimum(i, jnp.maximum(na[0] - 1, 0)), 0)

    def jj(i, j, na):
        return jnp.where(i < na[0], j, nj - 1)

    grid_spec = pltpu.PrefetchScalarGridSpec(
        num_scalar_prefetch=2,
        grid=(r // tm, nj),
        in_specs=[
            pl.BlockSpec((tm, d), row_map),
            pl.BlockSpec((None, d, fc), lambda i, j, te, na: (te[i], 0, jj(i, j, na))),
            pl.BlockSpec((None, d, fc), lambda i, j, te, na: (te[i], 0, jj(i, j, na))),
            pl.BlockSpec((None, fc, d), lambda i, j, te, na: (te[i], jj(i, j, na), 0)),
        ],
        out_specs=pl.BlockSpec((tm, d), row_map),
        scratch_shapes=[pltpu.VMEM((tm, d), BF16), pltpu.VMEM((tm, d), F32)],
    )
    return pl.pallas_call(
        _moe_ffn_kernel,
        grid_spec=grid_spec,
        out_shape=jax.ShapeDtypeStruct((r, d), F32),
        compiler_params=_cparams(("arbitrary", "arbitrary")),
        name="moe_ffn",
    )(tile_expert, n_active, xs, wg, wu, wd)


def _combine_kernel(pos_ref, y_ref, x1_ref, g2_ref, cw_ref, lng_ref, lnb_ref, o_ref, ybuf_ref, sem, *, alpha):
    nb, tt, d = x1_ref.shape
    rows = nb * tt
    npairs = 2 * rows

    def row_copy(j):
        dst = ybuf_ref.at[jnp.bitwise_and(j, 1), pl.ds(lax.shift_right_logical(j, 1), 1)]
        return pltpu.make_async_copy(y_ref.at[pl.ds(pos_ref[0, 0, j], 1)], dst, sem)

    def start(j, carry):
        row_copy(j).start()
        return carry

    lax.fori_loop(0, npairs, start, 0, unroll=DMA_UNROLL)

    def wait(j, carry):
        row_copy(j).wait()
        return carry

    lax.fori_loop(0, npairs, wait, 0, unroll=DMA_UNROLL)
    cw = cw_ref[...]
    f = cw[:, 0:1] * ybuf_ref[0] + cw[:, 1:2] * ybuf_ref[1]
    y = alpha * x1_ref[...] + g2_ref[...] * f.reshape(nb, tt, d)
    o_ref[...] = _layer_norm(y, lng_ref[...], lnb_ref[...])


def _combine(y, x1, mod, layer, pos, cw, ln_g, ln_b, alpha):
    b, t, d = x1.shape
    nb, tt = _row_block(b, t)
    rows = nb * tt
    tpb = t // tt
    row3 = pl.BlockSpec((nb, tt, d), lambda bi, ti: (bi, ti, 0))
    pair = pl.BlockSpec((1, 1, 2 * rows), lambda bi, ti: (bi * tpb + ti, 0, 0), memory_space=pltpu.SMEM)
    vec = _resident((1, d), lambda bi, ti: (0, 0))
    return pl.pallas_call(
        functools.partial(_combine_kernel, alpha=alpha),
        grid=(b // nb, tpb),
        in_specs=[
            pair, pl.BlockSpec(memory_space=pl.ANY), row3, _mod_spec(nb, d, layer, 5),
            pl.BlockSpec((rows, 2), lambda bi, ti: (bi * tpb + ti, 0)), vec, vec,
        ],
        out_specs=row3,
        out_shape=jax.ShapeDtypeStruct((b, t, d), F32),
        scratch_shapes=[pltpu.VMEM((2, rows, d), F32), pltpu.SemaphoreType.DMA(())],
        compiler_params=_cparams(("arbitrary", "arbitrary")),
        name="moe_combine",
    )(pos, y, x1, mod, cw, ln_g.reshape(1, d), ln_b.reshape(1, d))


def _moe(x1, h2, mod, layer, eidx, rank, cw, counts, wg, wu, wd, ln_g, ln_b, alpha):
    b, t, d = x1.shape
    n = b * t
    nb, tt = _row_block(b, t)
    rows = nb * tt
    n_experts = wg.shape[0]
    tm = MOE_TM_LARGE if 2 * n >= 8 * n_experts * MOE_TM_LARGE else MOE_TM_SMALL
    n_tiles_max = (2 * n) // tm + n_experts
    offsets, tile_expert, n_active, last_tile_row = _moe_plan(counts, tm, n_tiles_max)
    onehot = eidx[:, :, None] == jnp.arange(n_experts, dtype=I32)
    pos = (rank + jnp.sum(jnp.where(onehot, offsets, 0), axis=-1)).reshape(n // rows, 1, 2 * rows)
    xs = _dispatch(h2.reshape(n, d), pos, counts, last_tile_row, rows, tm, n_tiles_max * tm)
    y = _moe_ffn(xs, tile_expert, n_active, wg, wu, wd, tm)
    return _combine(y, x1, mod, layer, pos, cw, ln_g, ln_b, alpha)


def _trunk(x, mod, pos0, pool_hist, k_past, v_past, biases, W, depth):
    b, t, d = x.shape
    n_heads = d // HEAD_WIDTH
    qkw = n_heads * HEAD_WIDTH
    alpha = (2.0 * depth) ** 0.25
    k5 = v5 = None
    pool_states = []
    o4 = POOL_WIDTH + 3 * qkw
    for l in range(depth):
        lam_init = 0.8 - 0.6 * math.exp(-0.3 * l)
        w_a = W["w_in"][l, :, :o4].astype(BF16)
        w_gate = W["w_in"][l, :, o4:].astype(BF16)
        hist = None if pool_hist is None else pool_hist[l]
        q, kb, vb, k5, v5, po, pst = _inproj(
            x, mod, l, w_a, W["pool_w"][l].astype(BF16), W["pool_scale"][l], hist, k5, v5, depth, pos0)
        pool_states.append(pst)
        if k_past is None:
            ao = _attn_prompt(q, kb, vb, W["lam_qk"][l], W["subln_g"][l], biases[0], biases[1], lam_init)
        else:
            ao = _attn_sample(q, kb, vb, k_past, v_past, l, W["lam_qk"][l], W["subln_g"][l], biases[0], biases[1],
                              lam_init)
        moe = l % 2 == 1
        i = l // 2
        w_router_t = jnp.transpose(W["w_router"][i]) if moe else None
        outs = _post(x, mod, l, po, ao, w_gate, W["b_gate"][l], W["w_pool_up"][l].astype(BF16),
                     W["w_attn_up"][l].astype(BF16), W["w_o"][l].astype(BF16), W["ln_g"][l, 0], W["ln_b"][l, 0],
                     w_router_t, alpha)
        if not moe:
            x1, h2 = outs
            x = _ffn_dense(x1, h2, mod, l, W["w_ffn_gate"][i].astype(BF16), W["w_ffn_up"][i].astype(BF16),
                           W["w_ffn_down"][i].astype(BF16), W["ln_g"][l, 1], W["ln_b"][l, 1], alpha)
        else:
            x1, h2, eidx, rank, cw, cnt = outs
            n_experts = W["w_exp_gate"].shape[1]
            x = _moe(x1, h2, mod, l, eidx, rank, cw, cnt[0, :n_experts], W["w_exp_gate"][i].astype(BF16),
                     W["w_exp_up"][i].astype(BF16), W["w_exp_down"][i].astype(BF16), W["ln_g"][l, 1], W["ln_b"][l, 1],
                     alpha)
    return x, k5, v5, jnp.stack(pool_states)


def kernel(x_prompt, x_sample, cache_k, cache_v, state_pool, c_prompt, c_sample, rel_bias, w_ada, b_ada, w_in, b_gate,
           pool_w, pool_scale, lam_qk, subln_g, w_pool_up, w_attn_up, w_o, ln_g, ln_b, w_ffn_gate, w_ffn_up,
           w_ffn_down, w_router, w_exp_gate, w_exp_up, w_exp_down):
    W = dict(w_in=w_in, b_gate=b_gate, pool_w=pool_w, pool_scale=pool_scale, lam_qk=lam_qk, subln_g=subln_g,
             w_pool_up=w_pool_up, w_attn_up=w_attn_up, w_o=w_o, ln_g=ln_g, ln_b=ln_b, w_ffn_gate=w_ffn_gate,
             w_ffn_up=w_ffn_up, w_ffn_down=w_ffn_down, w_router=w_router, w_exp_gate=w_exp_gate, w_exp_up=w_exp_up,
             w_exp_down=w_exp_down)
    depth, d, _ = w_in.shape
    bp, s, _ = x_prompt.shape
    bs, ts, _ = x_sample.shape
    p = cache_k.shape[2]

    mod = _ada(jnp.concatenate([c_prompt, c_sample], axis=0), w_ada, b_ada)
    mod_p = mod[:, :bp].reshape(depth, bp, 1, 6 * d)
    mod_s = mod[:, bp:].reshape(depth, bs, 1, 6 * d)

    tq = ATTN_TQ
    q_pos = np.arange(tq, 2 * tq)
    bias_prev = _bias_table(rel_bias, _bucket_table(q_pos, np.arange(0, tq)))
    bias_diag = _bias_table(rel_bias, _bucket_table(q_pos, q_pos))
    s_pos = p + np.arange(ts)
    bias_past = _bias_table(rel_bias, _bucket_table(s_pos, np.arange(p)))
    bias_new = _bias_table(rel_bias, _bucket_table(s_pos, s_pos))

    y_p, k_p, v_p, pool_p = _trunk(x_prompt, mod_p, 0, None, None, None, (bias_prev, bias_diag), W, depth)
    y_s, k_s, v_s, pool_s = _trunk(x_sample, mod_s, p, state_pool, cache_k, cache_v, (bias_past, bias_new), W, depth)
    return (y_p, y_s, k_p, v_p, pool_p, k_s, v_s, pool_s)
```

```python
import functools
import math

import numpy as np
import jax
import jax.numpy as jnp
from jax import lax
from jax.experimental import pallas as pl
from jax.experimental.pallas import tpu as pltpu

F32 = jnp.float32
BF16 = jnp.bfloat16
I32 = jnp.int32

CHUNK = 64
HEAD_DIM = 64
HEAD_WIDTH = 2 * HEAD_DIM
POOL_WINDOWS = (2, 4, 8, 16)
POOL_GROUP_DIM = 128
POOL_WIDTH = len(POOL_WINDOWS) * POOL_GROUP_DIM
POOL_HIST = max(POOL_WINDOWS) - 1
HIST_ROWS = 16
N_BUCKETS = 32
MAX_DISTANCE = 128
FAR_BUCKET = N_BUCKETS // 2 - 1
LN_EPS = 1e-5
RMS_EPS = 1e-5
MASKED = -1e30
LOG2E = math.log2(math.e)

LANES = 128
ROW_TILE = 512
ATTN_TQ = 512
ATTN_TK = 512
ATTN_HEADS_PER_STEP = 4
MOE_CHUNK = 16
MOE_TM_LARGE = 1024
MOE_TM_SMALL = 256
MOE_FC = 512
VMEM_LIMIT = 56 * 1024 * 1024


def _cparams(sem):
    return pltpu.CompilerParams(dimension_semantics=sem, vmem_limit_bytes=VMEM_LIMIT)


def _resident(shape, index_map):
    return pl.BlockSpec(shape, index_map, pipeline_mode=pl.Buffered(1))


def _row_block(b, t):
    if t >= ROW_TILE:
        assert t % ROW_TILE == 0
        return 1, ROW_TILE
    nb = min(b, ROW_TILE // t)
    assert b % nb == 0 and t % 8 == 0
    return nb, t


def _rows2d(ref):
    nb, tt, w = ref.shape
    return ref[0] if nb == 1 else ref[...].reshape(nb * tt, w)


def _layer_norm(y, g, b):
    mu = jnp.mean(y, axis=-1, keepdims=True)
    yc = y - mu
    var = jnp.mean(yc * yc, axis=-1, keepdims=True)
    return yc * lax.rsqrt(var + LN_EPS) * g + b


def _ada_kernel(c_ref, w_ref, b_ref, o_ref):
    c = c_ref[...]
    s = c * jax.nn.sigmoid(c)
    o_ref[0] = jnp.dot(s.astype(BF16), w_ref[0].astype(BF16), preferred_element_type=F32) + b_ref[0]


def _ada(c_all, w_ada, b_ada):
    depth, d, n6 = w_ada.shape
    bc = c_all.shape[0]
    tn = 1536 if n6 % 1536 == 0 else n6
    return pl.pallas_call(
        _ada_kernel,
        grid=(depth, n6 // tn),
        in_specs=[
            pl.BlockSpec((bc, d), lambda l, j: (0, 0)),
            pl.BlockSpec((1, d, tn), lambda l, j: (l, 0, j)),
            pl.BlockSpec((1, 1, tn), lambda l, j: (l, 0, j)),
        ],
        out_specs=pl.BlockSpec((1, bc, tn), lambda l, j: (l, 0, j)),
        out_shape=jax.ShapeDtypeStruct((depth, bc, n6), F32),
        compiler_params=_cparams(("arbitrary", "arbitrary")),
        name="ada",
    )(c_all, w_ada, b_ada.reshape(depth, 1, n6))


def _mod_spec(nb, d, layer, chunk):
    return pl.BlockSpec((None, nb, 1, d), lambda bi, ti: (layer, bi, 0, chunk))


def _rel_bucket_np(rel):
    nb = N_BUCKETS // 2
    max_exact = nb // 2
    n = np.abs(rel)
    large = max_exact + (
        np.log(np.maximum(n, 1).astype(np.float32) / np.float32(max_exact))
        / np.float32(math.log(MAX_DISTANCE / max_exact))
        * np.float32(nb - max_exact)
    ).astype(np.int32)
    large = np.minimum(large, nb - 1)
    return np.where(rel > 0, nb, 0) + np.where(n < max_exact, n, large)


def _bucket_table(q_pos, k_pos):
    rel = k_pos[None, :] - q_pos[:, None]
    allowed = (k_pos[None, :] // CHUNK) <= (q_pos[:, None] // CHUNK)
    return np.where(allowed, _rel_bucket_np(rel), -1).astype(np.int32)


def _bias_kernel(rb_ref, bk_ref, o_ref):
    h = pl.program_id(0)
    bk = bk_ref[...]
    far = rb_ref[FAR_BUCKET, h]
    acc = jnp.zeros(bk.shape, F32)
    for b in range(N_BUCKETS):
        acc = jnp.where(bk == b, (rb_ref[b, h] - far) * LOG2E, acc)
    o_ref[0] = jnp.where(bk < 0, MASKED, acc)


def _bias_table(rel_bias, bucket_np):
    rows, cols = bucket_np.shape
    n_heads = rel_bias.shape[1]
    return pl.pallas_call(
        _bias_kernel,
        grid=(n_heads,),
        in_specs=[
            pl.BlockSpec(memory_space=pltpu.SMEM),
            pl.BlockSpec((rows, cols), lambda h: (0, 0)),
        ],
        out_specs=pl.BlockSpec((1, rows, cols), lambda h: (h, 0, 0)),
        out_shape=jax.ShapeDtypeStruct((n_heads, rows, cols), F32),
        compiler_params=_cparams(("arbitrary",)),
        name="bias_table",
    )(rel_bias, jnp.asarray(bucket_np))


def _inproj_kernel(*refs, pos0, has_hist, has_alias, n_heads):
    it = iter(refs)
    x_ref, sh_ref, sc_ref, w_ref, pw_ref, ps_ref = (next(it) for _ in range(6))
    hist_ref = next(it) if has_hist else None
    if has_alias:
        next(it)
        next(it)
    q_ref, kb_ref, vb_ref, k_ref, v_ref, po_ref, pst_ref, ext_ref = (next(it) for _ in range(8))

    ti = pl.program_id(1)
    nb, tt, d = x_ref.shape
    rows = nb * tt
    qkw = n_heads * HEAD_WIDTH
    o1 = POOL_WIDTH
    o2 = o1 + qkw
    o3 = o2 + qkw
    o4 = o3 + qkw

    h = x_ref[...] * (1.0 + sc_ref[...]) + sh_ref[...]
    hb = h.reshape(rows, d).astype(BF16)

    u = jnp.dot(hb, w_ref[:, 0:o1], preferred_element_type=F32)
    q = jnp.dot(hb, w_ref[:, o1:o2], preferred_element_type=F32)
    q_ref[...] = (q * (HEAD_DIM ** -0.5 * LOG2E)).reshape(nb, tt, qkw).astype(BF16)
    k = jnp.dot(hb, w_ref[:, o2:o3], preferred_element_type=F32)
    kb_ref[...] = k.reshape(nb, tt, qkw).astype(BF16)
    v = jnp.dot(hb, w_ref[:, o3:o4], preferred_element_type=F32)
    vb_ref[...] = v.reshape(nb, tt, qkw).astype(BF16)
    k_ref[...] = k.reshape(nb, tt, n_heads, HEAD_WIDTH)
    v_ref[...] = v.reshape(nb, tt, n_heads, HEAD_WIDTH)

    @pl.when(ti == 0)
    def _():
        if has_hist:
            ext_ref[:, 0:1, :] = jnp.zeros((nb, 1, POOL_WIDTH), F32)
            ext_ref[:, 1:HIST_ROWS, :] = hist_ref[...]
        else:
            ext_ref[:, 0:HIST_ROWS, :] = jnp.zeros((nb, HIST_ROWS, POOL_WIDTH), F32)

    ext_ref[:, HIST_ROWS:HIST_ROWS + tt, :] = u.reshape(nb, tt, POOL_WIDTH)
    pos = pos0 + ti * tt + lax.broadcasted_iota(I32, (1, tt, 1), 1)
    outs = []
    for gi, w in enumerate(POOL_WINDOWS):
        ls = slice(gi * POOL_GROUP_DIM, (gi + 1) * POOL_GROUP_DIM)
        tot = ext_ref[:, HIST_ROWS:HIST_ROWS + tt, ls]
        for j in range(1, w):
            tot = tot + ext_ref[:, HIST_ROWS - j:HIST_ROWS - j + tt, ls]
        cnt = jnp.minimum(pos + 1, w).astype(F32)
        dlt = (tot / cnt - ext_ref[:, HIST_ROWS:HIST_ROWS + tt, ls]).reshape(rows, POOL_GROUP_DIM)
        outs.append(jnp.dot(dlt.astype(BF16), pw_ref[gi], preferred_element_type=F32))
    po = jnp.concatenate(outs, axis=-1) * ps_ref[...]
    po_ref[...] = po.reshape(nb, tt, POOL_WIDTH).astype(BF16)
    pst_ref[...] = ext_ref[:, tt + 1:tt + HIST_ROWS, :]
    ext_ref[:, 0:HIST_ROWS, :] = ext_ref[:, tt:tt + HIST_ROWS, :]


def _inproj(x, mod, layer, w_a, pool_w, pool_scale, hist, k5, v5, depth, pos0):
    b, t, d = x.shape
    nb, tt = _row_block(b, t)
    n_heads = d // HEAD_WIDTH
    qkw = n_heads * HEAD_WIDTH
    has_hist = hist is not None
    has_alias = k5 is not None
    row3 = lambda w: pl.BlockSpec((nb, tt, w), lambda bi, ti: (bi, ti, 0))
    cache_spec = pl.BlockSpec((None, nb, tt, n_heads, HEAD_WIDTH), lambda bi, ti: (layer, bi, ti, 0, 0))
    in_specs = [
        row3(d),
        _mod_spec(nb, d, layer, 0),
        _mod_spec(nb, d, layer, 1),
        _resident(w_a.shape, lambda bi, ti: (0, 0)),
        _resident(pool_w.shape, lambda bi, ti: (0, 0, 0)),
        _resident((1, POOL_WIDTH), lambda bi, ti: (0, 0)),
    ]
    args = [x, mod, mod, w_a, pool_w, pool_scale.reshape(1, POOL_WIDTH)]
    if has_hist:
        in_specs.append(pl.BlockSpec((nb, POOL_HIST, POOL_WIDTH), lambda bi, ti: (bi, 0, 0)))
        args.append(hist)
    aliases = {}
    if has_alias:
        aliases = {len(args): 3, len(args) + 1: 4}
        in_specs += [pl.BlockSpec(memory_space=pl.ANY), pl.BlockSpec(memory_space=pl.ANY)]
        args += [k5, v5]
    cache_shape = jax.ShapeDtypeStruct((depth, b, t, n_heads, HEAD_WIDTH), F32)
    outs = pl.pallas_call(
        functools.partial(_inproj_kernel, pos0=pos0, has_hist=has_hist, has_alias=has_alias, n_heads=n_heads),
        grid=(b // nb, t // tt),
        in_specs=in_specs,
        out_specs=[
            row3(qkw), row3(qkw), row3(qkw), cache_spec, cache_spec, row3(POOL_WIDTH),
            pl.BlockSpec((nb, POOL_HIST, POOL_WIDTH), lambda bi, ti: (bi, 0, 0)),
        ],
        out_shape=[
            jax.ShapeDtypeStruct((b, t, qkw), BF16),
            jax.ShapeDtypeStruct((b, t, qkw), BF16),
            jax.ShapeDtypeStruct((b, t, qkw), BF16),
            cache_shape, cache_shape,
            jax.ShapeDtypeStruct((b, t, POOL_WIDTH), BF16),
            jax.ShapeDtypeStruct((b, POOL_HIST, POOL_WIDTH), F32),
        ],
        scratch_shapes=[pltpu.VMEM((nb, tt + HIST_ROWS, POOL_WIDTH), F32)],
        input_output_aliases=aliases,
        compiler_params=_cparams(("arbitrary", "arbitrary")),
        name="inproj",
    )(*args)
    return outs


def _lambda(lam_ref, lam_init):
    lq = lam_ref[...]
    s01 = jnp.sum(lq[0:1, :] * lq[1:2, :], axis=-1, keepdims=True)
    s23 = jnp.sum(lq[2:3, :] * lq[3:4, :], axis=-1, keepdims=True)
    return jnp.exp(s01) - jnp.exp(s23) + lam_init


def _stack_maps(q):
    lane = lax.broadcasted_iota(I32, q.shape, 1)
    zero = jnp.zeros_like(q)
    return jnp.concatenate([jnp.where(lane < HEAD_DIM, q, zero), jnp.where(lane >= HEAD_DIM, q, zero)], axis=0)


def _diff_out(acc, l, lam, g, lam_init, rows):
    o = acc / l
    o = o[:rows] - lam * o[rows:]
    o = o * lax.rsqrt(jnp.mean(o * o, axis=-1, keepdims=True) + RMS_EPS) * g * (1.0 - lam_init)
    return o.astype(BF16)


def _nt_dot(a, b):
    return lax.dot_general(a, b, (((1,), (1,)), ((), ())), preferred_element_type=F32)


def _attn_prompt_kernel(lam_ref, g_ref, q_ref, k_ref, v_ref, bprev_ref, bdiag_ref, o_ref, qs_ref, m_ref, acc_ref,
                        *, lam_init, hg):
    qi = pl.program_id(2)
    tq = q_ref.shape[1]
    tk = ATTN_TK
    for h in range(hg):
        qs_ref[h] = _stack_maps(q_ref[0, :, h * HEAD_WIDTH:(h + 1) * HEAD_WIDTH])
    m_ref[...] = jnp.full(m_ref.shape, MASKED, F32)
    acc_ref[...] = jnp.zeros(acc_ref.shape, F32)
    ones = jnp.ones((tk, LANES), BF16)

    def update(j, bias_ref):
        start = pl.multiple_of(j * tk, tk)
        for h in range(hg):
            sl = slice(h * HEAD_WIDTH, (h + 1) * HEAD_WIDTH)
            s = _nt_dot(qs_ref[h], k_ref[0, pl.ds(start, tk), sl])
            if bias_ref is not None:
                bias = bias_ref[h]
                s = s + jnp.concatenate([bias, bias], axis=0)
            m_prev = m_ref[h]
            m_new = jnp.maximum(m_prev, jnp.max(s, axis=-1, keepdims=True))
            alpha = jnp.exp2(m_prev - m_new)
            pb = jnp.concatenate(
                [jnp.exp2(s[:, c * LANES:(c + 1) * LANES] - m_new).astype(BF16) for c in range(tk // LANES)], axis=-1)
            v_ext = jnp.concatenate([v_ref[0, pl.ds(start, tk), sl], ones], axis=-1)
            acc_ref[h] = jnp.concatenate([alpha, alpha], axis=-1) * acc_ref[h] + jnp.dot(
                pb, v_ext, preferred_element_type=F32)
            m_ref[h] = m_new

    def far_body(j, carry):
        update(j, None)
        return carry

    lax.fori_loop(0, jnp.maximum(qi - 1, 0), far_body, 0)

    @pl.when(qi >= 1)
    def _():
        update(qi - 1, bprev_ref)

    update(qi, bdiag_ref)
    lam = _lambda(lam_ref, lam_init)
    for h in range(hg):
        acc = acc_ref[h]
        o_ref[0, :, h * HEAD_WIDTH:(h + 1) * HEAD_WIDTH] = _diff_out(
            acc[:, :HEAD_WIDTH], acc[:, HEAD_WIDTH:], lam, g_ref[...], lam_init, tq)


def _attn_prompt(q, kb, vb, lam_qk, subln_g, bias_prev, bias_diag, lam_init):
    b, t, qkw = q.shape
    n_heads = qkw // HEAD_WIDTH
    tq = ATTN_TQ
    hg = math.gcd(n_heads, ATTN_HEADS_PER_STEP)
    gw = hg * HEAD_WIDTH
    assert tq == ATTN_TK and t % tq == 0
    return pl.pallas_call(
        functools.partial(_attn_prompt_kernel, lam_init=lam_init, hg=hg),
        grid=(b, n_heads // hg, t // tq),
        in_specs=[
            pl.BlockSpec(lam_qk.shape, lambda bi, g, qi: (0, 0)),
            pl.BlockSpec((1, HEAD_WIDTH), lambda bi, g, qi: (0, 0)),
            pl.BlockSpec((1, tq, gw), lambda bi, g, qi: (bi, qi, g)),
            pl.BlockSpec((1, t, gw), lambda bi, g, qi: (bi, 0, g)),
            pl.BlockSpec((1, t, gw), lambda bi, g, qi: (bi, 0, g)),
            pl.BlockSpec((hg, tq, ATTN_TK), lambda bi, g, qi: (g, 0, 0), pipeline_mode=pl.Buffered(1)),
            pl.BlockSpec((hg, tq, ATTN_TK), lambda bi, g, qi: (g, 0, 0), pipeline_mode=pl.Buffered(1)),
        ],
        out_specs=pl.BlockSpec((1, tq, gw), lambda bi, g, qi: (bi, qi, g)),
        out_shape=jax.ShapeDtypeStruct((b, t, qkw), BF16),
        scratch_shapes=[
            pltpu.VMEM((hg, 2 * tq, HEAD_WIDTH), BF16),
            pltpu.VMEM((hg, 2 * tq, LANES), F32),
            pltpu.VMEM((hg, 2 * tq, HEAD_WIDTH + LANES), F32),
        ],
        compiler_params=_cparams(("arbitrary", "arbitrary", "arbitrary")),
        name="attn_prompt",
    )(lam_qk, subln_g.reshape(1, HEAD_WIDTH), q, kb, vb, bias_prev, bias_diag)


def _attn_sample_kernel(lam_ref, g_ref, q_ref, kn_ref, vn_ref, kp_ref, vp_ref, bpast_ref, bnew_ref, o_ref,
                        *, lam_init, n_heads):
    ts = q_ref.shape[1]
    lam = _lambda(lam_ref, lam_init)
    for hh in range(n_heads):
        sl = slice(hh * HEAD_WIDTH, (hh + 1) * HEAD_WIDTH)
        qs = _stack_maps(q_ref[0, :, sl])
        kp = kp_ref[0, :, hh, :].astype(BF16)
        vp = vp_ref[0, :, hh, :].astype(BF16)
        bp = bpast_ref[hh]
        bn = bnew_ref[hh]
        s_p = _nt_dot(qs, kp) + jnp.concatenate([bp, bp], axis=0)
        s_n = _nt_dot(qs, kn_ref[0, :, sl]) + jnp.concatenate([bn, bn], axis=0)
        m = jnp.maximum(jnp.max(s_p, axis=-1, keepdims=True), jnp.max(s_n, axis=-1, keepdims=True))
        p_p = jnp.exp2(s_p - m)
        p_n = jnp.exp2(s_n - m)
        l = jnp.sum(p_p, axis=-1, keepdims=True) + jnp.sum(p_n, axis=-1, keepdims=True)
        acc = jnp.dot(p_p.astype(BF16), vp, preferred_element_type=F32) + jnp.dot(
            p_n.astype(BF16), vn_ref[0, :, sl], preferred_element_type=F32)
        o_ref[0, :, sl] = _diff_out(acc, l, lam, g_ref[...], lam_init, ts)


def _attn_sample(q, kb, vb, k_past, v_past, layer, lam_qk, subln_g, bias_past, bias_new, lam_init):
    b, ts, qkw = q.shape
    n_heads = qkw // HEAD_WIDTH
    p = k_past.shape[2]
    row = pl.BlockSpec((1, ts, qkw), lambda bi: (bi, 0, 0))
    past = pl.BlockSpec((None, 1, p, n_heads, HEAD_WIDTH), lambda bi: (layer, bi, 0, 0, 0))
    return pl.pallas_call(
        functools.partial(_attn_sample_kernel, lam_init=lam_init, n_heads=n_heads),
        grid=(b,),
        in_specs=[
            pl.BlockSpec(lam_qk.shape, lambda bi: (0, 0)),
            pl.BlockSpec((1, HEAD_WIDTH), lambda bi: (0, 0)),
            row, row, row, past, past,
            pl.BlockSpec(bias_past.shape, lambda bi: (0, 0, 0)),
            pl.BlockSpec(bias_new.shape, lambda bi: (0, 0, 0)),
        ],
        out_specs=row,
        out_shape=jax.ShapeDtypeStruct((b, ts, qkw), BF16),
        compiler_params=_cparams(("arbitrary",)),
        name="attn_sample",
    )(lam_qk, subln_g.reshape(1, HEAD_WIDTH), q, kb, vb, k_past, v_past, bias_past, bias_new)


def _post_kernel(*refs, alpha, n_experts):
    moe = n_experts > 0
    it = iter(refs)
    (x_ref, sh1_ref, sc1_ref, g1_ref, sh2_ref, sc2_ref, po_ref, ao_ref, wgate_ref, bgate_ref, wpu_ref, wau_ref,
     wo_ref, lng_ref, lnb_ref) = (next(it) for _ in range(15))
    wr_ref = next(it) if moe else None
    x1_ref = next(it)
    h2_ref = next(it)
    if moe:
        eidx_ref, rank_ref, cw_ref, cnt_ref = (next(it) for _ in range(4))

    nb, tt, d = x_ref.shape
    rows = nb * tt
    x = x_ref[...]
    hb = (x * (1.0 + sc1_ref[...]) + sh1_ref[...]).reshape(rows, d).astype(BF16)
    gates = jax.nn.sigmoid(jnp.dot(hb, wgate_ref[...], preferred_element_type=F32) + bgate_ref[...])
    pu = jnp.dot(_rows2d(po_ref), wpu_ref[...], preferred_element_type=F32)
    au = jnp.dot(_rows2d(ao_ref), wau_ref[...], preferred_element_type=F32)
    merged = gates[:, :d] * pu + gates[:, d:] * au
    mix = jnp.dot(merged.astype(BF16), wo_ref[...], preferred_element_type=F32)
    y = alpha * x + g1_ref[...] * mix.reshape(nb, tt, d)
    x1 = _layer_norm(y, lng_ref[...], lnb_ref[...])
    x1_ref[...] = x1
    h2 = x1 * (1.0 + sc2_ref[...]) + sh2_ref[...]
    h2_ref[...] = h2.astype(BF16)
    if not moe:
        return

    h2r = h2.reshape(rows, d)
    lane = lax.broadcasted_iota(I32, (rows, LANES), 1)
    lanef = lane.astype(F32)
    logits = jnp.full((rows, LANES), -jnp.inf, F32)
    for e in range(n_experts):
        col = jnp.sum(h2r * wr_ref[e:e + 1, :], axis=-1, keepdims=True)
        logits = jnp.where(lane == e, col, logits)
    m1 = jnp.max(logits, axis=-1, keepdims=True)
    i1 = jnp.min(jnp.where(logits == m1, lanef, float(LANES)), axis=-1, keepdims=True)
    rest = jnp.where(lanef == i1, -jnp.inf, logits)
    m2 = jnp.max(rest, axis=-1, keepdims=True)
    i2 = jnp.min(jnp.where(rest == m2, lanef, float(LANES)), axis=-1, keepdims=True)
    t2 = jnp.exp(m2 - m1)
    w1 = 1.0 / (1.0 + t2)
    w2 = t2 / (1.0 + t2)
    oh1 = lanef == i1
    oh2 = lanef == i2
    oh = jnp.logical_or(oh1, oh2).astype(F32)
    tri = (lax.broadcasted_iota(I32, (rows, rows), 0) >= lax.broadcasted_iota(I32, (rows, rows), 1)).astype(BF16)
    before = jnp.dot(tri, oh.astype(BF16), preferred_element_type=F32) - oh
    r1 = jnp.sum(jnp.where(oh1, before, 0.0), axis=-1, keepdims=True)
    r2 = jnp.sum(jnp.where(oh2, before, 0.0), axis=-1, keepdims=True)
    cnt_ref[0] = jnp.sum(oh, axis=0, keepdims=True).astype(I32)
    slot = lax.broadcasted_iota(I32, (rows, 2), 1)
    eidx_ref[...] = jnp.where(slot == 0, i1, i2).astype(I32)
    rank_ref[...] = jnp.where(slot == 0, r1, r2).astype(I32)
    cw_ref[...] = jnp.where(slot == 0, w1, w2)


def _post(x, mod, layer, po, ao, w_gate, b_gate, w_pu, w_au, w_o, ln_g, ln_b, w_router_t, alpha):
    b, t, d = x.shape
    nb, tt = _row_block(b, t)
    rows = nb * tt
    assert nb == 1 or tt == t
    moe = w_router_t is not None
    n_experts = w_router_t.shape[0] if moe else 0
    n_tiles = (b // nb) * (t // tt)
    row3 = lambda w: pl.BlockSpec((nb, tt, w), lambda bi, ti: (bi, ti, 0))
    const2 = lambda a: _resident(a.shape, lambda bi, ti: (0, 0))
    tile2 = pl.BlockSpec((rows, 2), lambda bi, ti: (bi * (t // tt) + ti, 0))
    in_specs = [row3(d)] + [_mod_spec(nb, d, layer, c) for c in (0, 1, 2, 3, 4)] + [
        row3(po.shape[-1]), row3(ao.shape[-1]),
        const2(w_gate), _resident((1, 2 * d), lambda bi, ti: (0, 0)), const2(w_pu), const2(w_au), const2(w_o),
        _resident((1, d), lambda bi, ti: (0, 0)), _resident((1, d), lambda bi, ti: (0, 0)),
    ]
    args = [x, mod, mod, mod, mod, mod, po, ao, w_gate, b_gate.reshape(1, 2 * d), w_pu, w_au, w_o,
            ln_g.reshape(1, d), ln_b.reshape(1, d)]
    out_specs = [row3(d), row3(d)]
    out_shape = [jax.ShapeDtypeStruct((b, t, d), F32), jax.ShapeDtypeStruct((b, t, d), BF16)]
    if moe:
        in_specs.append(const2(w_router_t))
        args.append(w_router_t)
        out_specs += [tile2, tile2, tile2, pl.BlockSpec((1, 1, LANES), lambda bi, ti: (bi * (t // tt) + ti, 0, 0))]
        out_shape += [
            jax.ShapeDtypeStruct((n_tiles * rows, 2), I32),
            jax.ShapeDtypeStruct((n_tiles * rows, 2), I32),
            jax.ShapeDtypeStruct((n_tiles * rows, 2), F32),
            jax.ShapeDtypeStruct((n_tiles, 1, LANES), I32),
        ]
    return pl.pallas_call(
        functools.partial(_post_kernel, alpha=alpha, n_experts=n_experts),
        grid=(b // nb, t // tt),
        in_specs=in_specs,
        out_specs=out_specs,
        out_shape=out_shape,
        compiler_params=_cparams(("arbitrary", "arbitrary")),
        name="post_moe" if moe else "post",
    )(*args)


def _ffn_kernel(x1_ref, h2_ref, g2_ref, wg_ref, wu_ref, wd_ref, lng_ref, lnb_ref, o_ref, *, alpha, n_chunks):
    nb, tt, d = x1_ref.shape
    rows = nb * tt
    hb = _rows2d(h2_ref)
    f = wg_ref.shape[1]
    fc = f // n_chunks
    acc = None
    for c in range(n_chunks):
        sl = slice(c * fc, (c + 1) * fc)
        g = jnp.dot(hb, wg_ref[:, sl], preferred_element_type=F32)
        u = jnp.dot(hb, wu_ref[:, sl], preferred_element_type=F32)
        a = (g * jax.nn.sigmoid(g) * u).astype(BF16)
        part = jnp.dot(a, wd_ref[sl, :], preferred_element_type=F32)
        acc = part if acc is None else acc + part
    y = alpha * x1_ref[...] + g2_ref[...] * acc.reshape(nb, tt, d)
    o_ref[...] = _layer_norm(y, lng_ref[...], lnb_ref[...])


def _ffn_dense(x1, h2, mod, layer, wg, wu, wd, ln_g, ln_b, alpha):
    b, t, d = x1.shape
    nb, tt = _row_block(b, t)
    f = wg.shape[1]
    n_chunks = 2 if f % (2 * LANES) == 0 else 1
    row3 = pl.BlockSpec((nb, tt, d), lambda bi, ti: (bi, ti, 0))
    const2 = lambda a: _resident(a.shape, lambda bi, ti: (0, 0))
    vec = _resident((1, d), lambda bi, ti: (0, 0))
    return pl.pallas_call(
        functools.partial(_ffn_kernel, alpha=alpha, n_chunks=n_chunks),
        grid=(b // nb, t // tt),
        in_specs=[row3, row3, _mod_spec(nb, d, layer, 5), const2(wg), const2(wu), const2(wd), vec, vec],
        out_specs=row3,
        out_shape=jax.ShapeDtypeStruct((b, t, d), F32),
        compiler_params=_cparams(("arbitrary", "arbitrary")),
        name="ffn_dense",
    )(x1, h2, mod, wg, wu, wd, ln_g.reshape(1, d), ln_b.reshape(1, d))


def _moe_plan(cnt, tm, n_tiles_max):
    n_experts = cnt.shape[1]
    pc = (cnt + MOE_CHUNK - 1) // MOE_CHUNK * MOE_CHUNK
    seg_start = jnp.cumsum(pc, axis=1) - pc
    base = jnp.cumsum(pc, axis=0) - pc
    total = jnp.sum(pc, axis=0)
    tiles = (total + tm - 1) // tm
    ends = jnp.cumsum(tiles)
    starts = ends - tiles
    n_active = ends[-1]
    tile_ids = jnp.minimum(jnp.arange(n_tiles_max, dtype=I32), jnp.maximum(n_active - 1, 0))
    tile_expert = jnp.sum((tile_ids[:, None] >= ends[None, :]).astype(I32), axis=1)
    flat = lambda a: a.reshape(-1).astype(I32)
    return dict(
        seg_start=seg_start.astype(I32),
        seg=flat(seg_start), nch=flat(pc // MOE_CHUNK), gst=flat(starts[None, :] * tm + base),
        tot=flat(jnp.sum(pc // MOE_CHUNK, axis=1)), total=flat(total),
        last=flat((starts + jnp.maximum(tiles - 1, 0)) * tm),
        tile_expert=flat(jnp.minimum(tile_expert, n_experts - 1)), n_active=flat(n_active),
    )


def _dispatch_kernel(seg_ref, nch_ref, gst_ref, tot_ref, total_ref, last_ref, lpos_ref, h_ref, xs_ref, buf_ref,
                     zero_ref, sem, zsem, *, n_experts, tm, zrows):
    i = pl.program_id(0)
    n = pl.num_programs(0)
    slot = jnp.bitwise_and(i, 1)
    lrows = buf_ref.shape[1]
    rows = h_ref.shape[0]

    def chunk_copy(s, src_row, dst_row):
        return pltpu.make_async_copy(
            buf_ref.at[s, pl.ds(src_row, MOE_CHUNK)], xs_ref.at[pl.ds(dst_row, MOE_CHUNK)], sem.at[s])

    def drain(step):
        def body(c, carry):
            chunk_copy(jnp.bitwise_and(step, 1), 0, 0).wait()
            return carry

        lax.fori_loop(0, tot_ref[step], body, 0)

    @pl.when(i == 0)
    def _():
        zero_ref[...] = jnp.zeros(zero_ref.shape, BF16)

        def zero_copy(e, c):
            row = pl.multiple_of(last_ref[e] + c * zrows, zrows)
            return pltpu.make_async_copy(zero_ref, xs_ref.at[pl.ds(row, zrows)], zsem)

        for e in range(n_experts):
            @pl.when(total_ref[e] > 0)
            def _():
                for c in range(tm // zrows):
                    zero_copy(e, c).start()
        for e in range(n_experts):
            @pl.when(total_ref[e] > 0)
            def _():
                for c in range(tm // zrows):
                    zero_copy(e, c).wait()

    @pl.when(i >= 2)
    def _():
        drain(i - 2)

    lp = lpos_ref[0]
    row = lax.broadcasted_iota(I32, (lrows, rows), 0)
    sel = jnp.logical_or(row == lp[0:1, :], row == lp[1:2, :]).astype(BF16)
    buf_ref[slot] = jnp.dot(sel, h_ref[...], preferred_element_type=F32).astype(BF16)

    for e in range(n_experts):
        idx = i * n_experts + e
        src0 = seg_ref[idx]
        dst0 = gst_ref[idx]

        def body(c, carry, src0=src0, dst0=dst0):
            chunk_copy(slot, pl.multiple_of(src0 + c * MOE_CHUNK, MOE_CHUNK),
                       pl.multiple_of(dst0 + c * MOE_CHUNK, MOE_CHUNK)).start()
            return carry

        lax.fori_loop(0, nch_ref[idx], body, 0)

    @pl.when(i == n - 1)
    def _():
        @pl.when(i >= 1)
        def _():
            drain(i - 1)

        drain(i)


def _dispatch(h2, lpos_t, plan, rows, tm, n_rows_total):
    n, d = h2.shape
    n_tiles = n // rows
    n_experts = plan["total"].shape[0]
    zrows = min(tm, 256)
    lrows = _local_rows(rows, n_experts)
    grid_spec = pltpu.PrefetchScalarGridSpec(
        num_scalar_prefetch=6,
        grid=(n_tiles,),
        in_specs=[
            pl.BlockSpec((1, 2, rows), lambda i, *_: (i, 0, 0)),
            pl.BlockSpec((rows, d), lambda i, *_: (i, 0)),
        ],
        out_specs=pl.BlockSpec(memory_space=pl.ANY),
        scratch_shapes=[
            pltpu.VMEM((2, lrows, d), BF16), pltpu.VMEM((zrows, d), BF16),
            pltpu.SemaphoreType.DMA((2,)), pltpu.SemaphoreType.DMA(()),
        ],
    )
    return pl.pallas_call(
        functools.partial(_dispatch_kernel, n_experts=n_experts, tm=tm, zrows=zrows),
        grid_spec=grid_spec,
        out_shape=jax.ShapeDtypeStruct((n_rows_total, d), BF16),
        compiler_params=_cparams(("arbitrary",)),
        name="moe_dispatch",
    )(plan["seg"], plan["nch"], plan["gst"], plan["tot"], plan["total"], plan["last"], lpos_t, h2)


def _moe_ffn_kernel(te_ref, na_ref, xs_ref, wg_ref, wu_ref, wd_ref, y_ref, acc_ref):
    i = pl.program_id(0)
    j = pl.program_id(1)

    @pl.when(i < na_ref[0])
    def _():
        @pl.when(j == 0)
        def _():
            acc_ref[...] = jnp.zeros(acc_ref.shape, F32)

        xb = xs_ref[...]
        g = jnp.dot(xb, wg_ref[...], preferred_element_type=F32)
        u = jnp.dot(xb, wu_ref[...], preferred_element_type=F32)
        a = (g * jax.nn.sigmoid(g) * u).astype(BF16)
        acc_ref[...] += jnp.dot(a, wd_ref[...], preferred_element_type=F32)

        @pl.when(j == pl.num_programs(1) - 1)
        def _():
            y_ref[...] = acc_ref[...].astype(BF16)


def _moe_ffn(xs, tile_expert, n_active, wg, wu, wd, tm):
    r, d = xs.shape
    f = wg.shape[2]
    fc = MOE_FC
    assert f % fc == 0 and r % tm == 0
    nj = f // fc

    def row_map(i, j, te, na):
        return (jnp.minimum(i, jnp.maximum(na[0] - 1, 0)), 0)

    def jj(i, j, na):
        return jnp.where(i < na[0], j, nj - 1)

    grid_spec = pltpu.PrefetchScalarGridSpec(
        num_scalar_prefetch=2,
        grid=(r // tm, nj),
        in_specs=[
            pl.BlockSpec((tm, d), row_map),
            pl.BlockSpec((None, d, fc), lambda i, j, te, na: (te[i], 0, jj(i, j, na))),
            pl.BlockSpec((None, d, fc), lambda i, j, te, na: (te[i], 0, jj(i, j, na))),
            pl.BlockSpec((None, fc, d), lambda i, j, te, na: (te[i], jj(i, j, na), 0)),
        ],
        out_specs=pl.BlockSpec((tm, d), row_map),
        scratch_shapes=[pltpu.VMEM((tm, d), F32)],
    )
    return pl.pallas_call(
        _moe_ffn_kernel,
        grid_spec=grid_spec,
        out_shape=jax.ShapeDtypeStruct((r, d), BF16),
        compiler_params=_cparams(("arbitrary", "arbitrary")),
        name="moe_ffn",
    )(tile_expert, n_active, xs, wg, wu, wd)


def _combine_kernel(seg_ref, nch_ref, gst_ref, tot_ref, y_ref, x1_ref, g2_ref, lpos_ref, cw_ref, lng_ref, lnb_ref,
                    o_ref, buf_ref, sem, *, alpha, n_experts, n_tiles):
    nb, tt, d = x1_ref.shape
    rows = nb * tt
    lrows = buf_ref.shape[1]
    tile = pl.program_id(0) * pl.num_programs(1) + pl.program_id(1)
    slot = jnp.bitwise_and(tile, 1)

    def chunk_copy(s, src_row, dst_row):
        return pltpu.make_async_copy(
            y_ref.at[pl.ds(src_row, MOE_CHUNK)], buf_ref.at[s, pl.ds(dst_row, MOE_CHUNK)], sem.at[s])

    def fetch(step):
        s = jnp.bitwise_and(step, 1)
        buf_ref[s] = jnp.zeros((lrows, d), BF16)
        for e in range(n_experts):
            idx = step * n_experts + e
            src0 = gst_ref[idx]
            dst0 = seg_ref[idx]

            def body(c, carry, src0=src0, dst0=dst0):
                chunk_copy(s, pl.multiple_of(src0 + c * MOE_CHUNK, MOE_CHUNK),
                           pl.multiple_of(dst0 + c * MOE_CHUNK, MOE_CHUNK)).start()
                return carry

            lax.fori_loop(0, nch_ref[idx], body, 0)

    @pl.when(tile == 0)
    def _():
        fetch(tile)

    @pl.when(tile + 1 < n_tiles)
    def _():
        fetch(tile + 1)

    def wait(c, carry):
        chunk_copy(slot, 0, 0).wait()
        return carry

    lax.fori_loop(0, tot_ref[tile], wait, 0)
    lp = lpos_ref[...]
    lane = lax.broadcasted_iota(I32, (rows, lrows), 1)
    pick = jnp.concatenate([lane == lp[:, 0:1], lane == lp[:, 1:2]], axis=0).astype(BF16)
    g = jnp.dot(pick, buf_ref[slot], preferred_element_type=F32)
    cw = cw_ref[...]
    f = cw[:, 0:1] * g[:rows] + cw[:, 1:2] * g[rows:]
    y = alpha * x1_ref[...] + g2_ref[...] * f.reshape(nb, tt, d)
    o_ref[...] = _layer_norm(y, lng_ref[...], lnb_ref[...])


def _combine(y, x1, mod, layer, lpos, cw, plan, ln_g, ln_b, alpha):
    b, t, d = x1.shape
    nb, tt = _row_block(b, t)
    rows = nb * tt
    tpb = t // tt
    n_tiles = (b // nb) * tpb
    n_experts = plan["total"].shape[0]
    row3 = pl.BlockSpec((nb, tt, d), lambda bi, ti, *_: (bi, ti, 0))
    tile2 = pl.BlockSpec((rows, 2), lambda bi, ti, *_: (bi * tpb + ti, 0))
    vec = _resident((1, d), lambda bi, ti, *_: (0, 0))
    grid_spec = pltpu.PrefetchScalarGridSpec(
        num_scalar_prefetch=4,
        grid=(b // nb, tpb),
        in_specs=[
            pl.BlockSpec(memory_space=pl.ANY), row3,
            pl.BlockSpec((None, nb, 1, d), lambda bi, ti, *_: (layer, bi, 0, 5)), tile2, tile2, vec, vec,
        ],
        out_specs=row3,
        scratch_shapes=[pltpu.VMEM((2, _local_rows(rows, n_experts), d), BF16), pltpu.SemaphoreType.DMA((2,))],
    )
    return pl.pallas_call(
        functools.partial(_combine_kernel, alpha=alpha, n_experts=n_experts, n_tiles=n_tiles),
        grid_spec=grid_spec,
        out_shape=jax.ShapeDtypeStruct((b, t, d), F32),
        compiler_params=_cparams(("arbitrary", "arbitrary")),
        name="moe_combine",
    )(plan["seg"], plan["nch"], plan["gst"], plan["tot"], y, x1, mod, lpos, cw, ln_g.reshape(1, d),
      ln_b.reshape(1, d))


def _local_rows(rows, n_experts):
    need = 2 * rows + n_experts * (MOE_CHUNK - 1)
    return (need + LANES - 1) // LANES * LANES


def _moe(x1, h2, mod, layer, eidx, lrank, cw, cnt_tile, wg, wu, wd, ln_g, ln_b, alpha):
    b, t, d = x1.shape
    n = b * t
    nb, tt = _row_block(b, t)
    rows = nb * tt
    n_tiles = n // rows
    n_experts = wg.shape[0]
    tm = MOE_TM_LARGE if 2 * n >= 8 * n_experts * MOE_TM_LARGE else MOE_TM_SMALL
    n_tiles_max = -(-(2 * n + n_tiles * n_experts * (MOE_CHUNK - 1)) // tm) + n_experts
    plan = _moe_plan(cnt_tile[:, 0, :n_experts], tm, n_tiles_max)
    e3 = eidx.reshape(n_tiles, rows, 2)
    onehot = e3[..., None] == jnp.arange(n_experts, dtype=I32)
    seg_of_pair = jnp.sum(jnp.where(onehot, plan["seg_start"][:, None, None, :], 0), axis=-1)
    lpos = lrank.reshape(n_tiles, rows, 2) + seg_of_pair
    xs = _dispatch(h2.reshape(n, d), jnp.transpose(lpos, (0, 2, 1)), plan, rows, tm, n_tiles_max * tm)
    y = _moe_ffn(xs, plan["tile_expert"], plan["n_active"], wg, wu, wd, tm)
    return _combine(y, x1, mod, layer, lpos.reshape(n, 2), cw, plan, ln_g, ln_b, alpha)


def _trunk(x, mod, pos0, pool_hist, k_past, v_past, biases, W, depth):
    b, t, d = x.shape
    n_heads = d // HEAD_WIDTH
    qkw = n_heads * HEAD_WIDTH
    alpha = (2.0 * depth) ** 0.25
    k5 = v5 = None
    pool_states = []
    o4 = POOL_WIDTH + 3 * qkw
    for l in range(depth):
        lam_init = 0.8 - 0.6 * math.exp(-0.3 * l)
        w_a = W["w_in"][l, :, :o4].astype(BF16)
        w_gate = W["w_in"][l, :, o4:].astype(BF16)
        hist = None if pool_hist is None else pool_hist[l]
        q, kb, vb, k5, v5, po, pst = _inproj(
            x, mod, l, w_a, W["pool_w"][l].astype(BF16), W["pool_scale"][l], hist, k5, v5, depth, pos0)
        pool_states.append(pst)
        if k_past is None:
            ao = _attn_prompt(q, kb, vb, W["lam_qk"][l], W["subln_g"][l], biases[0], biases[1], lam_init)
        else:
            ao = _attn_sample(q, kb, vb, k_past, v_past, l, W["lam_qk"][l], W["subln_g"][l], biases[0], biases[1],
                              lam_init)
        moe = l % 2 == 1
        i = l // 2
        w_router_t = jnp.transpose(W["w_router"][i]) if moe else None
        outs = _post(x, mod, l, po, ao, w_gate, W["b_gate"][l], W["w_pool_up"][l].astype(BF16),
                     W["w_attn_up"][l].astype(BF16), W["w_o"][l].astype(BF16), W["ln_g"][l, 0], W["ln_b"][l, 0],
                     w_router_t, alpha)
        if not moe:
            x1, h2 = outs
            x = _ffn_dense(x1, h2, mod, l, W["w_ffn_gate"][i].astype(BF16), W["w_ffn_up"][i].astype(BF16),
                           W["w_ffn_down"][i].astype(BF16), W["ln_g"][l, 1], W["ln_b"][l, 1], alpha)
        else:
            x1, h2, eidx, rank, cw, cnt = outs
            x = _moe(x1, h2, mod, l, eidx, rank, cw, cnt, W["w_exp_gate"][i].astype(BF16),
                     W["w_exp_up"][i].astype(BF16), W["w_exp_down"][i].astype(BF16), W["ln_g"][l, 1], W["ln_b"][l, 1],
                     alpha)
    return x, k5, v5, jnp.stack(pool_states)


def kernel(x_prompt, x_sample, cache_k, cache_v, state_pool, c_prompt, c_sample, rel_bias, w_ada, b_ada, w_in, b_gate,
           pool_w, pool_scale, lam_qk, subln_g, w_pool_up, w_attn_up, w_o, ln_g, ln_b, w_ffn_gate, w_ffn_up,
           w_ffn_down, w_router, w_exp_gate, w_exp_up, w_exp_down):
    W = dict(w_in=w_in, b_gate=b_gate, pool_w=pool_w, pool_scale=pool_scale, lam_qk=lam_qk, subln_g=subln_g,
             w_pool_up=w_pool_up, w_attn_up=w_attn_up, w_o=w_o, ln_g=ln_g, ln_b=ln_b, w_ffn_gate=w_ffn_gate,
             w_ffn_up=w_ffn_up, w_ffn_down=w_ffn_down, w_router=w_router, w_exp_gate=w_exp_gate, w_exp_up=w_exp_up,
             w_exp_down=w_exp_down)
    depth, d, _ = w_in.shape
    bp, s, _ = x_prompt.shape
    bs, ts, _ = x_sample.shape
    p = cache_k.shape[2]

    mod = _ada(jnp.concatenate([c_prompt, c_sample], axis=0), w_ada, b_ada)
    mod_p = mod[:, :bp].reshape(depth, bp, 1, 6 * d)
    mod_s = mod[:, bp:].reshape(depth, bs, 1, 6 * d)

    tq = ATTN_TQ
    q_pos = np.arange(tq, 2 * tq)
    bias_prev = _bias_table(rel_bias, _bucket_table(q_pos, np.arange(0, tq)))
    bias_diag = _bias_table(rel_bias, _bucket_table(q_pos, q_pos))
    s_pos = p + np.arange(ts)
    bias_past = _bias_table(rel_bias, _bucket_table(s_pos, np.arange(p)))
    bias_new = _bias_table(rel_bias, _bucket_table(s_pos, s_pos))

    y_p, k_p, v_p, pool_p = _trunk(x_prompt, mod_p, 0, None, None, None, (bias_prev, bias_diag), W, depth)
    y_s, k_s, v_s, pool_s = _trunk(x_sample, mod_s, p, state_pool, cache_k, cache_v, (bias_past, bias_new), W, depth)
    return (y_p, y_s, k_p, v_p, pool_p, k_s, v_s, pool_s)
```

```python
import functools
import math

import numpy as np
import jax
import jax.numpy as jnp
from jax import lax
from jax.experimental import pallas as pl
from jax.experimental.pallas import tpu as pltpu

F32 = jnp.float32
BF16 = jnp.bfloat16
I32 = jnp.int32

CHUNK = 64
HEAD_DIM = 64
HEAD_WIDTH = 2 * HEAD_DIM
POOL_WINDOWS = (2, 4, 8, 16)
POOL_GROUP_DIM = 128
POOL_WIDTH = len(POOL_WINDOWS) * POOL_GROUP_DIM
POOL_HIST = max(POOL_WINDOWS) - 1
HIST_ROWS = 16
N_BUCKETS = 32
MAX_DISTANCE = 128
FAR_BUCKET = N_BUCKETS // 2 - 1
LN_EPS = 1e-5
RMS_EPS = 1e-5
MASKED = -1e30
LOG2E = math.log2(math.e)

LANES = 128
ROW_TILE = 512
ATTN_TQ = 512
ATTN_TK = 512
ATTN_HEADS_PER_STEP = 4
MOE_CHUNK = 16
MOE_TM_LARGE = 1024
MOE_TM_SMALL = 256
MOE_FC = 512
VMEM_LIMIT = 56 * 1024 * 1024


def _cparams(sem):
    return pltpu.CompilerParams(dimension_semantics=sem, vmem_limit_bytes=VMEM_LIMIT)


def _resident(shape, index_map):
    return pl.BlockSpec(shape, index_map, pipeline_mode=pl.Buffered(1))


def _row_block(b, t):
    if t >= ROW_TILE:
        assert t % ROW_TILE == 0
        return 1, ROW_TILE
    nb = min(b, ROW_TILE // t)
    assert b % nb == 0 and t % 8 == 0
    return nb, t


def _rows2d(ref):
    nb, tt, w = ref.shape
    return ref[0] if nb == 1 else ref[...].reshape(nb * tt, w)


def _layer_norm(y, g, b):
    mu = jnp.mean(y, axis=-1, keepdims=True)
    yc = y - mu
    var = jnp.mean(yc * yc, axis=-1, keepdims=True)
    return yc * lax.rsqrt(var + LN_EPS) * g + b


def _ada_kernel(c_ref, w_ref, b_ref, o_ref):
    c = c_ref[...]
    s = c * jax.nn.sigmoid(c)
    o_ref[0] = jnp.dot(s.astype(BF16), w_ref[0].astype(BF16), preferred_element_type=F32) + b_ref[0]


def _ada(c_all, w_ada, b_ada):
    depth, d, n6 = w_ada.shape
    bc = c_all.shape[0]
    tn = 1536 if n6 % 1536 == 0 else n6
    return pl.pallas_call(
        _ada_kernel,
        grid=(depth, n6 // tn),
        in_specs=[
            pl.BlockSpec((bc, d), lambda l, j: (0, 0)),
            pl.BlockSpec((1, d, tn), lambda l, j: (l, 0, j)),
            pl.BlockSpec((1, 1, tn), lambda l, j: (l, 0, j)),
        ],
        out_specs=pl.BlockSpec((1, bc, tn), lambda l, j: (l, 0, j)),
        out_shape=jax.ShapeDtypeStruct((depth, bc, n6), F32),
        compiler_params=_cparams(("arbitrary", "arbitrary")),
        name="ada",
    )(c_all, w_ada, b_ada.reshape(depth, 1, n6))


def _mod_spec(nb, d, layer, chunk):
    return pl.BlockSpec((None, nb, 1, d), lambda bi, ti: (layer, bi, 0, chunk))


def _rel_bucket_np(rel):
    nb = N_BUCKETS // 2
    max_exact = nb // 2
    n = np.abs(rel)
    large = max_exact + (
        np.log(np.maximum(n, 1).astype(np.float32) / np.float32(max_exact))
        / np.float32(math.log(MAX_DISTANCE / max_exact))
        * np.float32(nb - max_exact)
    ).astype(np.int32)
    large = np.minimum(large, nb - 1)
    return np.where(rel > 0, nb, 0) + np.where(n < max_exact, n, large)


def _bucket_table(q_pos, k_pos):
    rel = k_pos[None, :] - q_pos[:, None]
    allowed = (k_pos[None, :] // CHUNK) <= (q_pos[:, None] // CHUNK)
    return np.where(allowed, _rel_bucket_np(rel), -1).astype(np.int32)


def _bias_kernel(rb_ref, bk_ref, o_ref):
    h = pl.program_id(0)
    bk = bk_ref[...]
    far = rb_ref[FAR_BUCKET, h]
    acc = jnp.zeros(bk.shape, F32)
    for b in range(N_BUCKETS):
        acc = jnp.where(bk == b, (rb_ref[b, h] - far) * LOG2E, acc)
    o_ref[0] = jnp.where(bk < 0, MASKED, acc)


def _bias_table(rel_bias, bucket_np):
    rows, cols = bucket_np.shape
    n_heads = rel_bias.shape[1]
    return pl.pallas_call(
        _bias_kernel,
        grid=(n_heads,),
        in_specs=[
            pl.BlockSpec(memory_space=pltpu.SMEM),
            pl.BlockSpec((rows, cols), lambda h: (0, 0)),
        ],
        out_specs=pl.BlockSpec((1, rows, cols), lambda h: (h, 0, 0)),
        out_shape=jax.ShapeDtypeStruct((n_heads, rows, cols), F32),
        compiler_params=_cparams(("arbitrary",)),
        name="bias_table",
    )(rel_bias, jnp.asarray(bucket_np))


def _inproj_kernel(*refs, pos0, has_hist, has_alias, n_heads):
    it = iter(refs)
    x_ref, sh_ref, sc_ref, w_ref, pw_ref, ps_ref = (next(it) for _ in range(6))
    hist_ref = next(it) if has_hist else None
    if has_alias:
        next(it)
        next(it)
    q_ref, kb_ref, vb_ref, k_ref, v_ref, po_ref, pst_ref, ext_ref = (next(it) for _ in range(8))

    ti = pl.program_id(1)
    nb, tt, d = x_ref.shape
    rows = nb * tt
    qkw = n_heads * HEAD_WIDTH
    o1 = POOL_WIDTH
    o2 = o1 + qkw
    o3 = o2 + qkw
    o4 = o3 + qkw

    h = x_ref[...] * (1.0 + sc_ref[...]) + sh_ref[...]
    hb = h.reshape(rows, d).astype(BF16)

    u = jnp.dot(hb, w_ref[:, 0:o1], preferred_element_type=F32)
    q = jnp.dot(hb, w_ref[:, o1:o2], preferred_element_type=F32)
    q_ref[...] = (q * (HEAD_DIM ** -0.5 * LOG2E)).reshape(nb, tt, qkw).astype(BF16)
    k = jnp.dot(hb, w_ref[:, o2:o3], preferred_element_type=F32)
    kb_ref[...] = k.reshape(nb, tt, qkw).astype(BF16)
    v = jnp.dot(hb, w_ref[:, o3:o4], preferred_element_type=F32)
    vb_ref[...] = v.reshape(nb, tt, qkw).astype(BF16)
    k_ref[...] = k.reshape(nb, tt, n_heads, HEAD_WIDTH)
    v_ref[...] = v.reshape(nb, tt, n_heads, HEAD_WIDTH)

    @pl.when(ti == 0)
    def _():
        if has_hist:
            ext_ref[:, 0:1, :] = jnp.zeros((nb, 1, POOL_WIDTH), F32)
            ext_ref[:, 1:HIST_ROWS, :] = hist_ref[...]
        else:
            ext_ref[:, 0:HIST_ROWS, :] = jnp.zeros((nb, HIST_ROWS, POOL_WIDTH), F32)

    ext_ref[:, HIST_ROWS:HIST_ROWS + tt, :] = u.reshape(nb, tt, POOL_WIDTH)
    pos = pos0 + ti * tt + lax.broadcasted_iota(I32, (1, tt, 1), 1)
    outs = []
    for gi, w in enumerate(POOL_WINDOWS):
        ls = slice(gi * POOL_GROUP_DIM, (gi + 1) * POOL_GROUP_DIM)
        tot = ext_ref[:, HIST_ROWS:HIST_ROWS + tt, ls]
        for j in range(1, w):
            tot = tot + ext_ref[:, HIST_ROWS - j:HIST_ROWS - j + tt, ls]
        cnt = jnp.minimum(pos + 1, w).astype(F32)
        dlt = (tot / cnt - ext_ref[:, HIST_ROWS:HIST_ROWS + tt, ls]).reshape(rows, POOL_GROUP_DIM)
        outs.append(jnp.dot(dlt.astype(BF16), pw_ref[gi], preferred_element_type=F32))
    po = jnp.concatenate(outs, axis=-1) * ps_ref[...]
    po_ref[...] = po.reshape(nb, tt, POOL_WIDTH).astype(BF16)
    pst_ref[...] = ext_ref[:, tt + 1:tt + HIST_ROWS, :]
    ext_ref[:, 0:HIST_ROWS, :] = ext_ref[:, tt:tt + HIST_ROWS, :]


def _inproj(x, mod, layer, w_a, pool_w, pool_scale, hist, k5, v5, depth, pos0):
    b, t, d = x.shape
    nb, tt = _row_block(b, t)
    n_heads = d // HEAD_WIDTH
    qkw = n_heads * HEAD_WIDTH
    has_hist = hist is not None
    has_alias = k5 is not None
    row3 = lambda w: pl.BlockSpec((nb, tt, w), lambda bi, ti: (bi, ti, 0))
    cache_spec = pl.BlockSpec((None, nb, tt, n_heads, HEAD_WIDTH), lambda bi, ti: (layer, bi, ti, 0, 0))
    in_specs = [
        row3(d),
        _mod_spec(nb, d, layer, 0),
        _mod_spec(nb, d, layer, 1),
        _resident(w_a.shape, lambda bi, ti: (0, 0)),
        _resident(pool_w.shape, lambda bi, ti: (0, 0, 0)),
        _resident((1, POOL_WIDTH), lambda bi, ti: (0, 0)),
    ]
    args = [x, mod, mod, w_a, pool_w, pool_scale.reshape(1, POOL_WIDTH)]
    if has_hist:
        in_specs.append(pl.BlockSpec((nb, POOL_HIST, POOL_WIDTH), lambda bi, ti: (bi, 0, 0)))
        args.append(hist)
    aliases = {}
    if has_alias:
        aliases = {len(args): 3, len(args) + 1: 4}
        in_specs += [pl.BlockSpec(memory_space=pl.ANY), pl.BlockSpec(memory_space=pl.ANY)]
        args += [k5, v5]
    cache_shape = jax.ShapeDtypeStruct((depth, b, t, n_heads, HEAD_WIDTH), F32)
    outs = pl.pallas_call(
        functools.partial(_inproj_kernel, pos0=pos0, has_hist=has_hist, has_alias=has_alias, n_heads=n_heads),
        grid=(b // nb, t // tt),
        in_specs=in_specs,
        out_specs=[
            row3(qkw), row3(qkw), row3(qkw), cache_spec, cache_spec, row3(POOL_WIDTH),
            pl.BlockSpec((nb, POOL_HIST, POOL_WIDTH), lambda bi, ti: (bi, 0, 0)),
        ],
        out_shape=[
            jax.ShapeDtypeStruct((b, t, qkw), BF16),
            jax.ShapeDtypeStruct((b, t, qkw), BF16),
            jax.ShapeDtypeStruct((b, t, qkw), BF16),
            cache_shape, cache_shape,
            jax.ShapeDtypeStruct((b, t, POOL_WIDTH), BF16),
            jax.ShapeDtypeStruct((b, POOL_HIST, POOL_WIDTH), F32),
        ],
        scratch_shapes=[pltpu.VMEM((nb, tt + HIST_ROWS, POOL_WIDTH), F32)],
        input_output_aliases=aliases,
        compiler_params=_cparams(("arbitrary", "arbitrary")),
        name="inproj",
    )(*args)
    return outs


def _lambda(lam_ref, lam_init):
    lq = lam_ref[...]
    s01 = jnp.sum(lq[0:1, :] * lq[1:2, :], axis=-1, keepdims=True)
    s23 = jnp.sum(lq[2:3, :] * lq[3:4, :], axis=-1, keepdims=True)
    return jnp.exp(s01) - jnp.exp(s23) + lam_init


def _stack_maps(q):
    lane = lax.broadcasted_iota(I32, q.shape, 1)
    zero = jnp.zeros_like(q)
    return jnp.concatenate([jnp.where(lane < HEAD_DIM, q, zero), jnp.where(lane >= HEAD_DIM, q, zero)], axis=0)


def _diff_out(acc, l, lam, g, lam_init, rows):
    o = acc / l
    o = o[:rows] - lam * o[rows:]
    o = o * lax.rsqrt(jnp.mean(o * o, axis=-1, keepdims=True) + RMS_EPS) * g * (1.0 - lam_init)
    return o.astype(BF16)


def _nt_dot(a, b):
    return lax.dot_general(a, b, (((1,), (1,)), ((), ())), preferred_element_type=F32)


def _attn_prompt_kernel(lam_ref, g_ref, q_ref, k_ref, v_ref, bprev_ref, bdiag_ref, o_ref, qs_ref, m_ref, acc_ref,
                        *, lam_init, hg):
    qi = pl.program_id(2)
    tq = q_ref.shape[1]
    tk = ATTN_TK
    for h in range(hg):
        qs_ref[h] = _stack_maps(q_ref[0, :, h * HEAD_WIDTH:(h + 1) * HEAD_WIDTH])
    m_ref[...] = jnp.full(m_ref.shape, MASKED, F32)
    acc_ref[...] = jnp.zeros(acc_ref.shape, F32)
    ones = jnp.ones((tk, LANES), BF16)

    def update(j, bias_ref):
        start = pl.multiple_of(j * tk, tk)
        for h in range(hg):
            sl = slice(h * HEAD_WIDTH, (h + 1) * HEAD_WIDTH)
            s = _nt_dot(qs_ref[h], k_ref[0, pl.ds(start, tk), sl])
            if bias_ref is not None:
                bias = bias_ref[h]
                s = s + jnp.concatenate([bias, bias], axis=0)
            m_prev = m_ref[h]
            m_new = jnp.maximum(m_prev, jnp.max(s, axis=-1, keepdims=True))
            alpha = jnp.exp2(m_prev - m_new)
            pb = jnp.concatenate(
                [jnp.exp2(s[:, c * LANES:(c + 1) * LANES] - m_new).astype(BF16) for c in range(tk // LANES)], axis=-1)
            v_ext = jnp.concatenate([v_ref[0, pl.ds(start, tk), sl], ones], axis=-1)
            acc_ref[h] = jnp.concatenate([alpha, alpha], axis=-1) * acc_ref[h] + jnp.dot(
                pb, v_ext, preferred_element_type=F32)
            m_ref[h] = m_new

    def far_body(j, carry):
        update(j, None)
        return carry

    lax.fori_loop(0, jnp.maximum(qi - 1, 0), far_body, 0)

    @pl.when(qi >= 1)
    def _():
        update(qi - 1, bprev_ref)

    update(qi, bdiag_ref)
    lam = _lambda(lam_ref, lam_init)
    for h in range(hg):
        acc = acc_ref[h]
        o_ref[0, :, h * HEAD_WIDTH:(h + 1) * HEAD_WIDTH] = _diff_out(
            acc[:, :HEAD_WIDTH], acc[:, HEAD_WIDTH:], lam, g_ref[...], lam_init, tq)


def _attn_prompt(q, kb, vb, lam_qk, subln_g, bias_prev, bias_diag, lam_init):
    b, t, qkw = q.shape
    n_heads = qkw // HEAD_WIDTH
    tq = ATTN_TQ
    hg = math.gcd(n_heads, ATTN_HEADS_PER_STEP)
    gw = hg * HEAD_WIDTH
    assert tq == ATTN_TK and t % tq == 0
    return pl.pallas_call(
        functools.partial(_attn_prompt_kernel, lam_init=lam_init, hg=hg),
        grid=(n_heads // hg, b, t // tq),
        in_specs=[
            pl.BlockSpec(lam_qk.shape, lambda g, bi, qi: (0, 0)),
            pl.BlockSpec((1, HEAD_WIDTH), lambda g, bi, qi: (0, 0)),
            pl.BlockSpec((1, tq, gw), lambda g, bi, qi: (bi, qi, g)),
            pl.BlockSpec((1, t, gw), lambda g, bi, qi: (bi, 0, g)),
            pl.BlockSpec((1, t, gw), lambda g, bi, qi: (bi, 0, g)),
            pl.BlockSpec((hg, tq, ATTN_TK), lambda g, bi, qi: (g, 0, 0), pipeline_mode=pl.Buffered(1)),
            pl.BlockSpec((hg, tq, ATTN_TK), lambda g, bi, qi: (g, 0, 0), pipeline_mode=pl.Buffered(1)),
        ],
        out_specs=pl.BlockSpec((1, tq, gw), lambda g, bi, qi: (bi, qi, g)),
        out_shape=jax.ShapeDtypeStruct((b, t, qkw), BF16),
        scratch_shapes=[
            pltpu.VMEM((hg, 2 * tq, HEAD_WIDTH), BF16),
            pltpu.VMEM((hg, 2 * tq, LANES), F32),
            pltpu.VMEM((hg, 2 * tq, HEAD_WIDTH + LANES), F32),
        ],
        compiler_params=_cparams(("arbitrary", "arbitrary", "arbitrary")),
        name="attn_prompt",
    )(lam_qk, subln_g.reshape(1, HEAD_WIDTH), q, kb, vb, bias_prev, bias_diag)


def _attn_sample_kernel(lam_ref, g_ref, q_ref, kn_ref, vn_ref, kp_ref, vp_ref, bpast_ref, bnew_ref, o_ref,
                        *, lam_init, n_heads):
    ts = q_ref.shape[1]
    lam = _lambda(lam_ref, lam_init)
    past_len = kp_ref.shape[1]
    kp_all = kp_ref[0].reshape(past_len, n_heads * HEAD_WIDTH).astype(BF16)
    vp_all = vp_ref[0].reshape(past_len, n_heads * HEAD_WIDTH).astype(BF16)
    for hh in range(n_heads):
        sl = slice(hh * HEAD_WIDTH, (hh + 1) * HEAD_WIDTH)
        qs = _stack_maps(q_ref[0, :, sl])
        kp = kp_all[:, sl]
        vp = vp_all[:, sl]
        bp = bpast_ref[hh]
        bn = bnew_ref[hh]
        s_p = _nt_dot(qs, kp) + jnp.concatenate([bp, bp], axis=0)
        s_n = _nt_dot(qs, kn_ref[0, :, sl]) + jnp.concatenate([bn, bn], axis=0)
        m = jnp.maximum(jnp.max(s_p, axis=-1, keepdims=True), jnp.max(s_n, axis=-1, keepdims=True))
        p_p = jnp.exp2(s_p - m)
        p_n = jnp.exp2(s_n - m)
        l = jnp.sum(p_p, axis=-1, keepdims=True) + jnp.sum(p_n, axis=-1, keepdims=True)
        acc = jnp.dot(p_p.astype(BF16), vp, preferred_element_type=F32) + jnp.dot(
            p_n.astype(BF16), vn_ref[0, :, sl], preferred_element_type=F32)
        o_ref[0, :, sl] = _diff_out(acc, l, lam, g_ref[...], lam_init, ts)


def _attn_sample(q, kb, vb, k_past, v_past, layer, lam_qk, subln_g, bias_past, bias_new, lam_init):
    b, ts, qkw = q.shape
    n_heads = qkw // HEAD_WIDTH
    p = k_past.shape[2]
    row = pl.BlockSpec((1, ts, qkw), lambda bi: (bi, 0, 0))
    past = pl.BlockSpec((None, 1, p, n_heads, HEAD_WIDTH), lambda bi: (layer, bi, 0, 0, 0))
    return pl.pallas_call(
        functools.partial(_attn_sample_kernel, lam_init=lam_init, n_heads=n_heads),
        grid=(b,),
        in_specs=[
            pl.BlockSpec(lam_qk.shape, lambda bi: (0, 0)),
            pl.BlockSpec((1, HEAD_WIDTH), lambda bi: (0, 0)),
            row, row, row, past, past,
            pl.BlockSpec(bias_past.shape, lambda bi: (0, 0, 0)),
            pl.BlockSpec(bias_new.shape, lambda bi: (0, 0, 0)),
        ],
        out_specs=row,
        out_shape=jax.ShapeDtypeStruct((b, ts, qkw), BF16),
        compiler_params=_cparams(("arbitrary",)),
        name="attn_sample",
    )(lam_qk, subln_g.reshape(1, HEAD_WIDTH), q, kb, vb, k_past, v_past, bias_past, bias_new)


def _post_kernel(*refs, alpha, n_experts):
    moe = n_experts > 0
    it = iter(refs)
    (x_ref, sh1_ref, sc1_ref, g1_ref, sh2_ref, sc2_ref, po_ref, ao_ref, wgate_ref, bgate_ref, wpu_ref, wau_ref,
     wo_ref, lng_ref, lnb_ref) = (next(it) for _ in range(15))
    wr_ref = next(it) if moe else None
    x1_ref = next(it)
    h2_ref = next(it)
    if moe:
        eidx_ref, rank_ref, cw_ref, cnt_ref = (next(it) for _ in range(4))

    nb, tt, d = x_ref.shape
    rows = nb * tt
    x = x_ref[...]
    hb = (x * (1.0 + sc1_ref[...]) + sh1_ref[...]).reshape(rows, d).astype(BF16)
    gates = jax.nn.sigmoid(jnp.dot(hb, wgate_ref[...], preferred_element_type=F32) + bgate_ref[...])
    pu = jnp.dot(_rows2d(po_ref), wpu_ref[...], preferred_element_type=F32)
    au = jnp.dot(_rows2d(ao_ref), wau_ref[...], preferred_element_type=F32)
    merged = gates[:, :d] * pu + gates[:, d:] * au
    mix = jnp.dot(merged.astype(BF16), wo_ref[...], preferred_element_type=F32)
    y = alpha * x + g1_ref[...] * mix.reshape(nb, tt, d)
    x1 = _layer_norm(y, lng_ref[...], lnb_ref[...])
    x1_ref[...] = x1
    h2 = x1 * (1.0 + sc2_ref[...]) + sh2_ref[...]
    h2_ref[...] = h2.astype(BF16)
    if not moe:
        return

    h2r = h2.reshape(rows, d)
    lane = lax.broadcasted_iota(I32, (rows, LANES), 1)
    lanef = lane.astype(F32)
    logits = jnp.full((rows, LANES), -jnp.inf, F32)
    for e in range(n_experts):
        col = jnp.sum(h2r * wr_ref[e:e + 1, :], axis=-1, keepdims=True)
        logits = jnp.where(lane == e, col, logits)
    m1 = jnp.max(logits, axis=-1, keepdims=True)
    i1 = jnp.min(jnp.where(logits == m1, lanef, float(LANES)), axis=-1, keepdims=True)
    rest = jnp.where(lanef == i1, -jnp.inf, logits)
    m2 = jnp.max(rest, axis=-1, keepdims=True)
    i2 = jnp.min(jnp.where(rest == m2, lanef, float(LANES)), axis=-1, keepdims=True)
    t2 = jnp.exp(m2 - m1)
    w1 = 1.0 / (1.0 + t2)
    w2 = t2 / (1.0 + t2)
    oh1 = lanef == i1
    oh2 = lanef == i2
    oh = jnp.logical_or(oh1, oh2).astype(F32)
    tri = (lax.broadcasted_iota(I32, (rows, rows), 0) >= lax.broadcasted_iota(I32, (rows, rows), 1)).astype(BF16)
    before = jnp.dot(tri, oh.astype(BF16), preferred_element_type=F32) - oh
    r1 = jnp.sum(jnp.where(oh1, before, 0.0), axis=-1, keepdims=True)
    r2 = jnp.sum(jnp.where(oh2, before, 0.0), axis=-1, keepdims=True)
    cnt_ref[0] = jnp.sum(oh, axis=0, keepdims=True).astype(I32)
    slot = lax.broadcasted_iota(I32, (rows, 2), 1)
    eidx_ref[...] = jnp.where(slot == 0, i1, i2).astype(I32)
    rank_ref[...] = jnp.where(slot == 0, r1, r2).astype(I32)
    cw_ref[...] = jnp.where(slot == 0, w1, w2)


def _post(x, mod, layer, po, ao, w_gate, b_gate, w_pu, w_au, w_o, ln_g, ln_b, w_router_t, alpha):
    b, t, d = x.shape
    nb, tt = _row_block(b, t)
    rows = nb * tt
    assert nb == 1 or tt == t
    moe = w_router_t is not None
    n_experts = w_router_t.shape[0] if moe else 0
    n_tiles = (b // nb) * (t // tt)
    row3 = lambda w: pl.BlockSpec((nb, tt, w), lambda bi, ti: (bi, ti, 0))
    const2 = lambda a: _resident(a.shape, lambda bi, ti: (0, 0))
    tile2 = pl.BlockSpec((rows, 2), lambda bi, ti: (bi * (t // tt) + ti, 0))
    in_specs = [row3(d)] + [_mod_spec(nb, d, layer, c) for c in (0, 1, 2, 3, 4)] + [
        row3(po.shape[-1]), row3(ao.shape[-1]),
        const2(w_gate), _resident((1, 2 * d), lambda bi, ti: (0, 0)), const2(w_pu), const2(w_au), const2(w_o),
        _resident((1, d), lambda bi, ti: (0, 0)), _resident((1, d), lambda bi, ti: (0, 0)),
    ]
    args = [x, mod, mod, mod, mod, mod, po, ao, w_gate, b_gate.reshape(1, 2 * d), w_pu, w_au, w_o,
            ln_g.reshape(1, d), ln_b.reshape(1, d)]
    out_specs = [row3(d), row3(d)]
    out_shape = [jax.ShapeDtypeStruct((b, t, d), F32), jax.ShapeDtypeStruct((b, t, d), BF16)]
    if moe:
        in_specs.append(const2(w_router_t))
        args.append(w_router_t)
        out_specs += [tile2, tile2, tile2, pl.BlockSpec((1, 1, LANES), lambda bi, ti: (bi * (t // tt) + ti, 0, 0))]
        out_shape += [
            jax.ShapeDtypeStruct((n_tiles * rows, 2), I32),
            jax.ShapeDtypeStruct((n_tiles * rows, 2), I32),
            jax.ShapeDtypeStruct((n_tiles * rows, 2), F32),
            jax.ShapeDtypeStruct((n_tiles, 1, LANES), I32),
        ]
    return pl.pallas_call(
        functools.partial(_post_kernel, alpha=alpha, n_experts=n_experts),
        grid=(b // nb, t // tt),
        in_specs=in_specs,
        out_specs=out_specs,
        out_shape=out_shape,
        compiler_params=_cparams(("arbitrary", "arbitrary")),
        name="post_moe" if moe else "post",
    )(*args)


def _ffn_kernel(x1_ref, h2_ref, g2_ref, wg_ref, wu_ref, wd_ref, lng_ref, lnb_ref, o_ref, *, alpha, n_chunks):
    nb, tt, d = x1_ref.shape
    rows = nb * tt
    hb = _rows2d(h2_ref)
    f = wg_ref.shape[1]
    fc = f // n_chunks
    acc = None
    for c in range(n_chunks):
        sl = slice(c * fc, (c + 1) * fc)
        g = jnp.dot(hb, wg_ref[:, sl], preferred_element_type=F32)
        u = jnp.dot(hb, wu_ref[:, sl], preferred_element_type=F32)
        a = (g * jax.nn.sigmoid(g) * u).astype(BF16)
        part = jnp.dot(a, wd_ref[sl, :], preferred_element_type=F32)
        acc = part if acc is None else acc + part
    y = alpha * x1_ref[...] + g2_ref[...] * acc.reshape(nb, tt, d)
    o_ref[...] = _layer_norm(y, lng_ref[...], lnb_ref[...])


def _ffn_dense(x1, h2, mod, layer, wg, wu, wd, ln_g, ln_b, alpha):
    b, t, d = x1.shape
    nb, tt = _row_block(b, t)
    f = wg.shape[1]
    n_chunks = 2 if f % (2 * LANES) == 0 else 1
    row3 = pl.BlockSpec((nb, tt, d), lambda bi, ti: (bi, ti, 0))
    const2 = lambda a: _resident(a.shape, lambda bi, ti: (0, 0))
    vec = _resident((1, d), lambda bi, ti: (0, 0))
    return pl.pallas_call(
        functools.partial(_ffn_kernel, alpha=alpha, n_chunks=n_chunks),
        grid=(b // nb, t // tt),
        in_specs=[row3, row3, _mod_spec(nb, d, layer, 5), const2(wg), const2(wu), const2(wd), vec, vec],
        out_specs=row3,
        out_shape=jax.ShapeDtypeStruct((b, t, d), F32),
        compiler_params=_cparams(("arbitrary", "arbitrary")),
        name="ffn_dense",
    )(x1, h2, mod, wg, wu, wd, ln_g.reshape(1, d), ln_b.reshape(1, d))


def _moe_plan(cnt, tm, n_tiles_max):
    n_experts = cnt.shape[1]
    pc = (cnt + MOE_CHUNK - 1) // MOE_CHUNK * MOE_CHUNK
    seg_start = jnp.cumsum(pc, axis=1) - pc
    base = jnp.cumsum(pc, axis=0) - pc
    total = jnp.sum(pc, axis=0)
    tiles = (total + tm - 1) // tm
    ends = jnp.cumsum(tiles)
    starts = ends - tiles
    n_active = ends[-1]
    tile_ids = jnp.minimum(jnp.arange(n_tiles_max, dtype=I32), jnp.maximum(n_active - 1, 0))
    tile_expert = jnp.sum((tile_ids[:, None] >= ends[None, :]).astype(I32), axis=1)
    flat = lambda a: a.reshape(-1).astype(I32)
    return dict(
        seg_start=seg_start.astype(I32),
        seg=flat(seg_start), nch=flat(pc // MOE_CHUNK), gst=flat(starts[None, :] * tm + base),
        tot=flat(jnp.sum(pc // MOE_CHUNK, axis=1)), total=flat(total),
        last=flat((starts + jnp.maximum(tiles - 1, 0)) * tm),
        tile_expert=flat(jnp.minimum(tile_expert, n_experts - 1)), n_active=flat(n_active),
    )


def _dispatch_kernel(seg_ref, nch_ref, gst_ref, tot_ref, total_ref, last_ref, lpos_ref, h_ref, xs_ref, buf_ref,
                     zero_ref, sem, zsem, *, n_experts, tm, zrows):
    i = pl.program_id(0)
    n = pl.num_programs(0)
    slot = jnp.bitwise_and(i, 1)
    lrows = buf_ref.shape[1]
    rows = h_ref.shape[0]

    def chunk_copy(s, src_row, dst_row):
        return pltpu.make_async_copy(
            buf_ref.at[s, pl.ds(src_row, MOE_CHUNK)], xs_ref.at[pl.ds(dst_row, MOE_CHUNK)], sem.at[s])

    def drain(step):
        def body(c, carry):
            chunk_copy(jnp.bitwise_and(step, 1), 0, 0).wait()
            return carry

        lax.fori_loop(0, tot_ref[step], body, 0)

    @pl.when(i == 0)
    def _():
        zero_ref[...] = jnp.zeros(zero_ref.shape, BF16)

        def zero_copy(e, c):
            row = pl.multiple_of(last_ref[e] + c * zrows, zrows)
            return pltpu.make_async_copy(zero_ref, xs_ref.at[pl.ds(row, zrows)], zsem)

        for e in range(n_experts):
            @pl.when(total_ref[e] > 0)
            def _():
                for c in range(tm // zrows):
                    zero_copy(e, c).start()
        for e in range(n_experts):
            @pl.when(total_ref[e] > 0)
            def _():
                for c in range(tm // zrows):
                    zero_copy(e, c).wait()

    @pl.when(i >= 2)
    def _():
        drain(i - 2)

    lp = lpos_ref[0]
    row = lax.broadcasted_iota(I32, (lrows, rows), 0)
    sel = jnp.logical_or(row == lp[0:1, :], row == lp[1:2, :]).astype(BF16)
    buf_ref[slot] = jnp.dot(sel, h_ref[...], preferred_element_type=F32).astype(BF16)

    for e in range(n_experts):
        idx = i * n_experts + e
        src0 = seg_ref[idx]
        dst0 = gst_ref[idx]

        def body(c, carry, src0=src0, dst0=dst0):
            chunk_copy(slot, pl.multiple_of(src0 + c * MOE_CHUNK, MOE_CHUNK),
                       pl.multiple_of(dst0 + c * MOE_CHUNK, MOE_CHUNK)).start()
            return carry

        lax.fori_loop(0, nch_ref[idx], body, 0)

    @pl.when(i == n - 1)
    def _():
        @pl.when(i >= 1)
        def _():
            drain(i - 1)

        drain(i)


def _dispatch(h2, lpos_t, plan, rows, tm, n_rows_total):
    n, d = h2.shape
    n_tiles = n // rows
    n_experts = plan["total"].shape[0]
    zrows = min(tm, 256)
    lrows = _local_rows(rows, n_experts)
    grid_spec = pltpu.PrefetchScalarGridSpec(
        num_scalar_prefetch=6,
        grid=(n_tiles,),
        in_specs=[
            pl.BlockSpec((1, 2, rows), lambda i, *_: (i, 0, 0)),
            pl.BlockSpec((rows, d), lambda i, *_: (i, 0)),
        ],
        out_specs=pl.BlockSpec(memory_space=pl.ANY),
        scratch_shapes=[
            pltpu.VMEM((2, lrows, d), BF16), pltpu.VMEM((zrows, d), BF16),
            pltpu.SemaphoreType.DMA((2,)), pltpu.SemaphoreType.DMA(()),
        ],
    )
    return pl.pallas_call(
        functools.partial(_dispatch_kernel, n_experts=n_experts, tm=tm, zrows=zrows),
        grid_spec=grid_spec,
        out_shape=jax.ShapeDtypeStruct((n_rows_total, d), BF16),
        compiler_params=_cparams(("arbitrary",)),
        name="moe_dispatch",
    )(plan["seg"], plan["nch"], plan["gst"], plan["tot"], plan["total"], plan["last"], lpos_t, h2)


def _moe_ffn_kernel(te_ref, na_ref, xs_ref, wg_ref, wu_ref, wd_ref, y_ref, acc_ref):
    i = pl.program_id(0)
    j = pl.program_id(1)

    @pl.when(i < na_ref[0])
    def _():
        @pl.when(j == 0)
        def _():
            acc_ref[...] = jnp.zeros(acc_ref.shape, F32)

        xb = xs_ref[...]
        g = jnp.dot(xb, wg_ref[...], preferred_element_type=F32)
        u = jnp.dot(xb, wu_ref[...], preferred_element_type=F32)
        a = (g * jax.nn.sigmoid(g) * u).astype(BF16)
        acc_ref[...] += jnp.dot(a, wd_ref[...], preferred_element_type=F32)

        @pl.when(j == pl.num_programs(1) - 1)
        def _():
            y_ref[...] = acc_ref[...].astype(BF16)


def _moe_ffn(xs, tile_expert, n_active, wg, wu, wd, tm):
    r, d = xs.shape
    f = wg.shape[2]
    fc = MOE_FC
    assert f % fc == 0 and r % tm == 0
    nj = f // fc

    def row_map(i, j, te, na):
        return (jnp.minimum(i, jnp.maximum(na[0] - 1, 0)), 0)

    def jj(i, j, na):
        return jnp.where(i < na[0], j, nj - 1)

    grid_spec = pltpu.PrefetchScalarGridSpec(
        num_scalar_prefetch=2,
        grid=(r // tm, nj),
        in_specs=[
            pl.BlockSpec((tm, d), row_map),
            pl.BlockSpec((None, d, fc), lambda i, j, te, na: (te[i], 0, jj(i, j, na))),
            pl.BlockSpec((None, d, fc), lambda i, j, te, na: (te[i], 0, jj(i, j, na))),
            pl.BlockSpec((None, fc, d), lambda i, j, te, na: (te[i], jj(i, j, na), 0)),
        ],
        out_specs=pl.BlockSpec((tm, d), row_map),
        scratch_shapes=[pltpu.VMEM((tm, d), F32)],
    )
    return pl.pallas_call(
        _moe_ffn_kernel,
        grid_spec=grid_spec,
        out_shape=jax.ShapeDtypeStruct((r, d), BF16),
        compiler_params=_cparams(("arbitrary", "arbitrary")),
        name="moe_ffn",
    )(tile_expert, n_active, xs, wg, wu, wd)


def _combine_kernel(seg_ref, nch_ref, gst_ref, tot_ref, y_ref, x1_ref, g2_ref, lpos_ref, cw_ref, lng_ref, lnb_ref,
                    o_ref, buf_ref, sem, *, alpha, n_experts, n_tiles):
    nb, tt, d = x1_ref.shape
    rows = nb * tt
    lrows = buf_ref.shape[1]
    tile = pl.program_id(0) * pl.num_programs(1) + pl.program_id(1)
    slot = jnp.bitwise_and(tile, 1)

    def chunk_copy(s, src_row, dst_row):
        return pltpu.make_async_copy(
            y_ref.at[pl.ds(src_row, MOE_CHUNK)], buf_ref.at[s, pl.ds(dst_row, MOE_CHUNK)], sem.at[s])

    def fetch(step):
        s = jnp.bitwise_and(step, 1)
        buf_ref[s] = jnp.zeros((lrows, d), BF16)
        for e in range(n_experts):
            idx = step * n_experts + e
            src0 = gst_ref[idx]
            dst0 = seg_ref[idx]

            def body(c, carry, src0=src0, dst0=dst0):
                chunk_copy(s, pl.multiple_of(src0 + c * MOE_CHUNK, MOE_CHUNK),
                           pl.multiple_of(dst0 + c * MOE_CHUNK, MOE_CHUNK)).start()
                return carry

            lax.fori_loop(0, nch_ref[idx], body, 0)

    @pl.when(tile == 0)
    def _():
        fetch(tile)

    @pl.when(tile + 1 < n_tiles)
    def _():
        fetch(tile + 1)

    def wait(c, carry):
        chunk_copy(slot, 0, 0).wait()
        return carry

    lax.fori_loop(0, tot_ref[tile], wait, 0)
    lp = lpos_ref[...]
    lane = lax.broadcasted_iota(I32, (rows, lrows), 1)
    pick = jnp.concatenate([lane == lp[:, 0:1], lane == lp[:, 1:2]], axis=0).astype(BF16)
    g = jnp.dot(pick, buf_ref[slot], preferred_element_type=F32)
    cw = cw_ref[...]
    f = cw[:, 0:1] * g[:rows] + cw[:, 1:2] * g[rows:]
    y = alpha * x1_ref[...] + g2_ref[...] * f.reshape(nb, tt, d)
    o_ref[...] = _layer_norm(y, lng_ref[...], lnb_ref[...])


def _combine(y, x1, mod, layer, lpos, cw, plan, ln_g, ln_b, alpha):
    b, t, d = x1.shape
    nb, tt = _row_block(b, t)
    rows = nb * tt
    tpb = t // tt
    n_tiles = (b // nb) * tpb
    n_experts = plan["total"].shape[0]
    row3 = pl.BlockSpec((nb, tt, d), lambda bi, ti, *_: (bi, ti, 0))
    tile2 = pl.BlockSpec((rows, 2), lambda bi, ti, *_: (bi * tpb + ti, 0))
    vec = _resident((1, d), lambda bi, ti, *_: (0, 0))
    grid_spec = pltpu.PrefetchScalarGridSpec(
        num_scalar_prefetch=4,
        grid=(b // nb, tpb),
        in_specs=[
            pl.BlockSpec(memory_space=pl.ANY), row3,
            pl.BlockSpec((None, nb, 1, d), lambda bi, ti, *_: (layer, bi, 0, 5)), tile2, tile2, vec, vec,
        ],
        out_specs=row3,
        scratch_shapes=[pltpu.VMEM((2, _local_rows(rows, n_experts), d), BF16), pltpu.SemaphoreType.DMA((2,))],
    )
    return pl.pallas_call(
        functools.partial(_combine_kernel, alpha=alpha, n_experts=n_experts, n_tiles=n_tiles),
        grid_spec=grid_spec,
        out_shape=jax.ShapeDtypeStruct((b, t, d), F32),
        compiler_params=_cparams(("arbitrary", "arbitrary")),
        name="moe_combine",
    )(plan["seg"], plan["nch"], plan["gst"], plan["tot"], y, x1, mod, lpos, cw, ln_g.reshape(1, d),
      ln_b.reshape(1, d))


def _local_rows(rows, n_experts):
    need = 2 * rows + n_experts * (MOE_CHUNK - 1)
    return (need + LANES - 1) // LANES * LANES


def _moe(x1, h2, mod, layer, eidx, lrank, cw, cnt_tile, wg, wu, wd, ln_g, ln_b, alpha):
    b, t, d = x1.shape
    n = b * t
    nb, tt = _row_block(b, t)
    rows = nb * tt
    n_tiles = n // rows
    n_experts = wg.shape[0]
    tm = MOE_TM_LARGE if 2 * n >= 8 * n_experts * MOE_TM_LARGE else MOE_TM_SMALL
    n_tiles_max = -(-(2 * n + n_tiles * n_experts * (MOE_CHUNK - 1)) // tm) + n_experts
    plan = _moe_plan(cnt_tile[:, 0, :n_experts], tm, n_tiles_max)
    e3 = eidx.reshape(n_tiles, rows, 2)
    onehot = e3[..., None] == jnp.arange(n_experts, dtype=I32)
    seg_of_pair = jnp.sum(jnp.where(onehot, plan["seg_start"][:, None, None, :], 0), axis=-1)
    lpos = lrank.reshape(n_tiles, rows, 2) + seg_of_pair
    xs = _dispatch(h2.reshape(n, d), jnp.transpose(lpos, (0, 2, 1)), plan, rows, tm, n_tiles_max * tm)
    y = _moe_ffn(xs, plan["tile_expert"], plan["n_active"], wg, wu, wd, tm)
    return _combine(y, x1, mod, layer, lpos.reshape(n, 2), cw, plan, ln_g, ln_b, alpha)


def _trunk(x, mod, pos0, pool_hist, k_past, v_past, biases, W, depth):
    b, t, d = x.shape
    n_heads = d // HEAD_WIDTH
    qkw = n_heads * HEAD_WIDTH
    alpha = (2.0 * depth) ** 0.25
    k5 = v5 = None
    pool_states = []
    o4 = POOL_WIDTH + 3 * qkw
    for l in range(depth):
        lam_init = 0.8 - 0.6 * math.exp(-0.3 * l)
        w_a = W["w_in"][l, :, :o4].astype(BF16)
        w_gate = W["w_in"][l, :, o4:].astype(BF16)
        hist = None if pool_hist is None else pool_hist[l]
        q, kb, vb, k5, v5, po, pst = _inproj(
            x, mod, l, w_a, W["pool_w"][l].astype(BF16), W["pool_scale"][l], hist, k5, v5, depth, pos0)
        pool_states.append(pst)
        if k_past is None:
            ao = _attn_prompt(q, kb, vb, W["lam_qk"][l], W["subln_g"][l], biases[0], biases[1], lam_init)
        else:
            ao = _attn_sample(q, kb, vb, k_past, v_past, l, W["lam_qk"][l], W["subln_g"][l], biases[0], biases[1],
                              lam_init)
        moe = l % 2 == 1
        i = l // 2
        w_router_t = jnp.transpose(W["w_router"][i]) if moe else None
        outs = _post(x, mod, l, po, ao, w_gate, W["b_gate"][l], W["w_pool_up"][l].astype(BF16),
                     W["w_attn_up"][l].astype(BF16), W["w_o"][l].astype(BF16), W["ln_g"][l, 0], W["ln_b"][l, 0],
                     w_router_t, alpha)
        if not moe:
            x1, h2 = outs
            x = _ffn_dense(x1, h2, mod, l, W["w_ffn_gate"][i].astype(BF16), W["w_ffn_up"][i].astype(BF16),
                           W["w_ffn_down"][i].astype(BF16), W["ln_g"][l, 1], W["ln_b"][l, 1], alpha)
        else:
            x1, h2, eidx, rank, cw, cnt = outs
            x = _moe(x1, h2, mod, l, eidx, rank, cw, cnt, W["w_exp_gate"][i].astype(BF16),
                     W["w_exp_up"][i].astype(BF16), W["w_exp_down"][i].astype(BF16), W["ln_g"][l, 1], W["ln_b"][l, 1],
                     alpha)
    return x, k5, v5, jnp.stack(pool_states)


def kernel(x_prompt, x_sample, cache_k, cache_v, state_pool, c_prompt, c_sample, rel_bias, w_ada, b_ada, w_in, b_gate,
           pool_w, pool_scale, lam_qk, subln_g, w_pool_up, w_attn_up, w_o, ln_g, ln_b, w_ffn_gate, w_ffn_up,
           w_ffn_down, w_router, w_exp_gate, w_exp_up, w_exp_down):
    W = dict(w_in=w_in, b_gate=b_gate, pool_w=pool_w, pool_scale=pool_scale, lam_qk=lam_qk, subln_g=subln_g,
             w_pool_up=w_pool_up, w_attn_up=w_attn_up, w_o=w_o, ln_g=ln_g, ln_b=ln_b, w_ffn_gate=w_ffn_gate,
             w_ffn_up=w_ffn_up, w_ffn_down=w_ffn_down, w_router=w_router, w_exp_gate=w_exp_gate, w_exp_up=w_exp_up,
             w_exp_down=w_exp_down)
    depth, d, _ = w_in.shape
    bp, s, _ = x_prompt.shape
    bs, ts, _ = x_sample.shape
    p = cache_k.shape[2]

    mod = _ada(jnp.concatenate([c_prompt, c_sample], axis=0), w_ada, b_ada)
    mod_p = mod[:, :bp].reshape(depth, bp, 1, 6 * d)
    mod_s = mod[:, bp:].reshape(depth, bs, 1, 6 * d)

    tq = ATTN_TQ
    q_pos = np.arange(tq, 2 * tq)
    bias_prev = _bias_table(rel_bias, _bucket_table(q_pos, np.arange(0, tq)))
    bias_diag = _bias_table(rel_bias, _bucket_table(q_pos, q_pos))
    s_pos = p + np.arange(ts)
    bias_past = _bias_table(rel_bias, _bucket_table(s_pos, np.arange(p)))
    bias_new = _bias_table(rel_bias, _bucket_table(s_pos, s_pos))

    y_p, k_p, v_p, pool_p = _trunk(x_prompt, mod_p, 0, None, None, None, (bias_prev, bias_diag), W, depth)
    y_s, k_s, v_s, pool_s = _trunk(x_sample, mod_s, p, state_pool, cache_k, cache_v, (bias_past, bias_new), W, depth)
    return (y_p, y_s, k_p, v_p, pool_p, k_s, v_s, pool_s)
```

```python
import functools
import math

import numpy as np
import jax
import jax.numpy as jnp
from jax import lax
from jax.experimental import pallas as pl
from jax.experimental.pallas import tpu as pltpu

F32 = jnp.float32
BF16 = jnp.bfloat16
I32 = jnp.int32

CHUNK = 64
HEAD_DIM = 64
HEAD_WIDTH = 2 * HEAD_DIM
POOL_WINDOWS = (2, 4, 8, 16)
POOL_GROUP_DIM = 128
POOL_WIDTH = len(POOL_WINDOWS) * POOL_GROUP_DIM
POOL_HIST = max(POOL_WINDOWS) - 1
HIST_ROWS = 16
N_BUCKETS = 32
MAX_DISTANCE = 128
FAR_BUCKET = N_BUCKETS // 2 - 1
LN_EPS = 1e-5
RMS_EPS = 1e-5
MASKED = -1e30
LOG2E = math.log2(math.e)

LANES = 128
ROW_TILE = 512
ATTN_TQ = 512
ATTN_TK = 512
ATTN_HEADS_PER_STEP = 4
MOE_CHUNK = 16
MOE_TM_LARGE = 1024
MOE_TM_SMALL = 256
MOE_FC = 512
VMEM_LIMIT = 56 * 1024 * 1024


def _cparams(sem):
    return pltpu.CompilerParams(dimension_semantics=sem, vmem_limit_bytes=VMEM_LIMIT)


def _resident(shape, index_map):
    return pl.BlockSpec(shape, index_map, pipeline_mode=pl.Buffered(1))


def _row_block(b, t):
    if t >= ROW_TILE:
        assert t % ROW_TILE == 0
        return 1, ROW_TILE
    nb = min(b, ROW_TILE // t)
    assert b % nb == 0 and t % 8 == 0
    return nb, t


def _rows2d(ref):
    nb, tt, w = ref.shape
    return ref[0] if nb == 1 else ref[...].reshape(nb * tt, w)


def _layer_norm(y, g, b):
    mu = jnp.mean(y, axis=-1, keepdims=True)
    yc = y - mu
    var = jnp.mean(yc * yc, axis=-1, keepdims=True)
    return yc * lax.rsqrt(var + LN_EPS) * g + b


def _ada_kernel(c_ref, w_ref, b_ref, o_ref):
    c = c_ref[...]
    s = c * jax.nn.sigmoid(c)
    o_ref[0] = jnp.dot(s.astype(BF16), w_ref[0].astype(BF16), preferred_element_type=F32) + b_ref[0]


def _ada(c_all, w_ada, b_ada):
    depth, d, n6 = w_ada.shape
    bc = c_all.shape[0]
    tn = 1536 if n6 % 1536 == 0 else n6
    return pl.pallas_call(
        _ada_kernel,
        grid=(depth, n6 // tn),
        in_specs=[
            pl.BlockSpec((bc, d), lambda l, j: (0, 0)),
            pl.BlockSpec((1, d, tn), lambda l, j: (l, 0, j)),
            pl.BlockSpec((1, 1, tn), lambda l, j: (l, 0, j)),
        ],
        out_specs=pl.BlockSpec((1, bc, tn), lambda l, j: (l, 0, j)),
        out_shape=jax.ShapeDtypeStruct((depth, bc, n6), F32),
        compiler_params=_cparams(("arbitrary", "arbitrary")),
        name="ada",
    )(c_all, w_ada, b_ada.reshape(depth, 1, n6))


def _mod_spec(nb, d, layer, chunk):
    return pl.BlockSpec((None, nb, 1, d), lambda bi, ti: (layer, bi, 0, chunk))


def _rel_bucket_np(rel):
    nb = N_BUCKETS // 2
    max_exact = nb // 2
    n = np.abs(rel)
    large = max_exact + (
        np.log(np.maximum(n, 1).astype(np.float32) / np.float32(max_exact))
        / np.float32(math.log(MAX_DISTANCE / max_exact))
        * np.float32(nb - max_exact)
    ).astype(np.int32)
    large = np.minimum(large, nb - 1)
    return np.where(rel > 0, nb, 0) + np.where(n < max_exact, n, large)


def _bucket_table(q_pos, k_pos):
    rel = k_pos[None, :] - q_pos[:, None]
    allowed = (k_pos[None, :] // CHUNK) <= (q_pos[:, None] // CHUNK)
    return np.where(allowed, _rel_bucket_np(rel), -1).astype(np.int32)


def _bias_kernel(rb_ref, bk_ref, o_ref):
    h = pl.program_id(0)
    bk = bk_ref[...]
    far = rb_ref[FAR_BUCKET, h]
    acc = jnp.zeros(bk.shape, F32)
    for b in range(N_BUCKETS):
        acc = jnp.where(bk == b, (rb_ref[b, h] - far) * LOG2E, acc)
    o_ref[0] = jnp.where(bk < 0, MASKED, acc)


def _bias_table(rel_bias, bucket_np):
    rows, cols = bucket_np.shape
    n_heads = rel_bias.shape[1]
    return pl.pallas_call(
        _bias_kernel,
        grid=(n_heads,),
        in_specs=[
            pl.BlockSpec(memory_space=pltpu.SMEM),
            pl.BlockSpec((rows, cols), lambda h: (0, 0)),
        ],
        out_specs=pl.BlockSpec((1, rows, cols), lambda h: (h, 0, 0)),
        out_shape=jax.ShapeDtypeStruct((n_heads, rows, cols), F32),
        compiler_params=_cparams(("arbitrary",)),
        name="bias_table",
    )(rel_bias, jnp.asarray(bucket_np))


def _inproj_kernel(*refs, pos0, has_hist, has_alias, n_heads):
    it = iter(refs)
    x_ref, sh_ref, sc_ref, w_ref, pw_ref, ps_ref = (next(it) for _ in range(6))
    hist_ref = next(it) if has_hist else None
    if has_alias:
        next(it)
        next(it)
    q_ref, kb_ref, vb_ref, k_ref, v_ref, po_ref, pst_ref, ext_ref = (next(it) for _ in range(8))

    ti = pl.program_id(1)
    nb, tt, d = x_ref.shape
    rows = nb * tt
    qkw = n_heads * HEAD_WIDTH
    o1 = POOL_WIDTH
    o2 = o1 + qkw
    o3 = o2 + qkw
    o4 = o3 + qkw

    @pl.when(ti == 0)
    def _():
        if has_hist:
            ext_ref[:, 0:1, :] = jnp.zeros((nb, 1, POOL_WIDTH), F32)
            ext_ref[:, 1:HIST_ROWS, :] = hist_ref[...]
        else:
            ext_ref[:, 0:HIST_ROWS, :] = jnp.zeros((nb, HIST_ROWS, POOL_WIDTH), F32)

    h = x_ref[...] * (1.0 + sc_ref[...]) + sh_ref[...]
    hb = h.reshape(rows, d).astype(BF16)

    u = jnp.dot(hb, w_ref[:, 0:o1], preferred_element_type=F32)
    q = jnp.dot(hb, w_ref[:, o1:o2], preferred_element_type=F32)
    q_ref[...] = (q * (HEAD_DIM ** -0.5 * LOG2E)).reshape(nb, tt, qkw).astype(BF16)
    k = jnp.dot(hb, w_ref[:, o2:o3], preferred_element_type=F32)
    kb_ref[...] = k.reshape(nb, tt, qkw).astype(BF16)
    v = jnp.dot(hb, w_ref[:, o3:o4], preferred_element_type=F32)
    vb_ref[...] = v.reshape(nb, tt, qkw).astype(BF16)
    k_ref[...] = k.reshape(nb, tt, n_heads, HEAD_WIDTH)
    v_ref[...] = v.reshape(nb, tt, n_heads, HEAD_WIDTH)

    ext_ref[:, HIST_ROWS:HIST_ROWS + tt, :] = u.reshape(nb, tt, POOL_WIDTH)
    pos = pos0 + ti * tt + lax.broadcasted_iota(I32, (1, tt, 1), 1)
    outs = []
    for gi, w in enumerate(POOL_WINDOWS):
        ls = slice(gi * POOL_GROUP_DIM, (gi + 1) * POOL_GROUP_DIM)
        tot = ext_ref[:, HIST_ROWS:HIST_ROWS + tt, ls]
        for j in range(1, w):
            tot = tot + ext_ref[:, HIST_ROWS - j:HIST_ROWS - j + tt, ls]
        cnt = jnp.minimum(pos + 1, w).astype(F32)
        dlt = (tot / cnt - ext_ref[:, HIST_ROWS:HIST_ROWS + tt, ls]).reshape(rows, POOL_GROUP_DIM)
        outs.append(jnp.dot(dlt.astype(BF16), pw_ref[gi], preferred_element_type=F32))
    po = jnp.concatenate(outs, axis=-1) * ps_ref[...]
    po_ref[...] = po.reshape(nb, tt, POOL_WIDTH).astype(BF16)
    pst_ref[...] = ext_ref[:, tt + 1:tt + HIST_ROWS, :]
    ext_ref[:, 0:HIST_ROWS, :] = ext_ref[:, tt:tt + HIST_ROWS, :]


def _inproj(x, mod, layer, w_a, pool_w, pool_scale, hist, k5, v5, depth, pos0):
    b, t, d = x.shape
    nb, tt = _row_block(b, t)
    n_heads = d // HEAD_WIDTH
    qkw = n_heads * HEAD_WIDTH
    has_hist = hist is not None
    has_alias = k5 is not None
    row3 = lambda w: pl.BlockSpec((nb, tt, w), lambda bi, ti: (bi, ti, 0))
    cache_spec = pl.BlockSpec((None, nb, tt, n_heads, HEAD_WIDTH), lambda bi, ti: (layer, bi, ti, 0, 0))
    in_specs = [
        row3(d),
        _mod_spec(nb, d, layer, 0),
        _mod_spec(nb, d, layer, 1),
        _resident(w_a.shape, lambda bi, ti: (0, 0)),
        _resident(pool_w.shape, lambda bi, ti: (0, 0, 0)),
        _resident((1, POOL_WIDTH), lambda bi, ti: (0, 0)),
    ]
    args = [x, mod, mod, w_a, pool_w, pool_scale.reshape(1, POOL_WIDTH)]
    if has_hist:
        in_specs.append(pl.BlockSpec((nb, POOL_HIST, POOL_WIDTH), lambda bi, ti: (bi, 0, 0)))
        args.append(hist)
    aliases = {}
    if has_alias:
        aliases = {len(args): 3, len(args) + 1: 4}
        in_specs += [pl.BlockSpec(memory_space=pl.ANY), pl.BlockSpec(memory_space=pl.ANY)]
        args += [k5, v5]
    cache_shape = jax.ShapeDtypeStruct((depth, b, t, n_heads, HEAD_WIDTH), F32)
    outs = pl.pallas_call(
        functools.partial(_inproj_kernel, pos0=pos0, has_hist=has_hist, has_alias=has_alias, n_heads=n_heads),
        grid=(b // nb, t // tt),
        in_specs=in_specs,
        out_specs=[
            row3(qkw), row3(qkw), row3(qkw), cache_spec, cache_spec, row3(POOL_WIDTH),
            pl.BlockSpec((nb, POOL_HIST, POOL_WIDTH), lambda bi, ti: (bi, 0, 0)),
        ],
        out_shape=[
            jax.ShapeDtypeStruct((b, t, qkw), BF16),
            jax.ShapeDtypeStruct((b, t, qkw), BF16),
            jax.ShapeDtypeStruct((b, t, qkw), BF16),
            cache_shape, cache_shape,
            jax.ShapeDtypeStruct((b, t, POOL_WIDTH), BF16),
            jax.ShapeDtypeStruct((b, POOL_HIST, POOL_WIDTH), F32),
        ],
        scratch_shapes=[pltpu.VMEM((nb, tt + HIST_ROWS, POOL_WIDTH), F32)],
        input_output_aliases=aliases,
        compiler_params=_cparams(("arbitrary", "arbitrary")),
        name="inproj",
    )(*args)
    return outs


def _lambda(lam_ref, lam_init):
    lq = lam_ref[...]
    s01 = jnp.sum(lq[0:1, :] * lq[1:2, :], axis=-1, keepdims=True)
    s23 = jnp.sum(lq[2:3, :] * lq[3:4, :], axis=-1, keepdims=True)
    return jnp.exp(s01) - jnp.exp(s23) + lam_init


def _stack_maps(q):
    lane = lax.broadcasted_iota(I32, q.shape, 1)
    zero = jnp.zeros_like(q)
    return jnp.concatenate([jnp.where(lane < HEAD_DIM, q, zero), jnp.where(lane >= HEAD_DIM, q, zero)], axis=0)


def _diff_out(acc, l, lam, g, lam_init, rows):
    o = acc / l
    o = o[:rows] - lam * o[rows:]
    o = o * lax.rsqrt(jnp.mean(o * o, axis=-1, keepdims=True) + RMS_EPS) * g * (1.0 - lam_init)
    return o.astype(BF16)


def _nt_dot(a, b):
    return lax.dot_general(a, b, (((1,), (1,)), ((), ())), preferred_element_type=F32)


def _attn_prompt_kernel(lam_ref, g_ref, q_ref, k_ref, v_ref, bprev_ref, bdiag_ref, o_ref, qs_ref, m_ref, acc_ref,
                        *, lam_init, hg):
    qi = pl.program_id(2)
    tq = q_ref.shape[1]
    tk = ATTN_TK
    for h in range(hg):
        qs_ref[h] = _stack_maps(q_ref[0, :, h * HEAD_WIDTH:(h + 1) * HEAD_WIDTH])
    ones = jnp.ones((tk, LANES), BF16)

    def update(j, bias_ref, first=False):
        start = pl.multiple_of(j * tk, tk)
        for h in range(hg):
            sl = slice(h * HEAD_WIDTH, (h + 1) * HEAD_WIDTH)
            s = _nt_dot(qs_ref[h], k_ref[0, pl.ds(start, tk), sl])
            if bias_ref is not None:
                bias = bias_ref[h]
                s = s + jnp.concatenate([bias, bias], axis=0)
            m_new = jnp.broadcast_to(jnp.max(s, axis=-1, keepdims=True), (s.shape[0], LANES))
            if not first:
                m_prev = m_ref[h]
                m_new = jnp.maximum(m_prev, m_new)
            pb = jnp.concatenate(
                [jnp.exp2(s[:, c * LANES:(c + 1) * LANES] - m_new).astype(BF16) for c in range(tk // LANES)], axis=-1)
            v_ext = jnp.concatenate([v_ref[0, pl.ds(start, tk), sl], ones], axis=-1)
            pv = jnp.dot(pb, v_ext, preferred_element_type=F32)
            if first:
                acc_ref[h] = pv
            else:
                alpha = jnp.exp2(m_prev - m_new)
                acc_ref[h] = jnp.concatenate([alpha, alpha], axis=-1) * acc_ref[h] + pv
            m_ref[h] = m_new

    update(qi, bdiag_ref, first=True)

    @pl.when(qi >= 1)
    def _():
        update(qi - 1, bprev_ref)

    def far_body(j, carry):
        update(j, None)
        return carry

    lax.fori_loop(0, jnp.maximum(qi - 1, 0), far_body, 0)
    lam = _lambda(lam_ref, lam_init)
    for h in range(hg):
        acc = acc_ref[h]
        o_ref[0, :, h * HEAD_WIDTH:(h + 1) * HEAD_WIDTH] = _diff_out(
            acc[:, :HEAD_WIDTH], acc[:, HEAD_WIDTH:], lam, g_ref[...], lam_init, tq)


def _attn_prompt(q, kb, vb, lam_qk, subln_g, bias_prev, bias_diag, lam_init):
    b, t, qkw = q.shape
    n_heads = qkw // HEAD_WIDTH
    tq = ATTN_TQ
    hg = math.gcd(n_heads, ATTN_HEADS_PER_STEP)
    gw = hg * HEAD_WIDTH
    assert tq == ATTN_TK and t % tq == 0
    return pl.pallas_call(
        functools.partial(_attn_prompt_kernel, lam_init=lam_init, hg=hg),
        grid=(n_heads // hg, b, t // tq),
        in_specs=[
            pl.BlockSpec(lam_qk.shape, lambda g, bi, qi: (0, 0)),
            pl.BlockSpec((1, HEAD_WIDTH), lambda g, bi, qi: (0, 0)),
            pl.BlockSpec((1, tq, gw), lambda g, bi, qi: (bi, qi, g)),
            pl.BlockSpec((1, t, gw), lambda g, bi, qi: (bi, 0, g)),
            pl.BlockSpec((1, t, gw), lambda g, bi, qi: (bi, 0, g)),
            pl.BlockSpec((hg, tq, ATTN_TK), lambda g, bi, qi: (g, 0, 0), pipeline_mode=pl.Buffered(1)),
            pl.BlockSpec((hg, tq, ATTN_TK), lambda g, bi, qi: (g, 0, 0), pipeline_mode=pl.Buffered(1)),
        ],
        out_specs=pl.BlockSpec((1, tq, gw), lambda g, bi, qi: (bi, qi, g)),
        out_shape=jax.ShapeDtypeStruct((b, t, qkw), BF16),
        scratch_shapes=[
            pltpu.VMEM((hg, 2 * tq, HEAD_WIDTH), BF16),
            pltpu.VMEM((hg, 2 * tq, LANES), F32),
            pltpu.VMEM((hg, 2 * tq, HEAD_WIDTH + LANES), F32),
        ],
        compiler_params=_cparams(("arbitrary", "arbitrary", "arbitrary")),
        name="attn_prompt",
    )(lam_qk, subln_g.reshape(1, HEAD_WIDTH), q, kb, vb, bias_prev, bias_diag)


def _attn_sample_kernel(lam_ref, g_ref, q_ref, kn_ref, vn_ref, kp_ref, vp_ref, bpast_ref, bnew_ref, o_ref,
                        *, lam_init, n_heads):
    ts = q_ref.shape[1]
    lam = _lambda(lam_ref, lam_init)
    past_len = kp_ref.shape[1]
    kp_all = kp_ref[0].reshape(past_len, n_heads * HEAD_WIDTH).astype(BF16)
    vp_all = vp_ref[0].reshape(past_len, n_heads * HEAD_WIDTH).astype(BF16)
    for hh in range(n_heads):
        sl = slice(hh * HEAD_WIDTH, (hh + 1) * HEAD_WIDTH)
        qs = _stack_maps(q_ref[0, :, sl])
        kp = kp_all[:, sl]
        vp = vp_all[:, sl]
        bp = bpast_ref[hh]
        bn = bnew_ref[hh]
        s_p = _nt_dot(qs, kp) + jnp.concatenate([bp, bp], axis=0)
        s_n = _nt_dot(qs, kn_ref[0, :, sl]) + jnp.concatenate([bn, bn], axis=0)
        m = jnp.maximum(jnp.max(s_p, axis=-1, keepdims=True), jnp.max(s_n, axis=-1, keepdims=True))
        p_p = jnp.exp2(s_p - m)
        p_n = jnp.exp2(s_n - m)
        l = jnp.sum(p_p, axis=-1, keepdims=True) + jnp.sum(p_n, axis=-1, keepdims=True)
        acc = jnp.dot(p_p.astype(BF16), vp, preferred_element_type=F32) + jnp.dot(
            p_n.astype(BF16), vn_ref[0, :, sl], preferred_element_type=F32)
        o_ref[0, :, sl] = _diff_out(acc, l, lam, g_ref[...], lam_init, ts)


def _attn_sample(q, kb, vb, k_past, v_past, layer, lam_qk, subln_g, bias_past, bias_new, lam_init):
    b, ts, qkw = q.shape
    n_heads = qkw // HEAD_WIDTH
    p = k_past.shape[2]
    row = pl.BlockSpec((1, ts, qkw), lambda bi: (bi, 0, 0))
    past = pl.BlockSpec((None, 1, p, n_heads, HEAD_WIDTH), lambda bi: (layer, bi, 0, 0, 0))
    return pl.pallas_call(
        functools.partial(_attn_sample_kernel, lam_init=lam_init, n_heads=n_heads),
        grid=(b,),
        in_specs=[
            pl.BlockSpec(lam_qk.shape, lambda bi: (0, 0)),
            pl.BlockSpec((1, HEAD_WIDTH), lambda bi: (0, 0)),
            row, row, row, past, past,
            pl.BlockSpec(bias_past.shape, lambda bi: (0, 0, 0)),
            pl.BlockSpec(bias_new.shape, lambda bi: (0, 0, 0)),
        ],
        out_specs=row,
        out_shape=jax.ShapeDtypeStruct((b, ts, qkw), BF16),
        compiler_params=_cparams(("arbitrary",)),
        name="attn_sample",
    )(lam_qk, subln_g.reshape(1, HEAD_WIDTH), q, kb, vb, k_past, v_past, bias_past, bias_new)


def _post_kernel(*refs, alpha, n_experts):
    moe = n_experts > 0
    it = iter(refs)
    (x_ref, sh1_ref, sc1_ref, g1_ref, sh2_ref, sc2_ref, po_ref, ao_ref, wgate_ref, bgate_ref, wpu_ref, wau_ref,
     wo_ref, lng_ref, lnb_ref) = (next(it) for _ in range(15))
    wr_ref = next(it) if moe else None
    x1_ref = next(it)
    h2_ref = next(it)
    if moe:
        eidx_ref, rank_ref, cw_ref, cnt_ref = (next(it) for _ in range(4))

    nb, tt, d = x_ref.shape
    rows = nb * tt
    x = x_ref[...]
    hb = (x * (1.0 + sc1_ref[...]) + sh1_ref[...]).reshape(rows, d).astype(BF16)
    gates = jax.nn.sigmoid(jnp.dot(hb, wgate_ref[...], preferred_element_type=F32) + bgate_ref[...])
    pu = jnp.dot(_rows2d(po_ref), wpu_ref[...], preferred_element_type=F32)
    au = jnp.dot(_rows2d(ao_ref), wau_ref[...], preferred_element_type=F32)
    merged = gates[:, :d] * pu + gates[:, d:] * au
    mix = jnp.dot(merged.astype(BF16), wo_ref[...], preferred_element_type=F32)
    y = alpha * x + g1_ref[...] * mix.reshape(nb, tt, d)
    x1 = _layer_norm(y, lng_ref[...], lnb_ref[...])
    x1_ref[...] = x1
    h2 = x1 * (1.0 + sc2_ref[...]) + sh2_ref[...]
    h2_ref[...] = h2.astype(BF16)
    if not moe:
        return

    h2r = h2.reshape(rows, d)
    lane = lax.broadcasted_iota(I32, (rows, LANES), 1)
    lanef = lane.astype(F32)
    logits = jnp.full((rows, LANES), -jnp.inf, F32)
    for e in range(n_experts):
        col = jnp.sum(h2r * wr_ref[e:e + 1, :], axis=-1, keepdims=True)
        logits = jnp.where(lane == e, col, logits)
    m1 = jnp.max(logits, axis=-1, keepdims=True)
    i1 = jnp.min(jnp.where(logits == m1, lanef, float(LANES)), axis=-1, keepdims=True)
    rest = jnp.where(lanef == i1, -jnp.inf, logits)
    m2 = jnp.max(rest, axis=-1, keepdims=True)
    i2 = jnp.min(jnp.where(rest == m2, lanef, float(LANES)), axis=-1, keepdims=True)
    t2 = jnp.exp(m2 - m1)
    w1 = 1.0 / (1.0 + t2)
    w2 = t2 / (1.0 + t2)
    oh1 = lanef == i1
    oh2 = lanef == i2
    oh = jnp.logical_or(oh1, oh2).astype(F32)
    tri = (lax.broadcasted_iota(I32, (rows, rows), 0) >= lax.broadcasted_iota(I32, (rows, rows), 1)).astype(BF16)
    before = jnp.dot(tri, oh.astype(BF16), preferred_element_type=F32) - oh
    r1 = jnp.sum(jnp.where(oh1, before, 0.0), axis=-1, keepdims=True)
    r2 = jnp.sum(jnp.where(oh2, before, 0.0), axis=-1, keepdims=True)
    cnt_ref[0] = jnp.sum(oh, axis=0, keepdims=True).astype(I32)
    slot = lax.broadcasted_iota(I32, (rows, 2), 1)
    eidx_ref[...] = jnp.where(slot == 0, i1, i2).astype(I32)
    rank_ref[...] = jnp.where(slot == 0, r1, r2).astype(I32)
    cw_ref[...] = jnp.where(slot == 0, w1, w2)


def _post(x, mod, layer, po, ao, w_gate, b_gate, w_pu, w_au, w_o, ln_g, ln_b, w_router_t, alpha):
    b, t, d = x.shape
    nb, tt = _row_block(b, t)
    rows = nb * tt
    assert nb == 1 or tt == t
    moe = w_router_t is not None
    n_experts = w_router_t.shape[0] if moe else 0
    n_tiles = (b // nb) * (t // tt)
    row3 = lambda w: pl.BlockSpec((nb, tt, w), lambda bi, ti: (bi, ti, 0))
    const2 = lambda a: _resident(a.shape, lambda bi, ti: (0, 0))
    tile2 = pl.BlockSpec((rows, 2), lambda bi, ti: (bi * (t // tt) + ti, 0))
    in_specs = [row3(d)] + [_mod_spec(nb, d, layer, c) for c in (0, 1, 2, 3, 4)] + [
        row3(po.shape[-1]), row3(ao.shape[-1]),
        const2(w_gate), _resident((1, 2 * d), lambda bi, ti: (0, 0)), const2(w_pu), const2(w_au), const2(w_o),
        _resident((1, d), lambda bi, ti: (0, 0)), _resident((1, d), lambda bi, ti: (0, 0)),
    ]
    args = [x, mod, mod, mod, mod, mod, po, ao, w_gate, b_gate.reshape(1, 2 * d), w_pu, w_au, w_o,
            ln_g.reshape(1, d), ln_b.reshape(1, d)]
    out_specs = [row3(d), row3(d)]
    out_shape = [jax.ShapeDtypeStruct((b, t, d), F32), jax.ShapeDtypeStruct((b, t, d), BF16)]
    if moe:
        in_specs.append(const2(w_router_t))
        args.append(w_router_t)
        out_specs += [tile2, tile2, tile2, pl.BlockSpec((1, 1, LANES), lambda bi, ti: (bi * (t // tt) + ti, 0, 0))]
        out_shape += [
            jax.ShapeDtypeStruct((n_tiles * rows, 2), I32),
            jax.ShapeDtypeStruct((n_tiles * rows, 2), I32),
            jax.ShapeDtypeStruct((n_tiles * rows, 2), F32),
            jax.ShapeDtypeStruct((n_tiles, 1, LANES), I32),
        ]
    return pl.pallas_call(
        functools.partial(_post_kernel, alpha=alpha, n_experts=n_experts),
        grid=(b // nb, t // tt),
        in_specs=in_specs,
        out_specs=out_specs,
        out_shape=out_shape,
        compiler_params=_cparams(("arbitrary", "arbitrary")),
        name="post_moe" if moe else "post",
    )(*args)


def _ffn_kernel(x1_ref, h2_ref, g2_ref, wg_ref, wu_ref, wd_ref, lng_ref, lnb_ref, o_ref, *, alpha, n_chunks):
    nb, tt, d = x1_ref.shape
    rows = nb * tt
    hb = _rows2d(h2_ref)
    f = wg_ref.shape[1]
    fc = f // n_chunks
    acc = None
    for c in range(n_chunks):
        sl = slice(c * fc, (c + 1) * fc)
        g = jnp.dot(hb, wg_ref[:, sl], preferred_element_type=F32)
        u = jnp.dot(hb, wu_ref[:, sl], preferred_element_type=F32)
        a = (g * jax.nn.sigmoid(g) * u).astype(BF16)
        part = jnp.dot(a, wd_ref[sl, :], preferred_element_type=F32)
        acc = part if acc is None else acc + part
    y = alpha * x1_ref[...] + g2_ref[...] * acc.reshape(nb, tt, d)
    o_ref[...] = _layer_norm(y, lng_ref[...], lnb_ref[...])


def _ffn_dense(x1, h2, mod, layer, wg, wu, wd, ln_g, ln_b, alpha):
    b, t, d = x1.shape
    nb, tt = _row_block(b, t)
    f = wg.shape[1]
    n_chunks = 2 if f % (2 * LANES) == 0 else 1
    row3 = pl.BlockSpec((nb, tt, d), lambda bi, ti: (bi, ti, 0))
    const2 = lambda a: _resident(a.shape, lambda bi, ti: (0, 0))
    vec = _resident((1, d), lambda bi, ti: (0, 0))
    return pl.pallas_call(
        functools.partial(_ffn_kernel, alpha=alpha, n_chunks=n_chunks),
        grid=(b // nb, t // tt),
        in_specs=[row3, row3, _mod_spec(nb, d, layer, 5), const2(wg), const2(wu), const2(wd), vec, vec],
        out_specs=row3,
        out_shape=jax.ShapeDtypeStruct((b, t, d), F32),
        compiler_params=_cparams(("arbitrary", "arbitrary")),
        name="ffn_dense",
    )(x1, h2, mod, wg, wu, wd, ln_g.reshape(1, d), ln_b.reshape(1, d))


def _moe_plan(cnt, tm, n_tiles_max):
    n_experts = cnt.shape[1]
    pc = (cnt + MOE_CHUNK - 1) // MOE_CHUNK * MOE_CHUNK
    seg_start = jnp.cumsum(pc, axis=1) - pc
    base = jnp.cumsum(pc, axis=0) - pc
    total = jnp.sum(pc, axis=0)
    tiles = (total + tm - 1) // tm
    ends = jnp.cumsum(tiles)
    starts = ends - tiles
    n_active = ends[-1]
    tile_ids = jnp.minimum(jnp.arange(n_tiles_max, dtype=I32), jnp.maximum(n_active - 1, 0))
    tile_expert = jnp.sum((tile_ids[:, None] >= ends[None, :]).astype(I32), axis=1)
    flat = lambda a: a.reshape(-1).astype(I32)
    return dict(
        seg_start=seg_start.astype(I32),
        seg=flat(seg_start), nch=flat(pc // MOE_CHUNK), gst=flat(starts[None, :] * tm + base),
        tot=flat(jnp.sum(pc // MOE_CHUNK, axis=1)), total=flat(total),
        last=flat((starts + jnp.maximum(tiles - 1, 0)) * tm),
        tile_expert=flat(jnp.minimum(tile_expert, n_experts - 1)), n_active=flat(n_active),
    )


def _dispatch_kernel(seg_ref, nch_ref, gst_ref, tot_ref, total_ref, last_ref, lpos_ref, h_ref, xs_ref, buf_ref,
                     zero_ref, sem, zsem, *, n_experts, tm, zrows):
    i = pl.program_id(0)
    n = pl.num_programs(0)
    slot = jnp.bitwise_and(i, 1)
    lrows = buf_ref.shape[1]
    rows = h_ref.shape[0]

    def chunk_copy(s, src_row, dst_row):
        return pltpu.make_async_copy(
            buf_ref.at[s, pl.ds(src_row, MOE_CHUNK)], xs_ref.at[pl.ds(dst_row, MOE_CHUNK)], sem.at[s])

    def drain(step):
        def body(c, carry):
            chunk_copy(jnp.bitwise_and(step, 1), 0, 0).wait()
            return carry

        lax.fori_loop(0, tot_ref[step], body, 0)

    @pl.when(i == 0)
    def _():
        zero_ref[...] = jnp.zeros(zero_ref.shape, BF16)

        def zero_copy(e, c):
            row = pl.multiple_of(last_ref[e] + c * zrows, zrows)
            return pltpu.make_async_copy(zero_ref, xs_ref.at[pl.ds(row, zrows)], zsem)

        for e in range(n_experts):
            @pl.when(total_ref[e] > 0)
            def _():
                for c in range(tm // zrows):
                    zero_copy(e, c).start()
        for e in range(n_experts):
            @pl.when(total_ref[e] > 0)
            def _():
                for c in range(tm // zrows):
                    zero_copy(e, c).wait()

    @pl.when(i >= 2)
    def _():
        drain(i - 2)

    lp = lpos_ref[0]
    row = lax.broadcasted_iota(I32, (lrows, rows), 0)
    sel = jnp.logical_or(row == lp[0:1, :], row == lp[1:2, :]).astype(BF16)
    buf_ref[slot] = jnp.dot(sel, h_ref[...], preferred_element_type=F32).astype(BF16)

    for e in range(n_experts):
        idx = i * n_experts + e
        src0 = seg_ref[idx]
        dst0 = gst_ref[idx]

        def body(c, carry, src0=src0, dst0=dst0):
            chunk_copy(slot, pl.multiple_of(src0 + c * MOE_CHUNK, MOE_CHUNK),
                       pl.multiple_of(dst0 + c * MOE_CHUNK, MOE_CHUNK)).start()
            return carry

        lax.fori_loop(0, nch_ref[idx], body, 0)

    @pl.when(i == n - 1)
    def _():
        @pl.when(i >= 1)
        def _():
            drain(i - 1)

        drain(i)


def _dispatch(h2, lpos_t, plan, rows, tm, n_rows_total):
    n, d = h2.shape
    n_tiles = n // rows
    n_experts = plan["total"].shape[0]
    zrows = min(tm, 256)
    lrows = _local_rows(rows, n_experts)
    grid_spec = pltpu.PrefetchScalarGridSpec(
        num_scalar_prefetch=6,
        grid=(n_tiles,),
        in_specs=[
            pl.BlockSpec((1, 2, rows), lambda i, *_: (i, 0, 0)),
            pl.BlockSpec((rows, d), lambda i, *_: (i, 0)),
        ],
        out_specs=pl.BlockSpec(memory_space=pl.ANY),
        scratch_shapes=[
            pltpu.VMEM((2, lrows, d), BF16), pltpu.VMEM((zrows, d), BF16),
            pltpu.SemaphoreType.DMA((2,)), pltpu.SemaphoreType.DMA(()),
        ],
    )
    return pl.pallas_call(
        functools.partial(_dispatch_kernel, n_experts=n_experts, tm=tm, zrows=zrows),
        grid_spec=grid_spec,
        out_shape=jax.ShapeDtypeStruct((n_rows_total, d), BF16),
        compiler_params=_cparams(("arbitrary",)),
        name="moe_dispatch",
    )(plan["seg"], plan["nch"], plan["gst"], plan["tot"], plan["total"], plan["last"], lpos_t, h2)


def _moe_ffn_kernel(te_ref, na_ref, xs_ref, wg_ref, wu_ref, wd_ref, y_ref, acc_ref):
    i = pl.program_id(0)
    j = pl.program_id(1)

    @pl.when(i < na_ref[0])
    def _():
        @pl.when(j == 0)
        def _():
            acc_ref[...] = jnp.zeros(acc_ref.shape, F32)

        xb = xs_ref[...]
        g = jnp.dot(xb, wg_ref[...], preferred_element_type=F32)
        u = jnp.dot(xb, wu_ref[...], preferred_element_type=F32)
        a = (g * jax.nn.sigmoid(g) * u).astype(BF16)
        acc_ref[...] += jnp.dot(a, wd_ref[...], preferred_element_type=F32)

        @pl.when(j == pl.num_programs(1) - 1)
        def _():
            y_ref[...] = acc_ref[...].astype(BF16)


def _moe_ffn(xs, tile_expert, n_active, wg, wu, wd, tm):
    r, d = xs.shape
    f = wg.shape[2]
    fc = MOE_FC
    assert f % fc == 0 and r % tm == 0
    nj = f // fc

    def row_map(i, j, te, na):
        return (jnp.minimum(i, jnp.maximum(na[0] - 1, 0)), 0)

    def jj(i, j, na):
        return jnp.where(i < na[0], j, nj - 1)

    grid_spec = pltpu.PrefetchScalarGridSpec(
        num_scalar_prefetch=2,
        grid=(r // tm, nj),
        in_specs=[
            pl.BlockSpec((tm, d), row_map),
            pl.BlockSpec((None, d, fc), lambda i, j, te, na: (te[i], 0, jj(i, j, na))),
            pl.BlockSpec((None, d, fc), lambda i, j, te, na: (te[i], 0, jj(i, j, na))),
            pl.BlockSpec((None, fc, d), lambda i, j, te, na: (te[i], jj(i, j, na), 0)),
        ],
        out_specs=pl.BlockSpec((tm, d), row_map),
        scratch_shapes=[pltpu.VMEM((tm, d), F32)],
    )
    return pl.pallas_call(
        _moe_ffn_kernel,
        grid_spec=grid_spec,
        out_shape=jax.ShapeDtypeStruct((r, d), BF16),
        compiler_params=_cparams(("arbitrary", "arbitrary")),
        name="moe_ffn",
    )(tile_expert, n_active, xs, wg, wu, wd)


def _combine_kernel(seg_ref, nch_ref, gst_ref, tot_ref, y_ref, x1_ref, g2_ref, lpos_ref, cw_ref, lng_ref, lnb_ref,
                    o_ref, buf_ref, sem, *, alpha, n_experts, n_tiles):
    nb, tt, d = x1_ref.shape
    rows = nb * tt
    lrows = buf_ref.shape[1]
    tile = pl.program_id(0) * pl.num_programs(1) + pl.program_id(1)
    slot = jnp.bitwise_and(tile, 1)

    def chunk_copy(s, src_row, dst_row):
        return pltpu.make_async_copy(
            y_ref.at[pl.ds(src_row, MOE_CHUNK)], buf_ref.at[s, pl.ds(dst_row, MOE_CHUNK)], sem.at[s])

    def fetch(step):
        s = jnp.bitwise_and(step, 1)
        buf_ref[s] = jnp.zeros((lrows, d), BF16)
        for e in range(n_experts):
            idx = step * n_experts + e
            src0 = gst_ref[idx]
            dst0 = seg_ref[idx]

            def body(c, carry, src0=src0, dst0=dst0):
                chunk_copy(s, pl.multiple_of(src0 + c * MOE_CHUNK, MOE_CHUNK),
                           pl.multiple_of(dst0 + c * MOE_CHUNK, MOE_CHUNK)).start()
                return carry

            lax.fori_loop(0, nch_ref[idx], body, 0)

    @pl.when(tile == 0)
    def _():
        fetch(tile)

    @pl.when(tile + 1 < n_tiles)
    def _():
        fetch(tile + 1)

    def wait(c, carry):
        chunk_copy(slot, 0, 0).wait()
        return carry

    lax.fori_loop(0, tot_ref[tile], wait, 0)
    lp = lpos_ref[...]
    lane = lax.broadcasted_iota(I32, (rows, lrows), 1)
    pick = jnp.concatenate([lane == lp[:, 0:1], lane == lp[:, 1:2]], axis=0).astype(BF16)
    g = jnp.dot(pick, buf_ref[slot], preferred_element_type=F32)
    cw = cw_ref[...]
    f = cw[:, 0:1] * g[:rows] + cw[:, 1:2] * g[rows:]
    y = alpha * x1_ref[...] + g2_ref[...] * f.reshape(nb, tt, d)
    o_ref[...] = _layer_norm(y, lng_ref[...], lnb_ref[...])


def _combine(y, x1, mod, layer, lpos, cw, plan, ln_g, ln_b, alpha):
    b, t, d = x1.shape
    nb, tt = _row_block(b, t)
    rows = nb * tt
    tpb = t // tt
    n_tiles = (b // nb) * tpb
    n_experts = plan["total"].shape[0]
    row3 = pl.BlockSpec((nb, tt, d), lambda bi, ti, *_: (bi, ti, 0))
    tile2 = pl.BlockSpec((rows, 2), lambda bi, ti, *_: (bi * tpb + ti, 0))
    vec = _resident((1, d), lambda bi, ti, *_: (0, 0))
    grid_spec = pltpu.PrefetchScalarGridSpec(
        num_scalar_prefetch=4,
        grid=(b // nb, tpb),
        in_specs=[
            pl.BlockSpec(memory_space=pl.ANY), row3,
            pl.BlockSpec((None, nb, 1, d), lambda bi, ti, *_: (layer, bi, 0, 5)), tile2, tile2, vec, vec,
        ],
        out_specs=row3,
        scratch_shapes=[pltpu.VMEM((2, _local_rows(rows, n_experts), d), BF16), pltpu.SemaphoreType.DMA((2,))],
    )
    return pl.pallas_call(
        functools.partial(_combine_kernel, alpha=alpha, n_experts=n_experts, n_tiles=n_tiles),
        grid_spec=grid_spec,
        out_shape=jax.ShapeDtypeStruct((b, t, d), F32),
        compiler_params=_cparams(("arbitrary", "arbitrary")),
        name="moe_combine",
    )(plan["seg"], plan["nch"], plan["gst"], plan["tot"], y, x1, mod, lpos, cw, ln_g.reshape(1, d),
      ln_b.reshape(1, d))


def _local_rows(rows, n_experts):
    need = 2 * rows + n_experts * (MOE_CHUNK - 1)
    return (need + LANES - 1) // LANES * LANES


def _moe(x1, h2, mod, layer, eidx, lrank, cw, cnt_tile, wg, wu, wd, ln_g, ln_b, alpha):
    b, t, d = x1.shape
    n = b * t
    nb, tt = _row_block(b, t)
    rows = nb * tt
    n_tiles = n // rows
    n_experts = wg.shape[0]
    tm = MOE_TM_LARGE if 2 * n >= 8 * n_experts * MOE_TM_LARGE else MOE_TM_SMALL
    n_tiles_max = -(-(2 * n + n_tiles * n_experts * (MOE_CHUNK - 1)) // tm) + n_experts
    plan = _moe_plan(cnt_tile[:, 0, :n_experts], tm, n_tiles_max)
    e3 = eidx.reshape(n_tiles, rows, 2)
    onehot = e3[..., None] == jnp.arange(n_experts, dtype=I32)
    seg_of_pair = jnp.sum(jnp.where(onehot, plan["seg_start"][:, None, None, :], 0), axis=-1)
    lpos = lrank.reshape(n_tiles, rows, 2) + seg_of_pair
    xs = _dispatch(h2.reshape(n, d), jnp.transpose(lpos, (0, 2, 1)), plan, rows, tm, n_tiles_max * tm)
    y = _moe_ffn(xs, plan["tile_expert"], plan["n_active"], wg, wu, wd, tm)
    return _combine(y, x1, mod, layer, lpos.reshape(n, 2), cw, plan, ln_g, ln_b, alpha)


def _trunk(x, mod, pos0, pool_hist, k_past, v_past, biases, W, depth):
    b, t, d = x.shape
    n_heads = d // HEAD_WIDTH
    qkw = n_heads * HEAD_WIDTH
    alpha = (2.0 * depth) ** 0.25
    k5 = v5 = None
    pool_states = []
    o4 = POOL_WIDTH + 3 * qkw
    for l in range(depth):
        lam_init = 0.8 - 0.6 * math.exp(-0.3 * l)
        w_a = W["w_in"][l, :, :o4].astype(BF16)
        w_gate = W["w_in"][l, :, o4:].astype(BF16)
        hist = None if pool_hist is None else pool_hist[l]
        q, kb, vb, k5, v5, po, pst = _inproj(
            x, mod, l, w_a, W["pool_w"][l].astype(BF16), W["pool_scale"][l], hist, k5, v5, depth, pos0)
        pool_states.append(pst)
        if k_past is None:
            ao = _attn_prompt(q, kb, vb, W["lam_qk"][l], W["subln_g"][l], biases[0], biases[1], lam_init)
        else:
            ao = _attn_sample(q, kb, vb, k_past, v_past, l, W["lam_qk"][l], W["subln_g"][l], biases[0], biases[1],
                              lam_init)
        moe = l % 2 == 1
        i = l // 2
        w_router_t = jnp.transpose(W["w_router"][i]) if moe else None
        outs = _post(x, mod, l, po, ao, w_gate, W["b_gate"][l], W["w_pool_up"][l].astype(BF16),
                     W["w_attn_up"][l].astype(BF16), W["w_o"][l].astype(BF16), W["ln_g"][l, 0], W["ln_b"][l, 0],
                     w_router_t, alpha)
        if not moe:
            x1, h2 = outs
            x = _ffn_dense(x1, h2, mod, l, W["w_ffn_gate"][i].astype(BF16), W["w_ffn_up"][i].astype(BF16),
                           W["w_ffn_down"][i].astype(BF16), W["ln_g"][l, 1], W["ln_b"][l, 1], alpha)
        else:
            x1, h2, eidx, rank, cw, cnt = outs
            x = _moe(x1, h2, mod, l, eidx, rank, cw, cnt, W["w_exp_gate"][i].astype(BF16),
                     W["w_exp_up"][i].astype(BF16), W["w_exp_down"][i].astype(BF16), W["ln_g"][l, 1], W["ln_b"][l, 1],
                     alpha)
    return x, k5, v5, jnp.stack(pool_states)


def kernel(x_prompt, x_sample, cache_k, cache_v, state_pool, c_prompt, c_sample, rel_bias, w_ada, b_ada, w_in, b_gate,
           pool_w, pool_scale, lam_qk, subln_g, w_pool_up, w_attn_up, w_o, ln_g, ln_b, w_ffn_gate, w_ffn_up,
           w_ffn_down, w_router, w_exp_gate, w_exp_up, w_exp_down):
    W = dict(w_in=w_in, b_gate=b_gate, pool_w=pool_w, pool_scale=pool_scale, lam_qk=lam_qk, subln_g=subln_g,
             w_pool_up=w_pool_up, w_attn_up=w_attn_up, w_o=w_o, ln_g=ln_g, ln_b=ln_b, w_ffn_gate=w_ffn_gate,
             w_ffn_up=w_ffn_up, w_ffn_down=w_ffn_down, w_router=w_router, w_exp_gate=w_exp_gate, w_exp_up=w_exp_up,
             w_exp_down=w_exp_down)
    depth, d, _ = w_in.shape
    bp, s, _ = x_prompt.shape
    bs, ts, _ = x_sample.shape
    p = cache_k.shape[2]

    mod = _ada(jnp.concatenate([c_prompt, c_sample], axis=0), w_ada, b_ada)
    mod_p = mod[:, :bp].reshape(depth, bp, 1, 6 * d)
    mod_s = mod[:, bp:].reshape(depth, bs, 1, 6 * d)

    tq = ATTN_TQ
    q_pos = np.arange(tq, 2 * tq)
    bias_prev = _bias_table(rel_bias, _bucket_table(q_pos, np.arange(0, tq)))
    bias_diag = _bias_table(rel_bias, _bucket_table(q_pos, q_pos))
    s_pos = p + np.arange(ts)
    bias_past = _bias_table(rel_bias, _bucket_table(s_pos, np.arange(p)))
    bias_new = _bias_table(rel_bias, _bucket_table(s_pos, s_pos))

    y_p, k_p, v_p, pool_p = _trunk(x_prompt, mod_p, 0, None, None, None, (bias_prev, bias_diag), W, depth)
    y_s, k_s, v_s, pool_s = _trunk(x_sample, mod_s, p, state_pool, cache_k, cache_v, (bias_past, bias_new), W, depth)
    return (y_p, y_s, k_p, v_p, pool_p, k_s, v_s, pool_s)
```

```python
import functools
import math

import numpy as np
import jax
import jax.numpy as jnp
from jax import lax
from jax.experimental import pallas as pl
from jax.experimental.pallas import tpu as pltpu

F32 = jnp.float32
BF16 = jnp.bfloat16
I32 = jnp.int32

CHUNK = 64
HEAD_DIM = 64
HEAD_WIDTH = 2 * HEAD_DIM
POOL_WINDOWS = (2, 4, 8, 16)
POOL_GROUP_DIM = 128
POOL_WIDTH = len(POOL_WINDOWS) * POOL_GROUP_DIM
POOL_HIST = max(POOL_WINDOWS) - 1
HIST_ROWS = 16
N_BUCKETS = 32
MAX_DISTANCE = 128
FAR_BUCKET = N_BUCKETS // 2 - 1
LN_EPS = 1e-5
RMS_EPS = 1e-5
MASKED = -1e30
LOG2E = math.log2(math.e)

LANES = 128
ROW_TILE = 512
ATTN_TQ = 512
ATTN_TK = 512
ATTN_HEADS_PER_STEP = 4
MOE_CHUNK = 16
MOE_TM_LARGE = 1024
MOE_TM_SMALL = 256
MOE_FC = 512
VMEM_LIMIT = 56 * 1024 * 1024


def _cparams(sem):
    return pltpu.CompilerParams(dimension_semantics=sem, vmem_limit_bytes=VMEM_LIMIT)


def _resident(shape, index_map):
    return pl.BlockSpec(shape, index_map, pipeline_mode=pl.Buffered(1))


def _row_block(b, t):
    if t >= ROW_TILE:
        assert t % ROW_TILE == 0
        return 1, ROW_TILE
    nb = min(b, ROW_TILE // t)
    assert b % nb == 0 and t % 8 == 0
    return nb, t


def _rows2d(ref):
    nb, tt, w = ref.shape
    return ref[0] if nb == 1 else ref[...].reshape(nb * tt, w)


def _layer_norm(y, g, b):
    mu = jnp.mean(y, axis=-1, keepdims=True)
    yc = y - mu
    var = jnp.mean(yc * yc, axis=-1, keepdims=True)
    return yc * lax.rsqrt(var + LN_EPS) * g + b


def _ada_kernel(c_ref, w_ref, b_ref, o_ref):
    c = c_ref[...]
    s = c * jax.nn.sigmoid(c)
    o_ref[0] = jnp.dot(s.astype(BF16), w_ref[0].astype(BF16), preferred_element_type=F32) + b_ref[0]


def _ada(c_all, w_ada, b_ada):
    depth, d, n6 = w_ada.shape
    bc = c_all.shape[0]
    tn = 1536 if n6 % 1536 == 0 else n6
    return pl.pallas_call(
        _ada_kernel,
        grid=(depth, n6 // tn),
        in_specs=[
            pl.BlockSpec((bc, d), lambda l, j: (0, 0)),
            pl.BlockSpec((1, d, tn), lambda l, j: (l, 0, j)),
            pl.BlockSpec((1, 1, tn), lambda l, j: (l, 0, j)),
        ],
        out_specs=pl.BlockSpec((1, bc, tn), lambda l, j: (l, 0, j)),
        out_shape=jax.ShapeDtypeStruct((depth, bc, n6), F32),
        compiler_params=_cparams(("arbitrary", "arbitrary")),
        name="ada",
    )(c_all, w_ada, b_ada.reshape(depth, 1, n6))


def _mod_spec(nb, d, layer, chunk):
    return pl.BlockSpec((None, nb, 1, d), lambda bi, ti: (layer, bi, 0, chunk))


def _rel_bucket_np(rel):
    nb = N_BUCKETS // 2
    max_exact = nb // 2
    n = np.abs(rel)
    large = max_exact + (
        np.log(np.maximum(n, 1).astype(np.float32) / np.float32(max_exact))
        / np.float32(math.log(MAX_DISTANCE / max_exact))
        * np.float32(nb - max_exact)
    ).astype(np.int32)
    large = np.minimum(large, nb - 1)
    return np.where(rel > 0, nb, 0) + np.where(n < max_exact, n, large)


def _bucket_table(q_pos, k_pos):
    rel = k_pos[None, :] - q_pos[:, None]
    allowed = (k_pos[None, :] // CHUNK) <= (q_pos[:, None] // CHUNK)
    return np.where(allowed, _rel_bucket_np(rel), -1).astype(np.int32)


def _bias_kernel(rb_ref, bk_ref, o_ref):
    h = pl.program_id(0)
    bk = bk_ref[...]
    far = rb_ref[FAR_BUCKET, h]
    acc = jnp.zeros(bk.shape, F32)
    for b in range(N_BUCKETS):
        acc = jnp.where(bk == b, (rb_ref[b, h] - far) * LOG2E, acc)
    o_ref[0] = jnp.where(bk < 0, MASKED, acc)


def _bias_table(rel_bias, bucket_np):
    rows, cols = bucket_np.shape
    n_heads = rel_bias.shape[1]
    return pl.pallas_call(
        _bias_kernel,
        grid=(n_heads,),
        in_specs=[
            pl.BlockSpec(memory_space=pltpu.SMEM),
            pl.BlockSpec((rows, cols), lambda h: (0, 0)),
        ],
        out_specs=pl.BlockSpec((1, rows, cols), lambda h: (h, 0, 0)),
        out_shape=jax.ShapeDtypeStruct((n_heads, rows, cols), F32),
        compiler_params=_cparams(("arbitrary",)),
        name="bias_table",
    )(rel_bias, jnp.asarray(bucket_np))


def _inproj_kernel(*refs, pos0, has_hist, has_alias, n_heads):
    it = iter(refs)
    x_ref, sh_ref, sc_ref, w_ref, pw_ref, ps_ref = (next(it) for _ in range(6))
    hist_ref = next(it) if has_hist else None
    if has_alias:
        next(it)
        next(it)
    q_ref, kb_ref, vb_ref, k_ref, v_ref, po_ref, pst_ref, ext_ref = (next(it) for _ in range(8))

    ti = pl.program_id(1)
    nb, tt, d = x_ref.shape
    rows = nb * tt
    qkw = n_heads * HEAD_WIDTH
    o1 = POOL_WIDTH
    o2 = o1 + qkw
    o3 = o2 + qkw
    o4 = o3 + qkw

    @pl.when(ti == 0)
    def _():
        if has_hist:
            ext_ref[:, 0:1, :] = jnp.zeros((nb, 1, POOL_WIDTH), F32)
            ext_ref[:, 1:HIST_ROWS, :] = hist_ref[...]
        else:
            ext_ref[:, 0:HIST_ROWS, :] = jnp.zeros((nb, HIST_ROWS, POOL_WIDTH), F32)

    h = x_ref[...] * (1.0 + sc_ref[...]) + sh_ref[...]
    hb = h.reshape(rows, d).astype(BF16)

    u = jnp.dot(hb, w_ref[:, 0:o1], preferred_element_type=F32)
    q = jnp.dot(hb, w_ref[:, o1:o2], preferred_element_type=F32)
    q_ref[...] = (q * (HEAD_DIM ** -0.5 * LOG2E)).reshape(nb, tt, qkw).astype(BF16)
    k = jnp.dot(hb, w_ref[:, o2:o3], preferred_element_type=F32)
    kb_ref[...] = k.reshape(nb, tt, qkw).astype(BF16)
    v = jnp.dot(hb, w_ref[:, o3:o4], preferred_element_type=F32)
    vb_ref[...] = v.reshape(nb, tt, qkw).astype(BF16)
    k_ref[...] = k.reshape(nb, tt, n_heads, HEAD_WIDTH)
    v_ref[...] = v.reshape(nb, tt, n_heads, HEAD_WIDTH)

    ext_ref[:, HIST_ROWS:HIST_ROWS + tt, :] = u.reshape(nb, tt, POOL_WIDTH)
    pos = pos0 + ti * tt + lax.broadcasted_iota(I32, (1, tt, 1), 1)
    outs = []
    for gi, w in enumerate(POOL_WINDOWS):
        ls = slice(gi * POOL_GROUP_DIM, (gi + 1) * POOL_GROUP_DIM)
        tot = ext_ref[:, HIST_ROWS:HIST_ROWS + tt, ls]
        for j in range(1, w):
            tot = tot + ext_ref[:, HIST_ROWS - j:HIST_ROWS - j + tt, ls]
        cnt = jnp.minimum(pos + 1, w).astype(F32)
        dlt = (tot / cnt - ext_ref[:, HIST_ROWS:HIST_ROWS + tt, ls]).reshape(rows, POOL_GROUP_DIM)
        outs.append(jnp.dot(dlt.astype(BF16), pw_ref[gi], preferred_element_type=F32))
    po = jnp.concatenate(outs, axis=-1) * ps_ref[...]
    po_ref[...] = po.reshape(nb, tt, POOL_WIDTH).astype(BF16)
    pst_ref[...] = ext_ref[:, tt + 1:tt + HIST_ROWS, :]
    ext_ref[:, 0:HIST_ROWS, :] = ext_ref[:, tt:tt + HIST_ROWS, :]


def _inproj(x, mod, layer, w_a, pool_w, pool_scale, hist, k5, v5, depth, pos0):
    b, t, d = x.shape
    nb, tt = _row_block(b, t)
    n_heads = d // HEAD_WIDTH
    qkw = n_heads * HEAD_WIDTH
    has_hist = hist is not None
    has_alias = k5 is not None
    row3 = lambda w: pl.BlockSpec((nb, tt, w), lambda bi, ti: (bi, ti, 0))
    cache_spec = pl.BlockSpec((None, nb, tt, n_heads, HEAD_WIDTH), lambda bi, ti: (layer, bi, ti, 0, 0))
    in_specs = [
        row3(d),
        _mod_spec(nb, d, layer, 0),
        _mod_spec(nb, d, layer, 1),
        _resident(w_a.shape, lambda bi, ti: (0, 0)),
        _resident(pool_w.shape, lambda bi, ti: (0, 0, 0)),
        _resident((1, POOL_WIDTH), lambda bi, ti: (0, 0)),
    ]
    args = [x, mod, mod, w_a, pool_w, pool_scale.reshape(1, POOL_WIDTH)]
    if has_hist:
        in_specs.append(pl.BlockSpec((nb, POOL_HIST, POOL_WIDTH), lambda bi, ti: (bi, 0, 0)))
        args.append(hist)
    aliases = {}
    if has_alias:
        aliases = {len(args): 3, len(args) + 1: 4}
        in_specs += [pl.BlockSpec(memory_space=pl.ANY), pl.BlockSpec(memory_space=pl.ANY)]
        args += [k5, v5]
    cache_shape = jax.ShapeDtypeStruct((depth, b, t, n_heads, HEAD_WIDTH), F32)
    outs = pl.pallas_call(
        functools.partial(_inproj_kernel, pos0=pos0, has_hist=has_hist, has_alias=has_alias, n_heads=n_heads),
        grid=(b // nb, t // tt),
        in_specs=in_specs,
        out_specs=[
            row3(qkw), row3(qkw), row3(qkw), cache_spec, cache_spec, row3(POOL_WIDTH),
            pl.BlockSpec((nb, POOL_HIST, POOL_WIDTH), lambda bi, ti: (bi, 0, 0)),
        ],
        out_shape=[
            jax.ShapeDtypeStruct((b, t, qkw), BF16),
            jax.ShapeDtypeStruct((b, t, qkw), BF16),
            jax.ShapeDtypeStruct((b, t, qkw), BF16),
            cache_shape, cache_shape,
            jax.ShapeDtypeStruct((b, t, POOL_WIDTH), BF16),
            jax.ShapeDtypeStruct((b, POOL_HIST, POOL_WIDTH), F32),
        ],
        scratch_shapes=[pltpu.VMEM((nb, tt + HIST_ROWS, POOL_WIDTH), F32)],
        input_output_aliases=aliases,
        compiler_params=_cparams(("arbitrary", "arbitrary")),
        name="inproj",
    )(*args)
    return outs


def _lambda(lam_ref, lam_init):
    lq = lam_ref[...]
    s01 = jnp.sum(lq[0:1, :] * lq[1:2, :], axis=-1, keepdims=True)
    s23 = jnp.sum(lq[2:3, :] * lq[3:4, :], axis=-1, keepdims=True)
    return jnp.exp(s01) - jnp.exp(s23) + lam_init


def _stack_maps(q):
    lane = lax.broadcasted_iota(I32, q.shape, 1)
    zero = jnp.zeros_like(q)
    return jnp.concatenate([jnp.where(lane < HEAD_DIM, q, zero), jnp.where(lane >= HEAD_DIM, q, zero)], axis=0)


def _diff_out(acc, l, lam, g, lam_init, rows):
    o = acc / l
    o = o[:rows] - lam * o[rows:]
    o = o * lax.rsqrt(jnp.mean(o * o, axis=-1, keepdims=True) + RMS_EPS) * g * (1.0 - lam_init)
    return o.astype(BF16)


def _nt_dot(a, b):
    return lax.dot_general(a, b, (((1,), (1,)), ((), ())), preferred_element_type=F32)


def _attn_prompt_kernel(lam_ref, g_ref, q_ref, k_ref, v_ref, bprev_ref, bdiag_ref, o_ref, qs_ref, m_ref, acc_ref,
                        *, lam_init, hg):
    qi = pl.program_id(2)
    tq = q_ref.shape[1]
    tk = ATTN_TK
    for h in range(hg):
        qs_ref[h] = _stack_maps(q_ref[0, :, h * HEAD_WIDTH:(h + 1) * HEAD_WIDTH])
    hq = tq // 2
    lower = (slice(hq, tq), slice(tq + hq, 2 * tq))

    def update(start, width, bias, first=False, lower_half=False):
        ones = jnp.ones((width, LANES), BF16)
        for h in range(hg):
            sl = slice(h * HEAD_WIDTH, (h + 1) * HEAD_WIDTH)
            lhs = jnp.concatenate([qs_ref[h, r] for r in lower], axis=0) if lower_half else qs_ref[h]
            s = _nt_dot(lhs, k_ref[0, pl.ds(start, width), sl])
            if bias is not None:
                tile_bias = bias(h)
                s = s + jnp.concatenate([tile_bias, tile_bias], axis=0)
            m_new = jnp.broadcast_to(jnp.max(s, axis=-1, keepdims=True), (s.shape[0], LANES))
            if not first:
                m_prev = jnp.concatenate([m_ref[h, r] for r in lower], axis=0) if lower_half else m_ref[h]
                m_new = jnp.maximum(m_prev, m_new)
            pb = jnp.concatenate(
                [jnp.exp2(s[:, c * LANES:(c + 1) * LANES] - m_new).astype(BF16) for c in range(width // LANES)],
                axis=-1)
            v_ext = jnp.concatenate([v_ref[0, pl.ds(start, width), sl], ones], axis=-1)
            acc = jnp.dot(pb, v_ext, preferred_element_type=F32)
            if not first:
                alpha = jnp.exp2(m_prev - m_new)
                acc_prev = jnp.concatenate([acc_ref[h, r] for r in lower], axis=0) if lower_half else acc_ref[h]
                acc = jnp.concatenate([alpha, alpha], axis=-1) * acc_prev + acc
            if lower_half:
                for n, r in enumerate(lower):
                    m_ref[h, r] = m_new[n * hq:(n + 1) * hq]
                    acc_ref[h, r] = acc[n * hq:(n + 1) * hq]
            else:
                m_ref[h] = m_new
                acc_ref[h] = acc

    diag = pl.multiple_of(qi * tk, tk)
    update(diag, tk // 2, lambda h: bdiag_ref[h, :, 0:tk // 2], first=True)
    update(pl.multiple_of(diag + tk // 2, tk // 2), tk // 2, lambda h: bdiag_ref[h, hq:tq, tk // 2:tk],
           lower_half=True)

    @pl.when(qi >= 1)
    def _():
        update(pl.multiple_of((qi - 1) * tk, tk), tk, lambda h: bprev_ref[h])

    def far_body(j, carry):
        update(pl.multiple_of(j * tk, tk), tk, None)
        return carry

    lax.fori_loop(0, jnp.maximum(qi - 1, 0), far_body, 0)
    lam = _lambda(lam_ref, lam_init)
    for h in range(hg):
        acc = acc_ref[h]
        o_ref[0, :, h * HEAD_WIDTH:(h + 1) * HEAD_WIDTH] = _diff_out(
            acc[:, :HEAD_WIDTH], acc[:, HEAD_WIDTH:], lam, g_ref[...], lam_init, tq)


def _attn_prompt(q, kb, vb, lam_qk, subln_g, bias_prev, bias_diag, lam_init):
    b, t, qkw = q.shape
    n_heads = qkw // HEAD_WIDTH
    tq = ATTN_TQ
    hg = math.gcd(n_heads, ATTN_HEADS_PER_STEP)
    gw = hg * HEAD_WIDTH
    assert tq == ATTN_TK and t % tq == 0 and (tq // 2) % CHUNK == 0
    return pl.pallas_call(
        functools.partial(_attn_prompt_kernel, lam_init=lam_init, hg=hg),
        grid=(n_heads // hg, b, t // tq),
        in_specs=[
            pl.BlockSpec(lam_qk.shape, lambda g, bi, qi: (0, 0)),
            pl.BlockSpec((1, HEAD_WIDTH), lambda g, bi, qi: (0, 0)),
            pl.BlockSpec((1, tq, gw), lambda g, bi, qi: (bi, qi, g)),
            pl.BlockSpec((1, t, gw), lambda g, bi, qi: (bi, 0, g)),
            pl.BlockSpec((1, t, gw), lambda g, bi, qi: (bi, 0, g)),
            pl.BlockSpec((hg, tq, ATTN_TK), lambda g, bi, qi: (g, 0, 0), pipeline_mode=pl.Buffered(1)),
            pl.BlockSpec((hg, tq, ATTN_TK), lambda g, bi, qi: (g, 0, 0), pipeline_mode=pl.Buffered(1)),
        ],
        out_specs=pl.BlockSpec((1, tq, gw), lambda g, bi, qi: (bi, qi, g)),
        out_shape=jax.ShapeDtypeStruct((b, t, qkw), BF16),
        scratch_shapes=[
            pltpu.VMEM((hg, 2 * tq, HEAD_WIDTH), BF16),
            pltpu.VMEM((hg, 2 * tq, LANES), F32),
            pltpu.VMEM((hg, 2 * tq, HEAD_WIDTH + LANES), F32),
        ],
        compiler_params=_cparams(("arbitrary", "arbitrary", "arbitrary")),
        name="attn_prompt",
    )(lam_qk, subln_g.reshape(1, HEAD_WIDTH), q, kb, vb, bias_prev, bias_diag)


def _attn_sample_kernel(lam_ref, g_ref, q_ref, kn_ref, vn_ref, kp_ref, vp_ref, bpast_ref, bnew_ref, o_ref,
                        *, lam_init, n_heads):
    ts = q_ref.shape[1]
    lam = _lambda(lam_ref, lam_init)
    past_len = kp_ref.shape[1]
    kp_all = kp_ref[0].reshape(past_len, n_heads * HEAD_WIDTH).astype(BF16)
    vp_all = vp_ref[0].reshape(past_len, n_heads * HEAD_WIDTH).astype(BF16)
    for hh in range(n_heads):
        sl = slice(hh * HEAD_WIDTH, (hh + 1) * HEAD_WIDTH)
        qs = _stack_maps(q_ref[0, :, sl])
        kp = kp_all[:, sl]
        vp = vp_all[:, sl]
        bp = bpast_ref[hh]
        bn = bnew_ref[hh]
        s_p = _nt_dot(qs, kp) + jnp.concatenate([bp, bp], axis=0)
        s_n = _nt_dot(qs, kn_ref[0, :, sl]) + jnp.concatenate([bn, bn], axis=0)
        m = jnp.maximum(jnp.max(s_p, axis=-1, keepdims=True), jnp.max(s_n, axis=-1, keepdims=True))
        p_p = jnp.exp2(s_p - m)
        p_n = jnp.exp2(s_n - m)
        l = jnp.sum(p_p, axis=-1, keepdims=True) + jnp.sum(p_n, axis=-1, keepdims=True)
        acc = jnp.dot(p_p.astype(BF16), vp, preferred_element_type=F32) + jnp.dot(
            p_n.astype(BF16), vn_ref[0, :, sl], preferred_element_type=F32)
        o_ref[0, :, sl] = _diff_out(acc, l, lam, g_ref[...], lam_init, ts)


def _attn_sample(q, kb, vb, k_past, v_past, layer, lam_qk, subln_g, bias_past, bias_new, lam_init):
    b, ts, qkw = q.shape
    n_heads = qkw // HEAD_WIDTH
    p = k_past.shape[2]
    row = pl.BlockSpec((1, ts, qkw), lambda bi: (bi, 0, 0))
    past = pl.BlockSpec((None, 1, p, n_heads, HEAD_WIDTH), lambda bi: (layer, bi, 0, 0, 0))
    return pl.pallas_call(
        functools.partial(_attn_sample_kernel, lam_init=lam_init, n_heads=n_heads),
        grid=(b,),
        in_specs=[
            pl.BlockSpec(lam_qk.shape, lambda bi: (0, 0)),
            pl.BlockSpec((1, HEAD_WIDTH), lambda bi: (0, 0)),
            row, row, row, past, past,
            pl.BlockSpec(bias_past.shape, lambda bi: (0, 0, 0)),
            pl.BlockSpec(bias_new.shape, lambda bi: (0, 0, 0)),
        ],
        out_specs=row,
        out_shape=jax.ShapeDtypeStruct((b, ts, qkw), BF16),
        compiler_params=_cparams(("arbitrary",)),
        name="attn_sample",
    )(lam_qk, subln_g.reshape(1, HEAD_WIDTH), q, kb, vb, k_past, v_past, bias_past, bias_new)


def _post_kernel(*refs, alpha, n_experts):
    moe = n_experts > 0
    it = iter(refs)
    (x_ref, sh1_ref, sc1_ref, g1_ref, sh2_ref, sc2_ref, po_ref, ao_ref, wgate_ref, bgate_ref, wpu_ref, wau_ref,
     wo_ref, lng_ref, lnb_ref) = (next(it) for _ in range(15))
    wr_ref = next(it) if moe else None
    x1_ref = next(it)
    h2_ref = next(it)
    if moe:
        eidx_ref, rank_ref, cw_ref, cnt_ref, hprev_ref = (next(it) for _ in range(5))

    nb, tt, d = x_ref.shape
    rows = nb * tt
    if moe:
        @pl.when(pl.program_id(0) == 0)
        def _():
            hprev_ref[...] = jnp.zeros(hprev_ref.shape, F32)

        h2r = hprev_ref[...]
    x = x_ref[...]
    hb = (x * (1.0 + sc1_ref[...]) + sh1_ref[...]).reshape(rows, d).astype(BF16)
    gates = jax.nn.sigmoid(jnp.dot(hb, wgate_ref[...], preferred_element_type=F32) + bgate_ref[...])
    pu = jnp.dot(_rows2d(po_ref), wpu_ref[...], preferred_element_type=F32)
    au = jnp.dot(_rows2d(ao_ref), wau_ref[...], preferred_element_type=F32)
    merged = gates[:, :d] * pu + gates[:, d:] * au
    mix = jnp.dot(merged.astype(BF16), wo_ref[...], preferred_element_type=F32)
    y = alpha * x + g1_ref[...] * mix.reshape(nb, tt, d)
    x1 = _layer_norm(y, lng_ref[...], lnb_ref[...])
    x1_ref[...] = x1
    h2 = x1 * (1.0 + sc2_ref[...]) + sh2_ref[...]
    h2_ref[...] = h2.astype(BF16)
    if not moe:
        return
    hprev_ref[...] = h2.reshape(rows, d)

    lane =lax.broadcasted_iota(I32, (rows, LANES), 1)
    lanef = lane.astype(F32)
    logits = jnp.full((rows, LANES), -jnp.inf, F32)
    for e in range(n_experts):
        col = jnp.sum(h2r * wr_ref[e:e + 1, :], axis=-1, keepdims=True)
        logits = jnp.where(lane == e, col, logits)
    m1 = jnp.max(logits, axis=-1, keepdims=True)
    i1 = jnp.min(jnp.where(logits == m1, lanef, float(LANES)), axis=-1, keepdims=True)
    rest = jnp.where(lanef == i1, -jnp.inf, logits)
    m2 = jnp.max(rest, axis=-1, keepdims=True)
    i2 = jnp.min(jnp.where(rest == m2, lanef, float(LANES)), axis=-1, keepdims=True)
    t2 = jnp.exp(m2 - m1)
    w1 = 1.0 / (1.0 + t2)
    w2 = t2 / (1.0 + t2)
    oh1 = lanef == i1
    oh2 = lanef == i2
    oh = jnp.logical_or(oh1, oh2).astype(F32)
    tri = (lax.broadcasted_iota(I32, (rows, rows), 0) >= lax.broadcasted_iota(I32, (rows, rows), 1)).astype(BF16)
    before = jnp.dot(tri, oh.astype(BF16), preferred_element_type=F32) - oh
    r1 = jnp.sum(jnp.where(oh1, before, 0.0), axis=-1, keepdims=True)
    r2 = jnp.sum(jnp.where(oh2, before, 0.0), axis=-1, keepdims=True)
    cnt_ref[0] = jnp.sum(oh, axis=0, keepdims=True).astype(I32)
    slot = lax.broadcasted_iota(I32, (rows, 2), 1)
    eidx_ref[...] = jnp.where(slot == 0, i1, i2).astype(I32)
    rank_ref[...] = jnp.where(slot == 0, r1, r2).astype(I32)
    cw_ref[...] = jnp.where(slot == 0, w1, w2)


def _post(x, mod, layer, po, ao, w_gate, b_gate, w_pu, w_au, w_o, ln_g, ln_b, w_router_t, alpha):
    b, t, d = x.shape
    nb, tt = _row_block(b, t)
    rows = nb * tt
    assert nb == 1 or tt == t
    moe = w_router_t is not None
    n_experts = w_router_t.shape[0] if moe else 0
    tpb = t // tt
    n_tiles = (b // nb) * tpb
    lag = 1 if moe else 0
    tile = lambda i: jnp.minimum(i, n_tiles - 1)
    row3 = lambda w: pl.BlockSpec((nb, tt, w), lambda i: (tile(i) // tpb, tile(i) % tpb, 0))
    modc = lambda c: pl.BlockSpec((None, nb, 1, d), lambda i: (layer, tile(i) // tpb, 0, c))
    const2 = lambda a: _resident(a.shape, lambda i: (0, 0))
    tile2 = pl.BlockSpec((rows, 2), lambda i: (jnp.maximum(i - lag, 0), 0))
    in_specs = [row3(d)] + [modc(c) for c in (0, 1, 2, 3, 4)] + [
        row3(po.shape[-1]), row3(ao.shape[-1]),
        const2(w_gate), _resident((1, 2 * d), lambda i: (0, 0)), const2(w_pu), const2(w_au), const2(w_o),
        _resident((1, d), lambda i: (0, 0)), _resident((1, d), lambda i: (0, 0)),
    ]
    args = [x, mod, mod, mod, mod, mod, po, ao, w_gate, b_gate.reshape(1, 2 * d), w_pu, w_au, w_o,
            ln_g.reshape(1, d), ln_b.reshape(1, d)]
    out_specs = [row3(d), row3(d)]
    out_shape = [jax.ShapeDtypeStruct((b, t, d), F32), jax.ShapeDtypeStruct((b, t, d), BF16)]
    scratch = []
    if moe:
        in_specs.append(const2(w_router_t))
        args.append(w_router_t)
        out_specs += [tile2, tile2, tile2, pl.BlockSpec((1, 1, LANES), lambda i: (jnp.maximum(i - lag, 0), 0, 0))]
        out_shape += [
            jax.ShapeDtypeStruct((n_tiles * rows, 2), I32),
            jax.ShapeDtypeStruct((n_tiles * rows, 2), I32),
            jax.ShapeDtypeStruct((n_tiles * rows, 2), F32),
            jax.ShapeDtypeStruct((n_tiles, 1, LANES), I32),
        ]
        scratch = [pltpu.VMEM((rows, d), F32)]
    return pl.pallas_call(
        functools.partial(_post_kernel, alpha=alpha, n_experts=n_experts),
        grid=(n_tiles + lag,),
        in_specs=in_specs,
        out_specs=out_specs,
        out_shape=out_shape,
        scratch_shapes=scratch,
        compiler_params=_cparams(("arbitrary",)),
        name="post_moe" if moe else "post",
    )(*args)


def _ffn_kernel(x1_ref, h2_ref, g2_ref, wg_ref, wu_ref, wd_ref, lng_ref, lnb_ref, o_ref, *, alpha, n_chunks):
    nb, tt, d = x1_ref.shape
    rows = nb * tt
    hb = _rows2d(h2_ref)
    f = wg_ref.shape[1]
    fc = f // n_chunks
    acc = None
    for c in range(n_chunks):
        sl = slice(c * fc, (c + 1) * fc)
        g = jnp.dot(hb, wg_ref[:, sl], preferred_element_type=F32)
        u = jnp.dot(hb, wu_ref[:, sl], preferred_element_type=F32)
        a = (g * jax.nn.sigmoid(g) * u).astype(BF16)
        part = jnp.dot(a, wd_ref[sl, :], preferred_element_type=F32)
        acc = part if acc is None else acc + part
    y = alpha * x1_ref[...] + g2_ref[...] * acc.reshape(nb, tt, d)
    o_ref[...] = _layer_norm(y, lng_ref[...], lnb_ref[...])


def _ffn_dense(x1, h2, mod, layer, wg, wu, wd, ln_g, ln_b, alpha):
    b, t, d = x1.shape
    nb, tt = _row_block(b, t)
    f = wg.shape[1]
    n_chunks = 2 if f % (2 * LANES) == 0 else 1
    row3 = pl.BlockSpec((nb, tt, d), lambda bi, ti: (bi, ti, 0))
    const2 = lambda a: _resident(a.shape, lambda bi, ti: (0, 0))
    vec = _resident((1, d), lambda bi, ti: (0, 0))
    return pl.pallas_call(
        functools.partial(_ffn_kernel, alpha=alpha, n_chunks=n_chunks),
        grid=(b // nb, t // tt),
        in_specs=[row3, row3, _mod_spec(nb, d, layer, 5), const2(wg), const2(wu), const2(wd), vec, vec],
        out_specs=row3,
        out_shape=jax.ShapeDtypeStruct((b, t, d), F32),
        compiler_params=_cparams(("arbitrary", "arbitrary")),
        name="ffn_dense",
    )(x1, h2, mod, wg, wu, wd, ln_g.reshape(1, d), ln_b.reshape(1, d))


def _moe_plan(cnt, tm, n_tiles_max):
    n_experts = cnt.shape[1]
    pc = (cnt + MOE_CHUNK - 1) // MOE_CHUNK * MOE_CHUNK
    seg_start = jnp.cumsum(pc, axis=1) - pc
    base = jnp.cumsum(pc, axis=0) - pc
    total = jnp.sum(pc, axis=0)
    tiles = (total + tm - 1) // tm
    ends = jnp.cumsum(tiles)
    starts = ends - tiles
    n_active = ends[-1]
    tile_ids = jnp.minimum(jnp.arange(n_tiles_max, dtype=I32), jnp.maximum(n_active - 1, 0))
    tile_expert = jnp.sum((tile_ids[:, None] >= ends[None, :]).astype(I32), axis=1)
    flat = lambda a: a.reshape(-1).astype(I32)
    return dict(
        seg_start=seg_start.astype(I32),
        seg=flat(seg_start), nch=flat(pc // MOE_CHUNK), gst=flat(starts[None, :] * tm + base),
        tot=flat(jnp.sum(pc // MOE_CHUNK, axis=1)), total=flat(total),
        last=flat((starts + jnp.maximum(tiles - 1, 0)) * tm),
        tile_expert=flat(jnp.minimum(tile_expert, n_experts - 1)), n_active=flat(n_active),
    )


def _dispatch_kernel(seg_ref, nch_ref, gst_ref, tot_ref, total_ref, last_ref, lpos_ref, h_ref, xs_ref, buf_ref,
                     zero_ref, sem, zsem, *, n_experts, tm, zrows):
    i = pl.program_id(0)
    n = pl.num_programs(0)
    slot = jnp.bitwise_and(i, 1)
    lrows = buf_ref.shape[1]
    rows = h_ref.shape[0]

    def chunk_copy(s, src_row, dst_row):
        return pltpu.make_async_copy(
            buf_ref.at[s, pl.ds(src_row, MOE_CHUNK)], xs_ref.at[pl.ds(dst_row, MOE_CHUNK)], sem.at[s])

    def drain(step):
        def body(c, carry):
            chunk_copy(jnp.bitwise_and(step, 1), 0, 0).wait()
            return carry

        lax.fori_loop(0, tot_ref[step], body, 0)

    @pl.when(i == 0)
    def _():
        zero_ref[...] = jnp.zeros(zero_ref.shape, BF16)

        def zero_copy(e, c):
            row = pl.multiple_of(last_ref[e] + c * zrows, zrows)
            return pltpu.make_async_copy(zero_ref, xs_ref.at[pl.ds(row, zrows)], zsem)

        for e in range(n_experts):
            @pl.when(total_ref[e] > 0)
            def _():
                for c in range(tm // zrows):
                    zero_copy(e, c).start()
        for e in range(n_experts):
            @pl.when(total_ref[e] > 0)
            def _():
                for c in range(tm // zrows):
                    zero_copy(e, c).wait()

    @pl.when(i >= 2)
    def _():
        drain(i - 2)

    lp = lpos_ref[0]
    row = lax.broadcasted_iota(I32, (lrows, rows), 0)
    sel = jnp.logical_or(row == lp[0:1, :], row == lp[1:2, :]).astype(BF16)
    buf_ref[slot] = jnp.dot(sel, h_ref[...], preferred_element_type=F32).astype(BF16)

    for e in range(n_experts):
        idx = i * n_experts + e
        src0 = seg_ref[idx]
        dst0 = gst_ref[idx]

        def body(c, carry, src0=src0, dst0=dst0):
            chunk_copy(slot, pl.multiple_of(src0 + c * MOE_CHUNK, MOE_CHUNK),
                       pl.multiple_of(dst0 + c * MOE_CHUNK, MOE_CHUNK)).start()
            return carry

        lax.fori_loop(0, nch_ref[idx], body, 0)

    @pl.when(i == n - 1)
    def _():
        @pl.when(i >= 1)
        def _():
            drain(i - 1)

        drain(i)


def _dispatch(h2, lpos_t, plan, rows, tm, n_rows_total):
    n, d = h2.shape
    n_tiles = n // rows
    n_experts = plan["total"].shape[0]
    zrows = min(tm, 256)
    lrows = _local_rows(rows, n_experts)
    grid_spec = pltpu.PrefetchScalarGridSpec(
        num_scalar_prefetch=6,
        grid=(n_tiles,),
        in_specs=[
            pl.BlockSpec((1, 2, rows), lambda i, *_: (i, 0, 0)),
            pl.BlockSpec((rows, d), lambda i, *_: (i, 0)),
        ],
        out_specs=pl.BlockSpec(memory_space=pl.ANY),
        scratch_shapes=[
            pltpu.VMEM((2, lrows, d), BF16), pltpu.VMEM((zrows, d), BF16),
            pltpu.SemaphoreType.DMA((2,)), pltpu.SemaphoreType.DMA(()),
        ],
    )
    return pl.pallas_call(
        functools.partial(_dispatch_kernel, n_experts=n_experts, tm=tm, zrows=zrows),
        grid_spec=grid_spec,
        out_shape=jax.ShapeDtypeStruct((n_rows_total, d), BF16),
        compiler_params=_cparams(("arbitrary",)),
        name="moe_dispatch",
    )(plan["seg"], plan["nch"], plan["gst"], plan["tot"], plan["total"], plan["last"], lpos_t, h2)


def _moe_ffn_kernel(te_ref, na_ref, xs_ref, wg_ref, wu_ref, wd_ref, y_ref, acc_ref):
    i = pl.program_id(0)
    j = pl.program_id(1)

    @pl.when(i < na_ref[0])
    def _():
        @pl.when(j == 0)
        def _():
            acc_ref[...] = jnp.zeros(acc_ref.shape, F32)

        xb = xs_ref[...]
        g = jnp.dot(xb, wg_ref[...], preferred_element_type=F32)
        u = jnp.dot(xb, wu_ref[...], preferred_element_type=F32)
        a = (g * jax.nn.sigmoid(g) * u).astype(BF16)
        acc_ref[...] += jnp.dot(a, wd_ref[...], preferred_element_type=F32)

        @pl.when(j == pl.num_programs(1) - 1)
        def _():
            y_ref[...] = acc_ref[...].astype(BF16)


def _moe_ffn(xs, tile_expert, n_active, wg, wu, wd, tm):
    r, d = xs.shape
    f = wg.shape[2]
    fc = MOE_FC
    assert f % fc == 0 and r % tm == 0
    nj = f // fc

    def row_map(i, j, te, na):
        return (jnp.minimum(i, jnp.maximum(na[0] - 1, 0)), 0)

    def jj(i, j, na):
        return jnp.where(i < na[0], j, nj - 1)

    grid_spec = pltpu.PrefetchScalarGridSpec(
        num_scalar_prefetch=2,
        grid=(r // tm, nj),
        in_specs=[
            pl.BlockSpec((tm, d), row_map),
            pl.BlockSpec((None, d, fc), lambda i, j, te, na: (te[i], 0, jj(i, j, na))),
            pl.BlockSpec((None, d, fc), lambda i, j, te, na: (te[i], 0, jj(i, j, na))),
            pl.BlockSpec((None, fc, d), lambda i, j, te, na: (te[i], jj(i, j, na), 0)),
        ],
        out_specs=pl.BlockSpec((tm, d), row_map),
        scratch_shapes=[pltpu.VMEM((tm, d), F32)],
    )
    return pl.pallas_call(
        _moe_ffn_kernel,
        grid_spec=grid_spec,
        out_shape=jax.ShapeDtypeStruct((r, d), BF16),
        compiler_params=_cparams(("arbitrary", "arbitrary")),
        name="moe_ffn",
    )(tile_expert, n_active, xs, wg, wu, wd)


def _combine_kernel(seg_ref, nch_ref, gst_ref, tot_ref, y_ref, x1_ref, g2_ref, lpos_ref, cw_ref, lng_ref, lnb_ref,
                    o_ref, buf_ref, sem, *, alpha, n_experts, n_tiles):
    nb, tt, d = x1_ref.shape
    rows = nb * tt
    lrows = buf_ref.shape[1]
    tile = pl.program_id(0) * pl.num_programs(1) + pl.program_id(1)
    slot = jnp.bitwise_and(tile, 1)

    def chunk_copy(s, src_row, dst_row):
        return pltpu.make_async_copy(
            y_ref.at[pl.ds(src_row, MOE_CHUNK)], buf_ref.at[s, pl.ds(dst_row, MOE_CHUNK)], sem.at[s])

    def fetch(step):
        s = jnp.bitwise_and(step, 1)
        buf_ref[s] = jnp.zeros((lrows, d), BF16)
        for e in range(n_experts):
            idx = step * n_experts + e
            src0 = gst_ref[idx]
            dst0 = seg_ref[idx]

            def body(c, carry, src0=src0, dst0=dst0):
                chunk_copy(s, pl.multiple_of(src0 + c * MOE_CHUNK, MOE_CHUNK),
                           pl.multiple_of(dst0 + c * MOE_CHUNK, MOE_CHUNK)).start()
                return carry

            lax.fori_loop(0, nch_ref[idx], body, 0)

    @pl.when(tile == 0)
    def _():
        fetch(tile)

    @pl.when(tile + 1 < n_tiles)
    def _():
        fetch(tile + 1)

    def wait(c, carry):
        chunk_copy(slot, 0, 0).wait()
        return carry

    lax.fori_loop(0, tot_ref[tile], wait, 0)
    lp = lpos_ref[...]
    lane = lax.broadcasted_iota(I32, (rows, lrows), 1)
    pick = jnp.concatenate([lane == lp[:, 0:1], lane == lp[:, 1:2]], axis=0).astype(BF16)
    g = jnp.dot(pick, buf_ref[slot], preferred_element_type=F32)
    cw = cw_ref[...]
    f = cw[:, 0:1] * g[:rows] + cw[:, 1:2] * g[rows:]
    y = alpha * x1_ref[...] + g2_ref[...] * f.reshape(nb, tt, d)
    o_ref[...] = _layer_norm(y, lng_ref[...], lnb_ref[...])


def _combine(y, x1, mod, layer, lpos, cw, plan, ln_g, ln_b, alpha):
    b, t, d = x1.shape
    nb, tt = _row_block(b, t)
    rows = nb * tt
    tpb = t // tt
    n_tiles = (b // nb) * tpb
    n_experts = plan["total"].shape[0]
    row3 = pl.BlockSpec((nb, tt, d), lambda bi, ti, *_: (bi, ti, 0))
    tile2 = pl.BlockSpec((rows, 2), lambda bi, ti, *_: (bi * tpb + ti, 0))
    vec = _resident((1, d), lambda bi, ti, *_: (0, 0))
    grid_spec = pltpu.PrefetchScalarGridSpec(
        num_scalar_prefetch=4,
        grid=(b // nb, tpb),
        in_specs=[
            pl.BlockSpec(memory_space=pl.ANY), row3,
            pl.BlockSpec((None, nb, 1, d), lambda bi, ti, *_: (layer, bi, 0, 5)), tile2, tile2, vec, vec,
        ],
        out_specs=row3,
        scratch_shapes=[pltpu.VMEM((2, _local_rows(rows, n_experts), d), BF16), pltpu.SemaphoreType.DMA((2,))],
    )
    return pl.pallas_call(
        functools.partial(_combine_kernel, alpha=alpha, n_experts=n_experts, n_tiles=n_tiles),
        grid_spec=grid_spec,
        out_shape=jax.ShapeDtypeStruct((b, t, d), F32),
        compiler_params=_cparams(("arbitrary", "arbitrary")),
        name="moe_combine",
    )(plan["seg"], plan["nch"], plan["gst"], plan["tot"], y, x1, mod, lpos, cw, ln_g.reshape(1, d),
      ln_b.reshape(1, d))


def _local_rows(rows, n_experts):
    need = 2 * rows + n_experts * (MOE_CHUNK - 1)
    return (need + LANES - 1) // LANES * LANES


def _moe(x1, h2, mod, layer, eidx, lrank, cw, cnt_tile, wg, wu, wd, ln_g, ln_b, alpha):
    b, t, d = x1.shape
    n = b * t
    nb, tt = _row_block(b, t)
    rows = nb * tt
    n_tiles = n // rows
    n_experts = wg.shape[0]
    tm = MOE_TM_LARGE if 2 * n >= 8 * n_experts * MOE_TM_LARGE else MOE_TM_SMALL
    n_tiles_max = -(-(2 * n + n_tiles * n_experts * (MOE_CHUNK - 1)) // tm) + n_experts
    plan = _moe_plan(cnt_tile[:, 0, :n_experts], tm, n_tiles_max)
    e3 = eidx.reshape(n_tiles, rows, 2)
    onehot = e3[..., None] == jnp.arange(n_experts, dtype=I32)
    seg_of_pair = jnp.sum(jnp.where(onehot, plan["seg_start"][:, None, None, :], 0), axis=-1)
    lpos = lrank.reshape(n_tiles, rows, 2) + seg_of_pair
    xs = _dispatch(h2.reshape(n, d), jnp.transpose(lpos, (0, 2, 1)), plan, rows, tm, n_tiles_max * tm)
    y = _moe_ffn(xs, plan["tile_expert"], plan["n_active"], wg, wu, wd, tm)
    return _combine(y, x1, mod, layer, lpos.reshape(n, 2), cw, plan, ln_g, ln_b, alpha)


def _trunk(x, mod, pos0, pool_hist, k_past, v_past, biases, W, depth):
    b, t, d = x.shape
    n_heads = d // HEAD_WIDTH
    qkw = n_heads * HEAD_WIDTH
    alpha = (2.0 * depth) ** 0.25
    k5 = v5 = None
    pool_states = []
    o4 = POOL_WIDTH + 3 * qkw
    for l in range(depth):
        lam_init = 0.8 - 0.6 * math.exp(-0.3 * l)
        w_a = W["w_in"][l, :, :o4].astype(BF16)
        w_gate = W["w_in"][l, :, o4:].astype(BF16)
        hist = None if pool_hist is None else pool_hist[l]
        q, kb, vb, k5, v5, po, pst = _inproj(
            x, mod, l, w_a, W["pool_w"][l].astype(BF16), W["pool_scale"][l], hist, k5, v5, depth, pos0)
        pool_states.append(pst)
        if k_past is None:
            ao = _attn_prompt(q, kb, vb, W["lam_qk"][l], W["subln_g"][l], biases[0], biases[1], lam_init)
        else:
            ao = _attn_sample(q, kb, vb, k_past, v_past, l, W["lam_qk"][l], W["subln_g"][l], biases[0], biases[1],
                              lam_init)
        moe = l % 2 == 1
        i = l // 2
        w_router_t = jnp.transpose(W["w_router"][i]) if moe else None
        outs = _post(x, mod, l, po, ao, w_gate, W["b_gate"][l], W["w_pool_up"][l].astype(BF16),
                     W["w_attn_up"][l].astype(BF16), W["w_o"][l].astype(BF16), W["ln_g"][l, 0], W["ln_b"][l, 0],
                     w_router_t, alpha)
        if not moe:
            x1, h2 = outs
            x = _ffn_dense(x1, h2, mod, l, W["w_ffn_gate"][i].astype(BF16), W["w_ffn_up"][i].astype(BF16),
                           W["w_ffn_down"][i].astype(BF16), W["ln_g"][l, 1], W["ln_b"][l, 1], alpha)
        else:
            x1, h2, eidx, rank, cw, cnt = outs
            x = _moe(x1, h2, mod, l, eidx, rank, cw, cnt, W["w_exp_gate"][i].astype(BF16),
                     W["w_exp_up"][i].astype(BF16), W["w_exp_down"][i].astype(BF16), W["ln_g"][l, 1], W["ln_b"][l, 1],
                     alpha)
    return x, k5, v5, jnp.stack(pool_states)


def kernel(x_prompt, x_sample, cache_k, cache_v, state_pool, c_prompt, c_sample, rel_bias, w_ada, b_ada, w_in, b_gate,
           pool_w, pool_scale, lam_qk, subln_g, w_pool_up, w_attn_up, w_o, ln_g, ln_b, w_ffn_gate, w_ffn_up,
           w_ffn_down, w_router, w_exp_gate, w_exp_up, w_exp_down):
    W = dict(w_in=w_in, b_gate=b_gate, pool_w=pool_w, pool_scale=pool_scale, lam_qk=lam_qk, subln_g=subln_g,
             w_pool_up=w_pool_up, w_attn_up=w_attn_up, w_o=w_o, ln_g=ln_g, ln_b=ln_b, w_ffn_gate=w_ffn_gate,
             w_ffn_up=w_ffn_up, w_ffn_down=w_ffn_down, w_router=w_router, w_exp_gate=w_exp_gate, w_exp_up=w_exp_up,
             w_exp_down=w_exp_down)
    depth, d, _ = w_in.shape
    bp, s, _ = x_prompt.shape
    bs, ts, _ = x_sample.shape
    p = cache_k.shape[2]

    mod = _ada(jnp.concatenate([c_prompt, c_sample], axis=0), w_ada, b_ada)
    mod_p = mod[:, :bp].reshape(depth, bp, 1, 6 * d)
    mod_s = mod[:, bp:].reshape(depth, bs, 1, 6 * d)

    tq = ATTN_TQ
    q_pos = np.arange(tq, 2 * tq)
    bias_prev = _bias_table(rel_bias, _bucket_table(q_pos, np.arange(0, tq)))
    bias_diag = _bias_table(rel_bias, _bucket_table(q_pos, q_pos))
    s_pos = p + np.arange(ts)
    bias_past = _bias_table(rel_bias, _bucket_table(s_pos, np.arange(p)))
    bias_new = _bias_table(rel_bias, _bucket_table(s_pos, s_pos))

    y_p, k_p, v_p, pool_p = _trunk(x_prompt, mod_p, 0, None, None, None, (bias_prev, bias_diag), W, depth)
    y_s, k_s, v_s, pool_s = _trunk(x_sample, mod_s, p, state_pool, cache_k, cache_v, (bias_past, bias_new), W, depth)
    return (y_p, y_s, k_p, v_p, pool_p, k_s, v_s, pool_s)
```

```python
import functools
import math

import numpy as np
import jax
import jax.numpy as jnp
from jax import lax
from jax.experimental import pallas as pl
from jax.experimental.pallas import tpu as pltpu

F32 = jnp.float32
BF16 = jnp.bfloat16
I32 = jnp.int32

CHUNK = 64
HEAD_DIM = 64
HEAD_WIDTH = 2 * HEAD_DIM
POOL_WINDOWS = (2, 4, 8, 16)
POOL_GROUP_DIM = 128
POOL_WIDTH = len(POOL_WINDOWS) * POOL_GROUP_DIM
POOL_HIST = max(POOL_WINDOWS) - 1
HIST_ROWS = 16
N_BUCKETS = 32
MAX_DISTANCE = 128
FAR_BUCKET = N_BUCKETS // 2 - 1
LN_EPS = 1e-5
RMS_EPS = 1e-5
MASKED = -1e30
LOG2E = math.log2(math.e)

LANES = 128
MXU_WIDTH = 256
ROW_TILE = 512
ATTN_TQ = 512
ATTN_TK = 512
ATTN_HEADS_PER_STEP = 4
MOE_CHUNK = 16
MOE_TM_LARGE = 1024
MOE_TM_SMALL = 256
MOE_FC = 512
VMEM_LIMIT = 56 * 1024 * 1024


def _cparams(sem):
    return pltpu.CompilerParams(dimension_semantics=sem, vmem_limit_bytes=VMEM_LIMIT)


def _resident(shape, index_map):
    return pl.BlockSpec(shape, index_map, pipeline_mode=pl.Buffered(1))


def _row_block(b, t):
    if t >= ROW_TILE:
        assert t % ROW_TILE == 0
        return 1, ROW_TILE
    nb = min(b, ROW_TILE // t)
    assert b % nb == 0 and t % 8 == 0
    return nb, t


def _rows2d(ref):
    nb, tt, w = ref.shape
    return ref[0] if nb == 1 else ref[...].reshape(nb * tt, w)


def _layer_norm(y, g, b):
    mu = jnp.mean(y, axis=-1, keepdims=True)
    yc = y - mu
    var = jnp.mean(yc * yc, axis=-1, keepdims=True)
    return yc * lax.rsqrt(var + LN_EPS) * g + b


def _ada_kernel(c_ref, w_ref, b_ref, o_ref):
    c = c_ref[...]
    s = c * jax.nn.sigmoid(c)
    o_ref[0] = jnp.dot(s.astype(BF16), w_ref[0].astype(BF16), preferred_element_type=F32) + b_ref[0]


def _ada(c_all, w_ada, b_ada):
    depth, d, n6 = w_ada.shape
    bc = c_all.shape[0]
    tn = 1536 if n6 % 1536 == 0 else n6
    return pl.pallas_call(
        _ada_kernel,
        grid=(depth, n6 // tn),
        in_specs=[
            pl.BlockSpec((bc, d), lambda l, j: (0, 0)),
            pl.BlockSpec((1, d, tn), lambda l, j: (l, 0, j)),
            pl.BlockSpec((1, 1, tn), lambda l, j: (l, 0, j)),
        ],
        out_specs=pl.BlockSpec((1, bc, tn), lambda l, j: (l, 0, j)),
        out_shape=jax.ShapeDtypeStruct((depth, bc, n6), F32),
        compiler_params=_cparams(("arbitrary", "arbitrary")),
        name="ada",
    )(c_all, w_ada, b_ada.reshape(depth, 1, n6))


def _mod_spec(nb, d, layer, chunk):
    return pl.BlockSpec((None, nb, 1, d), lambda bi, ti: (layer, bi, 0, chunk))


def _rel_bucket_np(rel):
    nb = N_BUCKETS // 2
    max_exact = nb // 2
    n = np.abs(rel)
    large = max_exact + (
        np.log(np.maximum(n, 1).astype(np.float32) / np.float32(max_exact))
        / np.float32(math.log(MAX_DISTANCE / max_exact))
        * np.float32(nb - max_exact)
    ).astype(np.int32)
    large = np.minimum(large, nb - 1)
    return np.where(rel > 0, nb, 0) + np.where(n < max_exact, n, large)


def _bucket_table(q_pos, k_pos):
    rel = k_pos[None, :] - q_pos[:, None]
    allowed = (k_pos[None, :] // CHUNK) <= (q_pos[:, None] // CHUNK)
    return np.where(allowed, _rel_bucket_np(rel), -1).astype(np.int32)


def _bias_kernel(rb_ref, bk_ref, o_ref):
    h = pl.program_id(0)
    bk = bk_ref[...]
    far = rb_ref[FAR_BUCKET, h]
    acc = jnp.zeros(bk.shape, F32)
    for b in range(N_BUCKETS):
        acc = jnp.where(bk == b, (rb_ref[b, h] - far) * LOG2E, acc)
    o_ref[0] = jnp.where(bk < 0, MASKED, acc)


def _bias_table(rel_bias, bucket_np):
    rows, cols = bucket_np.shape
    n_heads = rel_bias.shape[1]
    return pl.pallas_call(
        _bias_kernel,
        grid=(n_heads,),
        in_specs=[
            pl.BlockSpec(memory_space=pltpu.SMEM),
            pl.BlockSpec((rows, cols), lambda h: (0, 0)),
        ],
        out_specs=pl.BlockSpec((1, rows, cols), lambda h: (h, 0, 0)),
        out_shape=jax.ShapeDtypeStruct((n_heads, rows, cols), F32),
        compiler_params=_cparams(("arbitrary",)),
        name="bias_table",
    )(rel_bias, jnp.asarray(bucket_np))


def _inproj_kernel(*refs, pos0, has_hist, has_alias, n_heads):
    it = iter(refs)
    x_ref, sh_ref, sc_ref, w_ref, pw_ref, ps_ref = (next(it) for _ in range(6))
    hist_ref = next(it) if has_hist else None
    if has_alias:
        next(it)
        next(it)
    q_ref, kb_ref, vb_ref, k_ref, v_ref, po_ref, pst_ref, ext_ref = (next(it) for _ in range(8))

    ti = pl.program_id(1)
    nb, tt, d = x_ref.shape
    rows = nb * tt
    qkw = n_heads * HEAD_WIDTH
    o1 = POOL_WIDTH
    o2 = o1 + qkw
    o3 = o2 + qkw
    o4 = o3 + qkw

    @pl.when(ti == 0)
    def _():
        if has_hist:
            ext_ref[:, 0:1, :] = jnp.zeros((nb, 1, POOL_WIDTH), F32)
            ext_ref[:, 1:HIST_ROWS, :] = hist_ref[...]
        else:
            ext_ref[:, 0:HIST_ROWS, :] = jnp.zeros((nb, HIST_ROWS, POOL_WIDTH), F32)

    h = x_ref[...] * (1.0 + sc_ref[...]) + sh_ref[...]
    hb = h.reshape(rows, d).astype(BF16)

    u = jnp.dot(hb, w_ref[:, 0:o1], preferred_element_type=F32)
    q = jnp.dot(hb, w_ref[:, o1:o2], preferred_element_type=F32)
    q_ref[...] = (q * (HEAD_DIM ** -0.5 * LOG2E)).reshape(nb, tt, qkw).astype(BF16)
    k = jnp.dot(hb, w_ref[:, o2:o3], preferred_element_type=F32)
    kb_ref[...] = k.reshape(nb, tt, qkw).astype(BF16)
    v = jnp.dot(hb, w_ref[:, o3:o4], preferred_element_type=F32)
    vb_ref[...] = v.reshape(nb, tt, qkw).astype(BF16)
    k_ref[...] = k.reshape(nb, tt, n_heads, HEAD_WIDTH)
    v_ref[...] = v.reshape(nb, tt, n_heads, HEAD_WIDTH)

    ext_ref[:, HIST_ROWS:HIST_ROWS + tt, :] = u.reshape(nb, tt, POOL_WIDTH)
    pos = pos0 + ti * tt + lax.broadcasted_iota(I32, (1, tt, 1), 1)
    outs = []
    for gi, w in enumerate(POOL_WINDOWS):
        ls = slice(gi * POOL_GROUP_DIM, (gi + 1) * POOL_GROUP_DIM)
        tot = ext_ref[:, HIST_ROWS:HIST_ROWS + tt, ls]
        for j in range(1, w):
            tot = tot + ext_ref[:, HIST_ROWS - j:HIST_ROWS - j + tt, ls]
        cnt = jnp.minimum(pos + 1, w).astype(F32)
        dlt = (tot / cnt - ext_ref[:, HIST_ROWS:HIST_ROWS + tt, ls]).reshape(rows, POOL_GROUP_DIM)
        outs.append(jnp.dot(dlt.astype(BF16), pw_ref[gi], preferred_element_type=F32))
    po = jnp.concatenate(outs, axis=-1) * ps_ref[...]
    po_ref[...] = po.reshape(nb, tt, POOL_WIDTH).astype(BF16)
    pst_ref[...] = ext_ref[:, tt + 1:tt + HIST_ROWS, :]
    ext_ref[:, 0:HIST_ROWS, :] = ext_ref[:, tt:tt + HIST_ROWS, :]


def _inproj(x, mod, layer, w_a, pool_w, pool_scale, hist, k5, v5, depth, pos0):
    b, t, d = x.shape
    nb, tt = _row_block(b, t)
    n_heads = d // HEAD_WIDTH
    qkw = n_heads * HEAD_WIDTH
    has_hist = hist is not None
    has_alias = k5 is not None
    row3 = lambda w: pl.BlockSpec((nb, tt, w), lambda bi, ti: (bi, ti, 0))
    cache_spec = pl.BlockSpec((None, nb, tt, n_heads, HEAD_WIDTH), lambda bi, ti: (layer, bi, ti, 0, 0))
    in_specs = [
        row3(d),
        _mod_spec(nb, d, layer, 0),
        _mod_spec(nb, d, layer, 1),
        _resident(w_a.shape, lambda bi, ti: (0, 0)),
        _resident(pool_w.shape, lambda bi, ti: (0, 0, 0)),
        _resident((1, POOL_WIDTH), lambda bi, ti: (0, 0)),
    ]
    args = [x, mod, mod, w_a, pool_w, pool_scale.reshape(1, POOL_WIDTH)]
    if has_hist:
        in_specs.append(pl.BlockSpec((nb, POOL_HIST, POOL_WIDTH), lambda bi, ti: (bi, 0, 0)))
        args.append(hist)
    aliases = {}
    if has_alias:
        aliases = {len(args): 3, len(args) + 1: 4}
        in_specs += [pl.BlockSpec(memory_space=pl.ANY), pl.BlockSpec(memory_space=pl.ANY)]
        args += [k5, v5]
    cache_shape = jax.ShapeDtypeStruct((depth, b, t, n_heads, HEAD_WIDTH), F32)
    outs = pl.pallas_call(
        functools.partial(_inproj_kernel, pos0=pos0, has_hist=has_hist, has_alias=has_alias, n_heads=n_heads),
        grid=(b // nb, t // tt),
        in_specs=in_specs,
        out_specs=[
            row3(qkw), row3(qkw), row3(qkw), cache_spec, cache_spec, row3(POOL_WIDTH),
            pl.BlockSpec((nb, POOL_HIST, POOL_WIDTH), lambda bi, ti: (bi, 0, 0)),
        ],
        out_shape=[
            jax.ShapeDtypeStruct((b, t, qkw), BF16),
            jax.ShapeDtypeStruct((b, t, qkw), BF16),
            jax.ShapeDtypeStruct((b, t, qkw), BF16),
            cache_shape, cache_shape,
            jax.ShapeDtypeStruct((b, t, POOL_WIDTH), BF16),
            jax.ShapeDtypeStruct((b, POOL_HIST, POOL_WIDTH), F32),
        ],
        scratch_shapes=[pltpu.VMEM((nb, tt + HIST_ROWS, POOL_WIDTH), F32)],
        input_output_aliases=aliases,
        compiler_params=_cparams(("arbitrary", "arbitrary")),
        name="inproj",
    )(*args)
    return outs


def _lambda(lam_ref, lam_init):
    lq = lam_ref[...]
    s01 = jnp.sum(lq[0:1, :] * lq[1:2, :], axis=-1, keepdims=True)
    s23 = jnp.sum(lq[2:3, :] * lq[3:4, :], axis=-1, keepdims=True)
    return jnp.exp(s01) - jnp.exp(s23) + lam_init


def _stack_maps(q):
    lane = lax.broadcasted_iota(I32, q.shape, 1)
    zero = jnp.zeros_like(q)
    return jnp.concatenate([jnp.where(lane < HEAD_DIM, q, zero), jnp.where(lane >= HEAD_DIM, q, zero)], axis=0)


def _diff_out(acc, l, lam, g, lam_init, rows):
    o = acc / l
    o = o[:rows] - lam * o[rows:]
    o = o * lax.rsqrt(jnp.mean(o * o, axis=-1, keepdims=True) + RMS_EPS) * g * (1.0 - lam_init)
    return o.astype(BF16)


def _nt_dot(a, b):
    return lax.dot_general(a, b, (((1,), (1,)), ((), ())), preferred_element_type=F32)


def _attn_prompt_kernel(lam_ref, g_ref, q_ref, k_ref, v_ref, bprev_ref, bdiag_ref, o_ref, qs_ref, m_ref, acc_ref,
                        *, lam_init, hg):
    qi = pl.program_id(2)
    tq = q_ref.shape[1]
    tk = ATTN_TK
    for h in range(hg):
        qs_ref[h] = _stack_maps(q_ref[0, :, h * HEAD_WIDTH:(h + 1) * HEAD_WIDTH])
    hq = tq // 2
    lower = (slice(hq, tq), slice(tq + hq, 2 * tq))

    def update(start, width, bias, first=False, lower_half=False):
        ones = jnp.ones((width, LANES), BF16)
        for h in range(hg):
            sl = slice(h * HEAD_WIDTH, (h + 1) * HEAD_WIDTH)
            lhs = jnp.concatenate([qs_ref[h, r] for r in lower], axis=0) if lower_half else qs_ref[h]
            s = _nt_dot(lhs, k_ref[0, pl.ds(start, width), sl])
            if bias is not None:
                tile_bias = bias(h)
                s = s + jnp.concatenate([tile_bias, tile_bias], axis=0)
            m_new = jnp.broadcast_to(jnp.max(s, axis=-1, keepdims=True), (s.shape[0], LANES))
            if not first:
                m_prev = jnp.concatenate([m_ref[h, r] for r in lower], axis=0) if lower_half else m_ref[h]
                m_new = jnp.maximum(m_prev, m_new)
            pb = jnp.concatenate(
                [jnp.exp2(s[:, c * LANES:(c + 1) * LANES] - m_new).astype(BF16) for c in range(width // LANES)],
                axis=-1)
            v_ext = jnp.concatenate([v_ref[0, pl.ds(start, width), sl], ones], axis=-1)
            acc = jnp.dot(pb, v_ext, preferred_element_type=F32)
            if not first:
                alpha = jnp.exp2(m_prev - m_new)
                acc_prev = jnp.concatenate([acc_ref[h, r] for r in lower], axis=0) if lower_half else acc_ref[h]
                acc = jnp.concatenate([alpha, alpha], axis=-1) * acc_prev + acc
            if lower_half:
                for n, r in enumerate(lower):
                    m_ref[h, r] = m_new[n * hq:(n + 1) * hq]
                    acc_ref[h, r] = acc[n * hq:(n + 1) * hq]
            else:
                m_ref[h] = m_new
                acc_ref[h] = acc

    diag = pl.multiple_of(qi * tk, tk)
    update(diag, tk // 2, lambda h: bdiag_ref[h, :, 0:tk // 2], first=True)
    update(pl.multiple_of(diag + tk // 2, tk // 2), tk // 2, lambda h: bdiag_ref[h, hq:tq, tk // 2:tk],
           lower_half=True)

    @pl.when(qi >= 1)
    def _():
        prev = pl.multiple_of((qi - 1) * tk, tk)
        update(prev, tk // 2, lambda h: bprev_ref[h, :, 0:tk // 2])
        update(pl.multiple_of(prev + tk // 2, tk // 2), tk // 2, lambda h: bprev_ref[h, :, tk // 2:tk])

    def far_body(j, carry):
        far = pl.multiple_of(j * tk, tk)
        update(far, tk // 2, None)
        update(pl.multiple_of(far + tk // 2, tk // 2), tk // 2, None)
        return carry

    lax.fori_loop(0, jnp.maximum(qi - 1, 0), far_body, 0)
    lam = _lambda(lam_ref, lam_init)
    for h in range(hg):
        acc = acc_ref[h]
        o_ref[0, :, h * HEAD_WIDTH:(h + 1) * HEAD_WIDTH] = _diff_out(
            acc[:, :HEAD_WIDTH], acc[:, HEAD_WIDTH:], lam, g_ref[...], lam_init, tq)


def _attn_prompt(q, kb, vb, lam_qk, subln_g, bias_prev, bias_diag, lam_init):
    b, t, qkw = q.shape
    n_heads = qkw // HEAD_WIDTH
    tq = ATTN_TQ
    hg = math.gcd(n_heads, ATTN_HEADS_PER_STEP)
    gw = hg * HEAD_WIDTH
    assert tq == ATTN_TK and t % tq == 0 and (tq // 2) % CHUNK == 0
    return pl.pallas_call(
        functools.partial(_attn_prompt_kernel, lam_init=lam_init, hg=hg),
        grid=(n_heads // hg, b, t // tq),
        in_specs=[
            pl.BlockSpec(lam_qk.shape, lambda g, bi, qi: (0, 0)),
            pl.BlockSpec((1, HEAD_WIDTH), lambda g, bi, qi: (0, 0)),
            pl.BlockSpec((1, tq, gw), lambda g, bi, qi: (bi, qi, g)),
            pl.BlockSpec((1, t, gw), lambda g, bi, qi: (bi, 0, g)),
            pl.BlockSpec((1, t, gw), lambda g, bi, qi: (bi, 0, g)),
            pl.BlockSpec((hg, tq, ATTN_TK), lambda g, bi, qi: (g, 0, 0), pipeline_mode=pl.Buffered(1)),
            pl.BlockSpec((hg, tq, ATTN_TK), lambda g, bi, qi: (g, 0, 0), pipeline_mode=pl.Buffered(1)),
        ],
        out_specs=pl.BlockSpec((1, tq, gw), lambda g, bi, qi: (bi, qi, g)),
        out_shape=jax.ShapeDtypeStruct((b, t, qkw), BF16),
        scratch_shapes=[
            pltpu.VMEM((hg, 2 * tq, HEAD_WIDTH), BF16),
            pltpu.VMEM((hg, 2 * tq, LANES), F32),
            pltpu.VMEM((hg, 2 * tq, HEAD_WIDTH + LANES), F32),
        ],
        compiler_params=_cparams(("arbitrary", "arbitrary", "arbitrary")),
        name="attn_prompt",
    )(lam_qk, subln_g.reshape(1, HEAD_WIDTH), q, kb, vb, bias_prev, bias_diag)


def _attn_sample_kernel(lam_ref, g_ref, q_ref, kn_ref, vn_ref, kp_ref, vp_ref, bpast_ref, bnew_ref, o_ref,
                        *, lam_init, n_heads):
    ts = q_ref.shape[1]
    lam = _lambda(lam_ref, lam_init)
    past_len = kp_ref.shape[1]
    kp_all = kp_ref[0].reshape(past_len, n_heads * HEAD_WIDTH).astype(BF16)
    vp_all = vp_ref[0].reshape(past_len, n_heads * HEAD_WIDTH).astype(BF16)
    for hh in range(n_heads):
        sl = slice(hh * HEAD_WIDTH, (hh + 1) * HEAD_WIDTH)
        qs = _stack_maps(q_ref[0, :, sl])
        kp = kp_all[:, sl]
        vp = vp_all[:, sl]
        bp = bpast_ref[hh]
        bn = bnew_ref[hh]
        s_p = _nt_dot(qs, kp) + jnp.concatenate([bp, bp], axis=0)
        s_n = _nt_dot(qs, kn_ref[0, :, sl]) + jnp.concatenate([bn, bn], axis=0)
        m = jnp.maximum(jnp.max(s_p, axis=-1, keepdims=True), jnp.max(s_n, axis=-1, keepdims=True))
        p_p = jnp.exp2(s_p - m)
        p_n = jnp.exp2(s_n - m)
        l = jnp.sum(p_p, axis=-1, keepdims=True) + jnp.sum(p_n, axis=-1, keepdims=True)
        acc = jnp.dot(p_p.astype(BF16), vp, preferred_element_type=F32) + jnp.dot(
            p_n.astype(BF16), vn_ref[0, :, sl], preferred_element_type=F32)
        o_ref[0, :, sl] = _diff_out(acc, l, lam, g_ref[...], lam_init, ts)


def _attn_sample(q, kb, vb, k_past, v_past, layer, lam_qk, subln_g, bias_past, bias_new, lam_init):
    b, ts, qkw = q.shape
    n_heads = qkw // HEAD_WIDTH
    p = k_past.shape[2]
    row = pl.BlockSpec((1, ts, qkw), lambda bi: (bi, 0, 0))
    past = pl.BlockSpec((None, 1, p, n_heads, HEAD_WIDTH), lambda bi: (layer, bi, 0, 0, 0))
    return pl.pallas_call(
        functools.partial(_attn_sample_kernel, lam_init=lam_init, n_heads=n_heads),
        grid=(b,),
        in_specs=[
            pl.BlockSpec(lam_qk.shape, lambda bi: (0, 0)),
            pl.BlockSpec((1, HEAD_WIDTH), lambda bi: (0, 0)),
            row, row, row, past, past,
            pl.BlockSpec(bias_past.shape, lambda bi: (0, 0, 0)),
            pl.BlockSpec(bias_new.shape, lambda bi: (0, 0, 0)),
        ],
        out_specs=row,
        out_shape=jax.ShapeDtypeStruct((b, ts, qkw), BF16),
        compiler_params=_cparams(("arbitrary",)),
        name="attn_sample",
    )(lam_qk, subln_g.reshape(1, HEAD_WIDTH), q, kb, vb, k_past, v_past, bias_past, bias_new)


def _post_kernel(*refs, alpha, n_experts):
    moe = n_experts > 0
    it = iter(refs)
    (x_ref, sh1_ref, sc1_ref, g1_ref, sh2_ref, sc2_ref, po_ref, ao_ref, wgate_ref, bgate_ref, wpu_ref, wau_ref,
     wo_ref, lng_ref, lnb_ref) = (next(it) for _ in range(15))
    wr_ref = next(it) if moe else None
    x1_ref = next(it)
    h2_ref = next(it)
    if moe:
        eidx_ref, rank_ref, cw_ref, cnt_ref, hprev_ref = (next(it) for _ in range(5))

    nb, tt, d = x_ref.shape
    rows = nb * tt
    if moe:
        @pl.when(pl.program_id(0) == 0)
        def _():
            hprev_ref[...] = jnp.zeros(hprev_ref.shape, F32)

        h2r = hprev_ref[...]
    x = x_ref[...]
    hb = (x * (1.0 + sc1_ref[...]) + sh1_ref[...]).reshape(rows, d).astype(BF16)
    gates = jax.nn.sigmoid(jnp.dot(hb, wgate_ref[...], preferred_element_type=F32) + bgate_ref[...])
    pu = jnp.dot(_rows2d(po_ref), wpu_ref[...], preferred_element_type=F32)
    au = jnp.dot(_rows2d(ao_ref), wau_ref[...], preferred_element_type=F32)
    merged = gates[:, :d] * pu + gates[:, d:] * au
    mix = jnp.dot(merged.astype(BF16), wo_ref[...], preferred_element_type=F32)
    y = alpha * x + g1_ref[...] * mix.reshape(nb, tt, d)
    x1 = _layer_norm(y, lng_ref[...], lnb_ref[...])
    x1_ref[...] = x1
    h2 = x1 * (1.0 + sc2_ref[...]) + sh2_ref[...]
    h2_ref[...] = h2.astype(BF16)
    if not moe:
        return
    hprev_ref[...] = h2.reshape(rows, d)

    lane = lax.broadcasted_iota(I32, (rows, LANES), 1)
    lanef = lane.astype(F32)
    logits = jnp.full((rows, LANES), -jnp.inf, F32)
    for e in range(n_experts):
        col = jnp.sum(h2r * wr_ref[e:e + 1, :], axis=-1, keepdims=True)
        logits = jnp.where(lane == e, col, logits)
    m1 = jnp.max(logits, axis=-1, keepdims=True)
    i1 = jnp.min(jnp.where(logits == m1, lanef, float(LANES)), axis=-1, keepdims=True)
    rest = jnp.where(lanef == i1, -jnp.inf, logits)
    m2 = jnp.max(rest, axis=-1, keepdims=True)
    i2 = jnp.min(jnp.where(rest == m2, lanef, float(LANES)), axis=-1, keepdims=True)
    t2 = jnp.exp(m2 - m1)
    w1 = 1.0 / (1.0 + t2)
    w2 = t2 / (1.0 + t2)
    oh1 = lanef == i1
    oh2 = lanef == i2
    oh = jnp.logical_or(oh1, oh2).astype(F32)
    tri = (lax.broadcasted_iota(I32, (rows, rows), 0) >= lax.broadcasted_iota(I32, (rows, rows), 1)).astype(BF16)
    before = jnp.dot(tri, oh.astype(BF16), preferred_element_type=F32) - oh
    r1 = jnp.sum(jnp.where(oh1, before, 0.0), axis=-1, keepdims=True)
    r2 = jnp.sum(jnp.where(oh2, before, 0.0), axis=-1, keepdims=True)
    cnt_ref[0] = jnp.sum(oh, axis=0, keepdims=True).astype(I32)
    slot = lax.broadcasted_iota(I32, (rows, 2), 1)
    eidx_ref[...] = jnp.where(slot == 0, i1, i2).astype(I32)
    rank_ref[...] = jnp.where(slot == 0, r1, r2).astype(I32)
    cw_ref[...] = jnp.where(slot == 0, w1, w2)


def _post(x, mod, layer, po, ao, w_gate, b_gate, w_pu, w_au, w_o, ln_g, ln_b, w_router_t, alpha):
    b, t, d = x.shape
    nb, tt = _row_block(b, t)
    rows = nb * tt
    assert nb == 1 or tt == t
    moe = w_router_t is not None
    n_experts = w_router_t.shape[0] if moe else 0
    tpb = t // tt
    n_tiles = (b // nb) * tpb
    lag = 1 if moe else 0
    tile = lambda i: jnp.minimum(i, n_tiles - 1)
    row3 = lambda w: pl.BlockSpec((nb, tt, w), lambda i: (tile(i) // tpb, tile(i) % tpb, 0))
    modc = lambda c: pl.BlockSpec((None, nb, 1, d), lambda i: (layer, tile(i) // tpb, 0, c))
    const2 = lambda a: _resident(a.shape, lambda i: (0, 0))
    tile2 = pl.BlockSpec((rows, 2), lambda i: (jnp.maximum(i - lag, 0), 0))
    in_specs = [row3(d)] + [modc(c) for c in (0, 1, 2, 3, 4)] + [
        row3(po.shape[-1]), row3(ao.shape[-1]),
        const2(w_gate), _resident((1, 2 * d), lambda i: (0, 0)), const2(w_pu), const2(w_au), const2(w_o),
        _resident((1, d), lambda i: (0, 0)), _resident((1, d), lambda i: (0, 0)),
    ]
    args = [x, mod, mod, mod, mod, mod, po, ao, w_gate, b_gate.reshape(1, 2 * d), w_pu, w_au, w_o,
            ln_g.reshape(1, d), ln_b.reshape(1, d)]
    out_specs = [row3(d), row3(d)]
    out_shape = [jax.ShapeDtypeStruct((b, t, d), F32), jax.ShapeDtypeStruct((b, t, d), BF16)]
    scratch = []
    if moe:
        in_specs.append(const2(w_router_t))
        args.append(w_router_t)
        out_specs += [tile2, tile2, tile2, pl.BlockSpec((1, 1, LANES), lambda i: (jnp.maximum(i - lag, 0), 0, 0))]
        out_shape += [
            jax.ShapeDtypeStruct((n_tiles * rows, 2), I32),
            jax.ShapeDtypeStruct((n_tiles * rows, 2), I32),
            jax.ShapeDtypeStruct((n_tiles * rows, 2), F32),
            jax.ShapeDtypeStruct((n_tiles, 1, LANES), I32),
        ]
        scratch = [pltpu.VMEM((rows, d), F32)]
    return pl.pallas_call(
        functools.partial(_post_kernel, alpha=alpha, n_experts=n_experts),
        grid=(n_tiles + lag,),
        in_specs=in_specs,
        out_specs=out_specs,
        out_shape=out_shape,
        scratch_shapes=scratch,
        compiler_params=_cparams(("arbitrary",)),
        name="post_moe" if moe else "post",
    )(*args)


def _ffn_kernel(x1_ref, h2_ref, g2_ref, wg_ref, wu_ref, wd_ref, lng_ref, lnb_ref, o_ref, *, alpha, n_chunks):
    nb, tt, d = x1_ref.shape
    rows = nb * tt
    hb = _rows2d(h2_ref)
    f = wg_ref.shape[1]
    fc = f // n_chunks
    acc = None
    for c in range(n_chunks):
        sl = slice(c * fc, (c + 1) * fc)
        g = jnp.dot(hb, wg_ref[:, sl], preferred_element_type=F32)
        u = jnp.dot(hb, wu_ref[:, sl], preferred_element_type=F32)
        a = (g * jax.nn.sigmoid(g) * u).astype(BF16)
        part = jnp.dot(a, wd_ref[sl, :], preferred_element_type=F32)
        acc = part if acc is None else acc + part
    y = alpha * x1_ref[...] + g2_ref[...] * acc.reshape(nb, tt, d)
    o_ref[...] = _layer_norm(y, lng_ref[...], lnb_ref[...])


def _ffn_dense(x1, h2, mod, layer, wg, wu, wd, ln_g, ln_b, alpha):
    b, t, d = x1.shape
    nb, tt = _row_block(b, t)
    f = wg.shape[1]
    n_chunks = 2 if f % (2 * MXU_WIDTH) == 0 else 1
    row3 = pl.BlockSpec((nb, tt, d), lambda bi, ti: (bi, ti, 0))
    const2 = lambda a: _resident(a.shape, lambda bi, ti: (0, 0))
    vec = _resident((1, d), lambda bi, ti: (0, 0))
    return pl.pallas_call(
        functools.partial(_ffn_kernel, alpha=alpha, n_chunks=n_chunks),
        grid=(b // nb, t // tt),
        in_specs=[row3, row3, _mod_spec(nb, d, layer, 5), const2(wg), const2(wu), const2(wd), vec, vec],
        out_specs=row3,
        out_shape=jax.ShapeDtypeStruct((b, t, d), F32),
        compiler_params=_cparams(("arbitrary", "arbitrary")),
        name="ffn_dense",
    )(x1, h2, mod, wg, wu, wd, ln_g.reshape(1, d), ln_b.reshape(1, d))


def _moe_plan(cnt, tm, n_tiles_max):
    n_experts = cnt.shape[1]
    pc = (cnt + MOE_CHUNK - 1) // MOE_CHUNK * MOE_CHUNK
    seg_start = jnp.cumsum(pc, axis=1) - pc
    base = jnp.cumsum(pc, axis=0) - pc
    total = jnp.sum(pc, axis=0)
    tiles = (total + tm - 1) // tm
    ends = jnp.cumsum(tiles)
    starts = ends - tiles
    n_active = ends[-1]
    tile_ids = jnp.minimum(jnp.arange(n_tiles_max, dtype=I32), jnp.maximum(n_active - 1, 0))
    tile_expert = jnp.sum((tile_ids[:, None] >= ends[None, :]).astype(I32), axis=1)
    flat = lambda a: a.reshape(-1).astype(I32)
    return dict(
        seg_start=seg_start.astype(I32),
        seg=flat(seg_start), nch=flat(pc // MOE_CHUNK), gst=flat(starts[None, :] * tm + base),
        tot=flat(jnp.sum(pc // MOE_CHUNK, axis=1)), total=flat(total),
        last=flat((starts + jnp.maximum(tiles - 1, 0)) * tm),
        tile_expert=flat(jnp.minimum(tile_expert, n_experts - 1)), n_active=flat(n_active),
    )


def _dispatch_kernel(seg_ref, nch_ref, gst_ref, tot_ref, total_ref, last_ref, lpos_ref, h_ref, xs_ref, buf_ref,
                     zero_ref, sem, zsem, *, n_experts, tm, zrows):
    i = pl.program_id(0)
    n = pl.num_programs(0)
    slot = jnp.bitwise_and(i, 1)
    lrows = buf_ref.shape[1]
    rows = h_ref.shape[0]

    def chunk_copy(s, src_row, dst_row):
        return pltpu.make_async_copy(
            buf_ref.at[s, pl.ds(src_row, MOE_CHUNK)], xs_ref.at[pl.ds(dst_row, MOE_CHUNK)], sem.at[s])

    def drain(step):
        def body(c, carry):
            chunk_copy(jnp.bitwise_and(step, 1), 0, 0).wait()
            return carry

        lax.fori_loop(0, tot_ref[step], body, 0)

    @pl.when(i == 0)
    def _():
        zero_ref[...] = jnp.zeros(zero_ref.shape, BF16)

        def zero_copy(e, c):
            row = pl.multiple_of(last_ref[e] + c * zrows, zrows)
            return pltpu.make_async_copy(zero_ref, xs_ref.at[pl.ds(row, zrows)], zsem)

        for e in range(n_experts):
            @pl.when(total_ref[e] > 0)
            def _():
                for c in range(tm // zrows):
                    zero_copy(e, c).start()
        for e in range(n_experts):
            @pl.when(total_ref[e] > 0)
            def _():
                for c in range(tm // zrows):
                    zero_copy(e, c).wait()

    @pl.when(i >= 2)
    def _():
        drain(i - 2)

    lp = lpos_ref[0]
    row = lax.broadcasted_iota(I32, (lrows, rows), 0)
    sel = jnp.logical_or(row == lp[0:1, :], row == lp[1:2, :]).astype(BF16)
    buf_ref[slot] = jnp.dot(sel, h_ref[...], preferred_element_type=F32).astype(BF16)

    for e in range(n_experts):
        idx = i * n_experts + e
        src0 = seg_ref[idx]
        dst0 = gst_ref[idx]

        def body(c, carry, src0=src0, dst0=dst0):
            chunk_copy(slot, pl.multiple_of(src0 + c * MOE_CHUNK, MOE_CHUNK),
                       pl.multiple_of(dst0 + c * MOE_CHUNK, MOE_CHUNK)).start()
            return carry

        lax.fori_loop(0, nch_ref[idx], body, 0)

    @pl.when(i == n - 1)
    def _():
        @pl.when(i >= 1)
        def _():
            drain(i - 1)

        drain(i)


def _dispatch(h2, lpos_t, plan, rows, tm, n_rows_total):
    n, d = h2.shape
    n_tiles = n // rows
    n_experts = plan["total"].shape[0]
    zrows = min(tm, 256)
    lrows = _local_rows(rows, n_experts)
    grid_spec = pltpu.PrefetchScalarGridSpec(
        num_scalar_prefetch=6,
        grid=(n_tiles,),
        in_specs=[
            pl.BlockSpec((1, 2, rows), lambda i, *_: (i, 0, 0)),
            pl.BlockSpec((rows, d), lambda i, *_: (i, 0)),
        ],
        out_specs=pl.BlockSpec(memory_space=pl.ANY),
        scratch_shapes=[
            pltpu.VMEM((2, lrows, d), BF16), pltpu.VMEM((zrows, d), BF16),
            pltpu.SemaphoreType.DMA((2,)), pltpu.SemaphoreType.DMA(()),
        ],
    )
    return pl.pallas_call(
        functools.partial(_dispatch_kernel, n_experts=n_experts, tm=tm, zrows=zrows),
        grid_spec=grid_spec,
        out_shape=jax.ShapeDtypeStruct((n_rows_total, d), BF16),
        compiler_params=_cparams(("arbitrary",)),
        name="moe_dispatch",
    )(plan["seg"], plan["nch"], plan["gst"], plan["tot"], plan["total"], plan["last"], lpos_t, h2)


def _moe_ffn_kernel(te_ref, na_ref, xs_ref, wg_ref, wu_ref, wd_ref, y_ref, acc_ref):
    i = pl.program_id(0)
    j = pl.program_id(1)

    @pl.when(i < na_ref[0])
    def _():
        @pl.when(j == 0)
        def _():
            acc_ref[...] = jnp.zeros(acc_ref.shape, F32)

        xb = xs_ref[...]
        g = jnp.dot(xb, wg_ref[...], preferred_element_type=F32)
        u = jnp.dot(xb, wu_ref[...], preferred_element_type=F32)
        a = (g * jax.nn.sigmoid(g) * u).astype(BF16)
        acc_ref[...] += jnp.dot(a, wd_ref[...], preferred_element_type=F32)

        @pl.when(j == pl.num_programs(1) - 1)
        def _():
            y_ref[...] = acc_ref[...].astype(BF16)


def _moe_ffn(xs, tile_expert, n_active, wg, wu, wd, tm):
    r, d = xs.shape
    f = wg.shape[2]
    fc = MOE_FC
    assert f % fc == 0 and r % tm == 0
    nj = f // fc

    def row_map(i, j, te, na):
        return (jnp.minimum(i, jnp.maximum(na[0] - 1, 0)), 0)

    def jj(i, j, na):
        return jnp.where(i < na[0], j, nj - 1)

    grid_spec = pltpu.PrefetchScalarGridSpec(
        num_scalar_prefetch=2,
        grid=(r // tm, nj),
        in_specs=[
            pl.BlockSpec((tm, d), row_map),
            pl.BlockSpec((None, d, fc), lambda i, j, te, na: (te[i], 0, jj(i, j, na))),
            pl.BlockSpec((None, d, fc), lambda i, j, te, na: (te[i], 0, jj(i, j, na))),
            pl.BlockSpec((None, fc, d), lambda i, j, te, na: (te[i], jj(i, j, na), 0)),
        ],
        out_specs=pl.BlockSpec((tm, d), row_map),
        scratch_shapes=[pltpu.VMEM((tm, d), F32)],
    )
    return pl.pallas_call(
        _moe_ffn_kernel,
        grid_spec=grid_spec,
        out_shape=jax.ShapeDtypeStruct((r, d), BF16),
        compiler_params=_cparams(("arbitrary", "arbitrary")),
        name="moe_ffn",
    )(tile_expert, n_active, xs, wg, wu, wd)


def _combine_kernel(seg_ref, nch_ref, gst_ref, tot_ref, y_ref, x1_ref, g2_ref, lpos_ref, cw_ref, lng_ref, lnb_ref,
                    o_ref, buf_ref, sem, *, alpha, n_experts, n_tiles):
    nb, tt, d = x1_ref.shape
    rows = nb * tt
    lrows = buf_ref.shape[1]
    tile = pl.program_id(0) * pl.num_programs(1) + pl.program_id(1)
    slot = jnp.bitwise_and(tile, 1)

    def chunk_copy(s, src_row, dst_row):
        return pltpu.make_async_copy(
            y_ref.at[pl.ds(src_row, MOE_CHUNK)], buf_ref.at[s, pl.ds(dst_row, MOE_CHUNK)], sem.at[s])

    def fetch(step):
        s = jnp.bitwise_and(step, 1)
        buf_ref[s] = jnp.zeros((lrows, d), BF16)
        for e in range(n_experts):
            idx = step * n_experts + e
            src0 = gst_ref[idx]
            dst0 = seg_ref[idx]

            def body(c, carry, src0=src0, dst0=dst0):
                chunk_copy(s, pl.multiple_of(src0 + c * MOE_CHUNK, MOE_CHUNK),
                           pl.multiple_of(dst0 + c * MOE_CHUNK, MOE_CHUNK)).start()
                return carry

            lax.fori_loop(0, nch_ref[idx], body, 0)

    @pl.when(tile == 0)
    def _():
        fetch(tile)

    @pl.when(tile + 1 < n_tiles)
    def _():
        fetch(tile + 1)

    def wait(c, carry):
        chunk_copy(slot, 0, 0).wait()
        return carry

    lax.fori_loop(0, tot_ref[tile], wait, 0)
    lp = lpos_ref[...]
    lane = lax.broadcasted_iota(I32, (rows, lrows), 1)
    pick = jnp.concatenate([lane == lp[:, 0:1], lane == lp[:, 1:2]], axis=0).astype(BF16)
    g = jnp.dot(pick, buf_ref[slot], preferred_element_type=F32)
    cw = cw_ref[...]
    f = cw[:, 0:1] * g[:rows] + cw[:, 1:2] * g[rows:]
    y = alpha * x1_ref[...] + g2_ref[...] * f.reshape(nb, tt, d)
    o_ref[...] = _layer_norm(y, lng_ref[...], lnb_ref[...])


def _combine(y, x1, mod, layer, lpos, cw, plan, ln_g, ln_b, alpha):
    b, t, d = x1.shape
    nb, tt = _row_block(b, t)
    rows = nb * tt
    tpb = t // tt
    n_tiles = (b // nb) * tpb
    n_experts = plan["total"].shape[0]
    row3 = pl.BlockSpec((nb, tt, d), lambda bi, ti, *_: (bi, ti, 0))
    tile2 = pl.BlockSpec((rows, 2), lambda bi, ti, *_: (bi * tpb + ti, 0))
    vec = _resident((1, d), lambda bi, ti, *_: (0, 0))
    grid_spec = pltpu.PrefetchScalarGridSpec(
        num_scalar_prefetch=4,
        grid=(b // nb, tpb),
        in_specs=[
            pl.BlockSpec(memory_space=pl.ANY), row3,
            pl.BlockSpec((None, nb, 1, d), lambda bi, ti, *_: (layer, bi, 0, 5)), tile2, tile2, vec, vec,
        ],
        out_specs=row3,
        scratch_shapes=[pltpu.VMEM((2, _local_rows(rows, n_experts), d), BF16), pltpu.SemaphoreType.DMA((2,))],
    )
    return pl.pallas_call(
        functools.partial(_combine_kernel, alpha=alpha, n_experts=n_experts, n_tiles=n_tiles),
        grid_spec=grid_spec,
        out_shape=jax.ShapeDtypeStruct((b, t, d), F32),
        compiler_params=_cparams(("arbitrary", "arbitrary")),
        name="moe_combine",
    )(plan["seg"], plan["nch"], plan["gst"], plan["tot"], y, x1, mod, lpos, cw, ln_g.reshape(1, d),
      ln_b.reshape(1, d))


def _local_rows(rows, n_experts):
    need = 2 * rows + n_experts * (MOE_CHUNK - 1)
    return (need + LANES - 1) // LANES * LANES


def _moe(x1, h2, mod, layer, eidx, lrank, cw, cnt_tile, wg, wu, wd, ln_g, ln_b, alpha):
    b, t, d = x1.shape
    n = b * t
    nb, tt = _row_block(b, t)
    rows = nb * tt
    n_tiles = n // rows
    n_experts = wg.shape[0]
    tm = MOE_TM_LARGE if 2 * n >= 8 * n_experts * MOE_TM_LARGE else MOE_TM_SMALL
    n_tiles_max = -(-(2 * n + n_tiles * n_experts * (MOE_CHUNK - 1)) // tm) + n_experts
    plan = _moe_plan(cnt_tile[:, 0, :n_experts], tm, n_tiles_max)
    e3 = eidx.reshape(n_tiles, rows, 2)
    onehot = e3[..., None] == jnp.arange(n_experts, dtype=I32)
    seg_of_pair = jnp.sum(jnp.where(onehot, plan["seg_start"][:, None, None, :], 0), axis=-1)
    lpos = lrank.reshape(n_tiles, rows, 2) + seg_of_pair
    xs = _dispatch(h2.reshape(n, d), jnp.transpose(lpos, (0, 2, 1)), plan, rows, tm, n_tiles_max * tm)
    y = _moe_ffn(xs, plan["tile_expert"], plan["n_active"], wg, wu, wd, tm)
    return _combine(y, x1, mod, layer, lpos.reshape(n, 2), cw, plan, ln_g, ln_b, alpha)


def _trunk(x, mod, pos0, pool_hist, k_past, v_past, biases, W, depth):
    b, t, d = x.shape
    n_heads = d // HEAD_WIDTH
    qkw = n_heads * HEAD_WIDTH
    alpha = (2.0 * depth) ** 0.25
    k5 = v5 = None
    pool_states = []
    o4 = POOL_WIDTH + 3 * qkw
    for l in range(depth):
        lam_init = 0.8 - 0.6 * math.exp(-0.3 * l)
        w_a = W["w_in"][l, :, :o4].astype(BF16)
        w_gate = W["w_in"][l, :, o4:].astype(BF16)
        hist = None if pool_hist is None else pool_hist[l]
        q, kb, vb, k5, v5, po, pst = _inproj(
            x, mod, l, w_a, W["pool_w"][l].astype(BF16), W["pool_scale"][l], hist, k5, v5, depth, pos0)
        pool_states.append(pst)
        if k_past is None:
            ao = _attn_prompt(q, kb, vb, W["lam_qk"][l], W["subln_g"][l], biases[0], biases[1], lam_init)
        else:
            ao = _attn_sample(q, kb, vb, k_past, v_past, l, W["lam_qk"][l], W["subln_g"][l], biases[0], biases[1],
                              lam_init)
        moe = l % 2 == 1
        i = l // 2
        w_router_t = jnp.transpose(W["w_router"][i]) if moe else None
        outs = _post(x, mod, l, po, ao, w_gate, W["b_gate"][l], W["w_pool_up"][l].astype(BF16),
                     W["w_attn_up"][l].astype(BF16), W["w_o"][l].astype(BF16), W["ln_g"][l, 0], W["ln_b"][l, 0],
                     w_router_t, alpha)
        if not moe:
            x1, h2 = outs
            x = _ffn_dense(x1, h2, mod, l, W["w_ffn_gate"][i].astype(BF16), W["w_ffn_up"][i].astype(BF16),
                           W["w_ffn_down"][i].astype(BF16), W["ln_g"][l, 1], W["ln_b"][l, 1], alpha)
        else:
            x1, h2, eidx, rank, cw, cnt = outs
            x = _moe(x1, h2, mod, l, eidx, rank, cw, cnt, W["w_exp_gate"][i].astype(BF16),
                     W["w_exp_up"][i].astype(BF16), W["w_exp_down"][i].astype(BF16), W["ln_g"][l, 1], W["ln_b"][l, 1],
                     alpha)
    return x, k5, v5, jnp.stack(pool_states)


def kernel(x_prompt, x_sample, cache_k, cache_v, state_pool, c_prompt, c_sample, rel_bias, w_ada, b_ada, w_in, b_gate,
           pool_w, pool_scale, lam_qk, subln_g, w_pool_up, w_attn_up, w_o, ln_g, ln_b, w_ffn_gate, w_ffn_up,
           w_ffn_down, w_router, w_exp_gate, w_exp_up, w_exp_down):
    W = dict(w_in=w_in, b_gate=b_gate, pool_w=pool_w, pool_scale=pool_scale, lam_qk=lam_qk, subln_g=subln_g,
             w_pool_up=w_pool_up, w_attn_up=w_attn_up, w_o=w_o, ln_g=ln_g, ln_b=ln_b, w_ffn_gate=w_ffn_gate,
             w_ffn_up=w_ffn_up, w_ffn_down=w_ffn_down, w_router=w_router, w_exp_gate=w_exp_gate, w_exp_up=w_exp_up,
             w_exp_down=w_exp_down)
    depth, d, _ = w_in.shape
    bp, s, _ = x_prompt.shape
    bs, ts, _ = x_sample.shape
    p = cache_k.shape[2]

    mod = _ada(jnp.concatenate([c_prompt, c_sample], axis=0), w_ada, b_ada)
    mod_p = mod[:, :bp].reshape(depth, bp, 1, 6 * d)
    mod_s = mod[:, bp:].reshape(depth, bs, 1, 6 * d)

    tq = ATTN_TQ
    q_pos = np.arange(tq, 2 * tq)
    bias_prev = _bias_table(rel_bias, _bucket_table(q_pos, np.arange(0, tq)))
    bias_diag = _bias_table(rel_bias, _bucket_table(q_pos, q_pos))
    s_pos = p + np.arange(ts)
    bias_past = _bias_table(rel_bias, _bucket_table(s_pos, np.arange(p)))
    bias_new = _bias_table(rel_bias, _bucket_table(s_pos, s_pos))

    y_p, k_p, v_p, pool_p = _trunk(x_prompt, mod_p, 0, None, None, None, (bias_prev, bias_diag), W, depth)
    y_s, k_s, v_s, pool_s = _trunk(x_sample, mod_s, p, state_pool, cache_k, cache_v, (bias_past, bias_new), W, depth)
    return (y_p, y_s, k_p, v_p, pool_p, k_s, v_s, pool_s)
```

```python
import functools
import math

import numpy as np
import jax
import jax.numpy as jnp
from jax import lax
from jax.experimental import pallas as pl
from jax.experimental.pallas import tpu as pltpu

F32 = jnp.float32
BF16 = jnp.bfloat16
I32 = jnp.int32

CHUNK = 64
HEAD_DIM = 64
HEAD_WIDTH = 2 * HEAD_DIM
POOL_WINDOWS = (2, 4, 8, 16)
POOL_GROUP_DIM = 128
POOL_WIDTH = len(POOL_WINDOWS) * POOL_GROUP_DIM
POOL_HIST = max(POOL_WINDOWS) - 1
HIST_ROWS = 16
N_BUCKETS = 32
MAX_DISTANCE = 128
FAR_BUCKET = N_BUCKETS // 2 - 1
LN_EPS = 1e-5
RMS_EPS = 1e-5
MASKED = -1e30
LOG2E = math.log2(math.e)

LANES = 128
MXU_WIDTH = 256
ROW_TILE = 512
ATTN_TQ = 512
ATTN_TK = 512
ATTN_HEADS_PER_STEP = 8
MOE_CHUNK = 16
MOE_TM_LARGE = 1024
MOE_TM_SMALL = 256
MOE_FC = 512
VMEM_LIMIT = 56 * 1024 * 1024


def _cparams(sem):
    return pltpu.CompilerParams(dimension_semantics=sem, vmem_limit_bytes=VMEM_LIMIT)


def _resident(shape, index_map):
    return pl.BlockSpec(shape, index_map, pipeline_mode=pl.Buffered(1))


def _row_block(b, t):
    if t >= ROW_TILE:
        assert t % ROW_TILE == 0
        return 1, ROW_TILE
    nb = min(b, ROW_TILE // t)
    assert b % nb == 0 and t % 8 == 0
    return nb, t


def _rows2d(ref):
    nb, tt, w = ref.shape
    return ref[0] if nb == 1 else ref[...].reshape(nb * tt, w)


def _layer_norm(y, g, b):
    mu = jnp.mean(y, axis=-1, keepdims=True)
    yc = y - mu
    var = jnp.mean(yc * yc, axis=-1, keepdims=True)
    return yc * lax.rsqrt(var + LN_EPS) * g + b


def _ada_kernel(c_ref, w_ref, b_ref, o_ref):
    c = c_ref[...]
    s = c * jax.nn.sigmoid(c)
    o_ref[0] = jnp.dot(s.astype(BF16), w_ref[0].astype(BF16), preferred_element_type=F32) + b_ref[0]


def _ada(c_all, w_ada, b_ada):
    depth, d, n6 = w_ada.shape
    bc = c_all.shape[0]
    tn = 1536 if n6 % 1536 == 0 else n6
    return pl.pallas_call(
        _ada_kernel,
        grid=(depth, n6 // tn),
        in_specs=[
            pl.BlockSpec((bc, d), lambda l, j: (0, 0)),
            pl.BlockSpec((1, d, tn), lambda l, j: (l, 0, j)),
            pl.BlockSpec((1, 1, tn), lambda l, j: (l, 0, j)),
        ],
        out_specs=pl.BlockSpec((1, bc, tn), lambda l, j: (l, 0, j)),
        out_shape=jax.ShapeDtypeStruct((depth, bc, n6), F32),
        compiler_params=_cparams(("arbitrary", "arbitrary")),
        name="ada",
    )(c_all, w_ada, b_ada.reshape(depth, 1, n6))


def _mod_spec(nb, d, layer, chunk):
    return pl.BlockSpec((None, nb, 1, d), lambda bi, ti: (layer, bi, 0, chunk))


def _rel_bucket_np(rel):
    nb = N_BUCKETS // 2
    max_exact = nb // 2
    n = np.abs(rel)
    large = max_exact + (
        np.log(np.maximum(n, 1).astype(np.float32) / np.float32(max_exact))
        / np.float32(math.log(MAX_DISTANCE / max_exact))
        * np.float32(nb - max_exact)
    ).astype(np.int32)
    large = np.minimum(large, nb - 1)
    return np.where(rel > 0, nb, 0) + np.where(n < max_exact, n, large)


def _bucket_table(q_pos, k_pos):
    rel = k_pos[None, :] - q_pos[:, None]
    allowed = (k_pos[None, :] // CHUNK) <= (q_pos[:, None] // CHUNK)
    return np.where(allowed, _rel_bucket_np(rel), -1).astype(np.int32)


def _bias_kernel(rb_ref, bk_ref, o_ref):
    h = pl.program_id(0)
    bk = bk_ref[...]
    far = rb_ref[FAR_BUCKET, h]
    acc = jnp.zeros(bk.shape, F32)
    for b in range(N_BUCKETS):
        acc = jnp.where(bk == b, (rb_ref[b, h] - far) * LOG2E, acc)
    o_ref[0] = jnp.where(bk < 0, MASKED, acc)


def _bias_table(rel_bias, bucket_np):
    rows, cols = bucket_np.shape
    n_heads = rel_bias.shape[1]
    return pl.pallas_call(
        _bias_kernel,
        grid=(n_heads,),
        in_specs=[
            pl.BlockSpec(memory_space=pltpu.SMEM),
            pl.BlockSpec((rows, cols), lambda h: (0, 0)),
        ],
        out_specs=pl.BlockSpec((1, rows, cols), lambda h: (h, 0, 0)),
        out_shape=jax.ShapeDtypeStruct((n_heads, rows, cols), F32),
        compiler_params=_cparams(("arbitrary",)),
        name="bias_table",
    )(rel_bias, jnp.asarray(bucket_np))


def _inproj_kernel(*refs, pos0, has_hist, has_alias, n_heads):
    it = iter(refs)
    x_ref, sh_ref, sc_ref, w_ref, pw_ref, ps_ref = (next(it) for _ in range(6))
    hist_ref = next(it) if has_hist else None
    if has_alias:
        next(it)
        next(it)
    q_ref, kb_ref, vb_ref, k_ref, v_ref, po_ref, pst_ref, ext_ref = (next(it) for _ in range(8))

    ti = pl.program_id(1)
    nb, tt, d = x_ref.shape
    rows = nb * tt
    qkw = n_heads * HEAD_WIDTH
    o1 = POOL_WIDTH
    o2 = o1 + qkw
    o3 = o2 + qkw
    o4 = o3 + qkw

    @pl.when(ti == 0)
    def _():
        if has_hist:
            ext_ref[:, 0:1, :] = jnp.zeros((nb, 1, POOL_WIDTH), F32)
            ext_ref[:, 1:HIST_ROWS, :] = hist_ref[...]
        else:
            ext_ref[:, 0:HIST_ROWS, :] = jnp.zeros((nb, HIST_ROWS, POOL_WIDTH), F32)

    h = x_ref[...] * (1.0 + sc_ref[...]) + sh_ref[...]
    hb = h.reshape(rows, d).astype(BF16)

    u = jnp.dot(hb, w_ref[:, 0:o1], preferred_element_type=F32)
    q = jnp.dot(hb, w_ref[:, o1:o2], preferred_element_type=F32)
    q_ref[...] = (q * (HEAD_DIM ** -0.5 * LOG2E)).reshape(nb, tt, qkw).astype(BF16)
    k = jnp.dot(hb, w_ref[:, o2:o3], preferred_element_type=F32)
    kb_ref[...] = k.reshape(nb, tt, qkw).astype(BF16)
    v = jnp.dot(hb, w_ref[:, o3:o4], preferred_element_type=F32)
    vb_ref[...] = v.reshape(nb, tt, qkw).astype(BF16)
    k_ref[...] = k.reshape(nb, tt, n_heads, HEAD_WIDTH)
    v_ref[...] = v.reshape(nb, tt, n_heads, HEAD_WIDTH)

    ext_ref[:, HIST_ROWS:HIST_ROWS + tt, :] = u.reshape(nb, tt, POOL_WIDTH)
    pos = pos0 + ti * tt + lax.broadcasted_iota(I32, (1, tt, 1), 1)
    outs = []
    for gi, w in enumerate(POOL_WINDOWS):
        ls = slice(gi * POOL_GROUP_DIM, (gi + 1) * POOL_GROUP_DIM)
        tot = ext_ref[:, HIST_ROWS:HIST_ROWS + tt, ls]
        for j in range(1, w):
            tot = tot + ext_ref[:, HIST_ROWS - j:HIST_ROWS - j + tt, ls]
        cnt = jnp.minimum(pos + 1, w).astype(F32)
        dlt = (tot / cnt - ext_ref[:, HIST_ROWS:HIST_ROWS + tt, ls]).reshape(rows, POOL_GROUP_DIM)
        outs.append(jnp.dot(dlt.astype(BF16), pw_ref[gi], preferred_element_type=F32))
    po = jnp.concatenate(outs, axis=-1) * ps_ref[...]
    po_ref[...] = po.reshape(nb, tt, POOL_WIDTH).astype(BF16)
    pst_ref[...] = ext_ref[:, tt + 1:tt + HIST_ROWS, :]
    ext_ref[:, 0:HIST_ROWS, :] = ext_ref[:, tt:tt + HIST_ROWS, :]


def _inproj(x, mod, layer, w_a, pool_w, pool_scale, hist, k5, v5, depth, pos0):
    b, t, d = x.shape
    nb, tt = _row_block(b, t)
    n_heads = d // HEAD_WIDTH
    qkw = n_heads * HEAD_WIDTH
    has_hist = hist is not None
    has_alias = k5 is not None
    row3 = lambda w: pl.BlockSpec((nb, tt, w), lambda bi, ti: (bi, ti, 0))
    cache_spec = pl.BlockSpec((None, nb, tt, n_heads, HEAD_WIDTH), lambda bi, ti: (layer, bi, ti, 0, 0))
    in_specs = [
        row3(d),
        _mod_spec(nb, d, layer, 0),
        _mod_spec(nb, d, layer, 1),
        _resident(w_a.shape, lambda bi, ti: (0, 0)),
        _resident(pool_w.shape, lambda bi, ti: (0, 0, 0)),
        _resident((1, POOL_WIDTH), lambda bi, ti: (0, 0)),
    ]
    args = [x, mod, mod, w_a, pool_w, pool_scale.reshape(1, POOL_WIDTH)]
    if has_hist:
        in_specs.append(pl.BlockSpec((nb, POOL_HIST, POOL_WIDTH), lambda bi, ti: (bi, 0, 0)))
        args.append(hist)
    aliases = {}
    if has_alias:
        aliases = {len(args): 3, len(args) + 1: 4}
        in_specs += [pl.BlockSpec(memory_space=pl.ANY), pl.BlockSpec(memory_space=pl.ANY)]
        args += [k5, v5]
    cache_shape = jax.ShapeDtypeStruct((depth, b, t, n_heads, HEAD_WIDTH), F32)
    outs = pl.pallas_call(
        functools.partial(_inproj_kernel, pos0=pos0, has_hist=has_hist, has_alias=has_alias, n_heads=n_heads),
        grid=(b // nb, t // tt),
        in_specs=in_specs,
        out_specs=[
            row3(qkw), row3(qkw), row3(qkw), cache_spec, cache_spec, row3(POOL_WIDTH),
            pl.BlockSpec((nb, POOL_HIST, POOL_WIDTH), lambda bi, ti: (bi, 0, 0)),
        ],
        out_shape=[
            jax.ShapeDtypeStruct((b, t, qkw), BF16),
            jax.ShapeDtypeStruct((b, t, qkw), BF16),
            jax.ShapeDtypeStruct((b, t, qkw), BF16),
            cache_shape, cache_shape,
            jax.ShapeDtypeStruct((b, t, POOL_WIDTH), BF16),
            jax.ShapeDtypeStruct((b, POOL_HIST, POOL_WIDTH), F32),
        ],
        scratch_shapes=[pltpu.VMEM((nb, tt + HIST_ROWS, POOL_WIDTH), F32)],
        input_output_aliases=aliases,
        compiler_params=_cparams(("arbitrary", "arbitrary")),
        name="inproj",
    )(*args)
    return outs


def _lambda(lam_ref, lam_init):
    lq = lam_ref[...]
    s01 = jnp.sum(lq[0:1, :] * lq[1:2, :], axis=-1, keepdims=True)
    s23 = jnp.sum(lq[2:3, :] * lq[3:4, :], axis=-1, keepdims=True)
    return jnp.exp(s01) - jnp.exp(s23) + lam_init


def _stack_maps(q):
    lane = lax.broadcasted_iota(I32, q.shape, 1)
    zero = jnp.zeros_like(q)
    return jnp.concatenate([jnp.where(lane < HEAD_DIM, q, zero), jnp.where(lane >= HEAD_DIM, q, zero)], axis=0)


def _diff_out(acc, l, lam, g, lam_init, rows):
    o = acc / l
    o = o[:rows] - lam * o[rows:]
    o = o * lax.rsqrt(jnp.mean(o * o, axis=-1, keepdims=True) + RMS_EPS) * g * (1.0 - lam_init)
    return o.astype(BF16)


def _nt_dot(a, b):
    return lax.dot_general(a, b, (((1,), (1,)), ((), ())), preferred_element_type=F32)


def _attn_prompt_kernel(lam_ref, g_ref, q_ref, k_ref, v_ref, bprev_ref, bdiag_ref, o_ref, qs_ref, m_ref, acc_ref,
                        *, lam_init, hg):
    qi = pl.program_id(2)
    tq = q_ref.shape[1]
    tk = ATTN_TK
    for h in range(hg):
        qs_ref[h] = _stack_maps(q_ref[0, :, h * HEAD_WIDTH:(h + 1) * HEAD_WIDTH])
    hq = tq // 2
    lower = (slice(hq, tq), slice(tq + hq, 2 * tq))

    def update(start, width, bias, first=False, lower_half=False):
        ones = jnp.ones((width, LANES), BF16)
        for h in range(hg):
            sl = slice(h * HEAD_WIDTH, (h + 1) * HEAD_WIDTH)
            lhs = jnp.concatenate([qs_ref[h, r] for r in lower], axis=0) if lower_half else qs_ref[h]
            s = _nt_dot(lhs, k_ref[0, pl.ds(start, width), sl])
            if bias is not None:
                tile_bias = bias(h)
                s = s + jnp.concatenate([tile_bias, tile_bias], axis=0)
            m_new = jnp.broadcast_to(jnp.max(s, axis=-1, keepdims=True), (s.shape[0], LANES))
            if not first:
                m_prev = jnp.concatenate([m_ref[h, r] for r in lower], axis=0) if lower_half else m_ref[h]
                m_new = jnp.maximum(m_prev, m_new)
            pb = jnp.concatenate(
                [jnp.exp2(s[:, c * LANES:(c + 1) * LANES] - m_new).astype(BF16) for c in range(width // LANES)],
                axis=-1)
            v_ext = jnp.concatenate([v_ref[0, pl.ds(start, width), sl], ones], axis=-1)
            acc = jnp.dot(pb, v_ext, preferred_element_type=F32)
            if not first:
                alpha = jnp.exp2(m_prev - m_new)
                acc_prev = jnp.concatenate([acc_ref[h, r] for r in lower], axis=0) if lower_half else acc_ref[h]
                acc = jnp.concatenate([alpha, alpha], axis=-1) * acc_prev + acc
            if lower_half:
                for n, r in enumerate(lower):
                    m_ref[h, r] = m_new[n * hq:(n + 1) * hq]
                    acc_ref[h, r] = acc[n * hq:(n + 1) * hq]
            else:
                m_ref[h] = m_new
                acc_ref[h] = acc

    diag = pl.multiple_of(qi * tk, tk)
    update(diag, tk // 2, lambda h: bdiag_ref[h, :, 0:tk // 2], first=True)
    update(pl.multiple_of(diag + tk // 2, tk // 2), tk // 2, lambda h: bdiag_ref[h, hq:tq, tk // 2:tk],
           lower_half=True)

    @pl.when(qi >= 1)
    def _():
        prev = pl.multiple_of((qi - 1) * tk, tk)
        update(prev, tk // 2, lambda h: bprev_ref[h, :, 0:tk // 2])
        update(pl.multiple_of(prev + tk // 2, tk // 2), tk // 2, lambda h: bprev_ref[h, :, tk // 2:tk])

    def far_body(j, carry):
        far = pl.multiple_of(j * tk, tk)
        update(far, tk // 2, None)
        update(pl.multiple_of(far + tk // 2, tk // 2), tk // 2, None)
        return carry

    lax.fori_loop(0, jnp.maximum(qi - 1, 0), far_body, 0)
    lam = _lambda(lam_ref, lam_init)
    for h in range(hg):
        acc = acc_ref[h]
        o_ref[0, :, h * HEAD_WIDTH:(h + 1) * HEAD_WIDTH] = _diff_out(
            acc[:, :HEAD_WIDTH], acc[:, HEAD_WIDTH:], lam, g_ref[...], lam_init, tq)


def _attn_prompt(q, kb, vb, lam_qk, subln_g, bias_prev, bias_diag, lam_init):
    b, t, qkw = q.shape
    n_heads = qkw // HEAD_WIDTH
    tq = ATTN_TQ
    hg = math.gcd(n_heads, ATTN_HEADS_PER_STEP)
    gw = hg * HEAD_WIDTH
    assert tq == ATTN_TK and t % tq == 0 and (tq // 2) % CHUNK == 0
    return pl.pallas_call(
        functools.partial(_attn_prompt_kernel, lam_init=lam_init, hg=hg),
        grid=(n_heads // hg, b, t // tq),
        in_specs=[
            pl.BlockSpec(lam_qk.shape, lambda g, bi, qi: (0, 0)),
            pl.BlockSpec((1, HEAD_WIDTH), lambda g, bi, qi: (0, 0)),
            pl.BlockSpec((1, tq, gw), lambda g, bi, qi: (bi, qi, g)),
            pl.BlockSpec((1, t, gw), lambda g, bi, qi: (bi, 0, g)),
            pl.BlockSpec((1, t, gw), lambda g, bi, qi: (bi, 0, g)),
            pl.BlockSpec((hg, tq, ATTN_TK), lambda g, bi, qi: (g, 0, 0), pipeline_mode=pl.Buffered(1)),
            pl.BlockSpec((hg, tq, ATTN_TK), lambda g, bi, qi: (g, 0, 0), pipeline_mode=pl.Buffered(1)),
        ],
        out_specs=pl.BlockSpec((1, tq, gw), lambda g, bi, qi: (bi, qi, g)),
        out_shape=jax.ShapeDtypeStruct((b, t, qkw), BF16),
        scratch_shapes=[
            pltpu.VMEM((hg, 2 * tq, HEAD_WIDTH), BF16),
            pltpu.VMEM((hg, 2 * tq, LANES), F32),
            pltpu.VMEM((hg, 2 * tq, HEAD_WIDTH + LANES), F32),
        ],
        compiler_params=_cparams(("arbitrary", "arbitrary", "arbitrary")),
        name="attn_prompt",
    )(lam_qk, subln_g.reshape(1, HEAD_WIDTH), q, kb, vb, bias_prev, bias_diag)


def _attn_sample_kernel(lam_ref, g_ref, q_ref, kn_ref, vn_ref, kp_ref, vp_ref, bpast_ref, bnew_ref, o_ref,
                        *, lam_init, n_heads):
    ts = q_ref.shape[1]
    lam = _lambda(lam_ref, lam_init)
    past_len = kp_ref.shape[1]
    kp_all = kp_ref[0].reshape(past_len, n_heads * HEAD_WIDTH).astype(BF16)
    vp_all = vp_ref[0].reshape(past_len, n_heads * HEAD_WIDTH).astype(BF16)
    for hh in range(n_heads):
        sl = slice(hh * HEAD_WIDTH, (hh + 1) * HEAD_WIDTH)
        qs = _stack_maps(q_ref[0, :, sl])
        kp = kp_all[:, sl]
        vp = vp_all[:, sl]
        bp = bpast_ref[hh]
        bn = bnew_ref[hh]
        s_p = _nt_dot(qs, kp) + jnp.concatenate([bp, bp], axis=0)
        s_n = _nt_dot(qs, kn_ref[0, :, sl]) + jnp.concatenate([bn, bn], axis=0)
        m = jnp.maximum(jnp.max(s_p, axis=-1, keepdims=True), jnp.max(s_n, axis=-1, keepdims=True))
        p_p = jnp.exp2(s_p - m)
        p_n = jnp.exp2(s_n - m)
        l = jnp.sum(p_p, axis=-1, keepdims=True) + jnp.sum(p_n, axis=-1, keepdims=True)
        acc = jnp.dot(p_p.astype(BF16), vp, preferred_element_type=F32) + jnp.dot(
            p_n.astype(BF16), vn_ref[0, :, sl], preferred_element_type=F32)
        o_ref[0, :, sl] = _diff_out(acc, l, lam, g_ref[...], lam_init, ts)


def _attn_sample(q, kb, vb, k_past, v_past, layer, lam_qk, subln_g, bias_past, bias_new, lam_init):
    b, ts, qkw = q.shape
    n_heads = qkw // HEAD_WIDTH
    p = k_past.shape[2]
    row = pl.BlockSpec((1, ts, qkw), lambda bi: (bi, 0, 0))
    past = pl.BlockSpec((None, 1, p, n_heads, HEAD_WIDTH), lambda bi: (layer, bi, 0, 0, 0))
    return pl.pallas_call(
        functools.partial(_attn_sample_kernel, lam_init=lam_init, n_heads=n_heads),
        grid=(b,),
        in_specs=[
            pl.BlockSpec(lam_qk.shape, lambda bi: (0, 0)),
            pl.BlockSpec((1, HEAD_WIDTH), lambda bi: (0, 0)),
            row, row, row, past, past,
            pl.BlockSpec(bias_past.shape, lambda bi: (0, 0, 0)),
            pl.BlockSpec(bias_new.shape, lambda bi: (0, 0, 0)),
        ],
        out_specs=row,
        out_shape=jax.ShapeDtypeStruct((b, ts, qkw), BF16),
        compiler_params=_cparams(("arbitrary",)),
        name="attn_sample",
    )(lam_qk, subln_g.reshape(1, HEAD_WIDTH), q, kb, vb, k_past, v_past, bias_past, bias_new)


def _post_kernel(*refs, alpha, n_experts):
    moe = n_experts > 0
    it = iter(refs)
    (x_ref, sh1_ref, sc1_ref, g1_ref, sh2_ref, sc2_ref, po_ref, ao_ref, wgate_ref, bgate_ref, wpu_ref, wau_ref,
     wo_ref, lng_ref, lnb_ref) = (next(it) for _ in range(15))
    wr_ref = next(it) if moe else None
    x1_ref = next(it)
    h2_ref = next(it)
    if moe:
        eidx_ref, rank_ref, cw_ref, cnt_ref, hprev_ref = (next(it) for _ in range(5))

    nb, tt, d = x_ref.shape
    rows = nb * tt
    if moe:
        @pl.when(pl.program_id(0) == 0)
        def _():
            hprev_ref[...] = jnp.zeros(hprev_ref.shape, F32)

        h2r = hprev_ref[...]
    x = x_ref[...]
    hb = (x * (1.0 + sc1_ref[...]) + sh1_ref[...]).reshape(rows, d).astype(BF16)
    gates = jax.nn.sigmoid(jnp.dot(hb, wgate_ref[...], preferred_element_type=F32) + bgate_ref[...])
    pu = jnp.dot(_rows2d(po_ref), wpu_ref[...], preferred_element_type=F32)
    au = jnp.dot(_rows2d(ao_ref), wau_ref[...], preferred_element_type=F32)
    merged = gates[:, :d] * pu + gates[:, d:] * au
    mix = jnp.dot(merged.astype(BF16), wo_ref[...], preferred_element_type=F32)
    y = alpha * x + g1_ref[...] * mix.reshape(nb, tt, d)
    x1 = _layer_norm(y, lng_ref[...], lnb_ref[...])
    x1_ref[...] = x1
    h2 = x1 * (1.0 + sc2_ref[...]) + sh2_ref[...]
    h2_ref[...] = h2.astype(BF16)
    if not moe:
        return
    hprev_ref[...] = h2.reshape(rows, d)

    lane = lax.broadcasted_iota(I32, (rows, LANES), 1)
    lanef = lane.astype(F32)
    logits = jnp.full((rows, LANES), -jnp.inf, F32)
    for e in range(n_experts):
        col = jnp.sum(h2r * wr_ref[e:e + 1, :], axis=-1, keepdims=True)
        logits = jnp.where(lane == e, col, logits)
    m1 = jnp.max(logits, axis=-1, keepdims=True)
    i1 = jnp.min(jnp.where(logits == m1, lanef, float(LANES)), axis=-1, keepdims=True)
    rest = jnp.where(lanef == i1, -jnp.inf, logits)
    m2 = jnp.max(rest, axis=-1, keepdims=True)
    i2 = jnp.min(jnp.where(rest == m2, lanef, float(LANES)), axis=-1, keepdims=True)
    t2 = jnp.exp(m2 - m1)
    w1 = 1.0 / (1.0 + t2)
    w2 = t2 / (1.0 + t2)
    oh1 = lanef == i1
    oh2 = lanef == i2
    oh = jnp.logical_or(oh1, oh2).astype(F32)
    tri = (lax.broadcasted_iota(I32, (rows, rows), 0) >= lax.broadcasted_iota(I32, (rows, rows), 1)).astype(BF16)
    before = jnp.dot(tri, oh.astype(BF16), preferred_element_type=F32) - oh
    r1 = jnp.sum(jnp.where(oh1, before, 0.0), axis=-1, keepdims=True)
    r2 = jnp.sum(jnp.where(oh2, before, 0.0), axis=-1, keepdims=True)
    cnt_ref[0] = jnp.sum(oh, axis=0, keepdims=True).astype(I32)
    slot = lax.broadcasted_iota(I32, (rows, 2), 1)
    eidx_ref[...] = jnp.where(slot == 0, i1, i2).astype(I32)
    rank_ref[...] = jnp.where(slot == 0, r1, r2).astype(I32)
    cw_ref[...] = jnp.where(slot == 0, w1, w2)


def _post(x, mod, layer, po, ao, w_gate, b_gate, w_pu, w_au, w_o, ln_g, ln_b, w_router_t, alpha):
    b, t, d = x.shape
    nb, tt = _row_block(b, t)
    rows = nb * tt
    assert nb == 1 or tt == t
    moe = w_router_t is not None
    n_experts = w_router_t.shape[0] if moe else 0
    tpb = t // tt
    n_tiles = (b // nb) * tpb
    lag = 1 if moe else 0
    tile = lambda i: jnp.minimum(i, n_tiles - 1)
    row3 = lambda w: pl.BlockSpec((nb, tt, w), lambda i: (tile(i) // tpb, tile(i) % tpb, 0))
    modc = lambda c: pl.BlockSpec((None, nb, 1, d), lambda i: (layer, tile(i) // tpb, 0, c))
    const2 = lambda a: _resident(a.shape, lambda i: (0, 0))
    tile2 = pl.BlockSpec((rows, 2), lambda i: (jnp.maximum(i - lag, 0), 0))
    in_specs = [row3(d)] + [modc(c) for c in (0, 1, 2, 3, 4)] + [
        row3(po.shape[-1]), row3(ao.shape[-1]),
        const2(w_gate), _resident((1, 2 * d), lambda i: (0, 0)), const2(w_pu), const2(w_au), const2(w_o),
        _resident((1, d), lambda i: (0, 0)), _resident((1, d), lambda i: (0, 0)),
    ]
    args = [x, mod, mod, mod, mod, mod, po, ao, w_gate, b_gate.reshape(1, 2 * d), w_pu, w_au, w_o,
            ln_g.reshape(1, d), ln_b.reshape(1, d)]
    out_specs = [row3(d), row3(d)]
    out_shape = [jax.ShapeDtypeStruct((b, t, d), F32), jax.ShapeDtypeStruct((b, t, d), BF16)]
    scratch = []
    if moe:
        in_specs.append(const2(w_router_t))
        args.append(w_router_t)
        out_specs += [tile2, tile2, tile2, pl.BlockSpec((1, 1, LANES), lambda i: (jnp.maximum(i - lag, 0), 0, 0))]
        out_shape += [
            jax.ShapeDtypeStruct((n_tiles * rows, 2), I32),
            jax.ShapeDtypeStruct((n_tiles * rows, 2), I32),
            jax.ShapeDtypeStruct((n_tiles * rows, 2), F32),
            jax.ShapeDtypeStruct((n_tiles, 1, LANES), I32),
        ]
        scratch = [pltpu.VMEM((rows, d), F32)]
    return pl.pallas_call(
        functools.partial(_post_kernel, alpha=alpha, n_experts=n_experts),
        grid=(n_tiles + lag,),
        in_specs=in_specs,
        out_specs=out_specs,
        out_shape=out_shape,
        scratch_shapes=scratch,
        compiler_params=_cparams(("arbitrary",)),
        name="post_moe" if moe else "post",
    )(*args)


def _ffn_kernel(x1_ref, h2_ref, g2_ref, wg_ref, wu_ref, wd_ref, lng_ref, lnb_ref, o_ref, *, alpha, n_chunks):
    nb, tt, d = x1_ref.shape
    rows = nb * tt
    hb = _rows2d(h2_ref)
    f = wg_ref.shape[1]
    fc = f // n_chunks
    acc = None
    for c in range(n_chunks):
        sl = slice(c * fc, (c + 1) * fc)
        g = jnp.dot(hb, wg_ref[:, sl], preferred_element_type=F32)
        u = jnp.dot(hb, wu_ref[:, sl], preferred_element_type=F32)
        a = (g * jax.nn.sigmoid(g) * u).astype(BF16)
        part = jnp.dot(a, wd_ref[sl, :], preferred_element_type=F32)
        acc = part if acc is None else acc + part
    y = alpha * x1_ref[...] + g2_ref[...] * acc.reshape(nb, tt, d)
    o_ref[...] = _layer_norm(y, lng_ref[...], lnb_ref[...])


def _ffn_dense(x1, h2, mod, layer, wg, wu, wd, ln_g, ln_b, alpha):
    b, t, d = x1.shape
    nb, tt = _row_block(b, t)
    f = wg.shape[1]
    n_chunks = 2 if f % (2 * MXU_WIDTH) == 0 else 1
    row3 = pl.BlockSpec((nb, tt, d), lambda bi, ti: (bi, ti, 0))
    const2 = lambda a: _resident(a.shape, lambda bi, ti: (0, 0))
    vec = _resident((1, d), lambda bi, ti: (0, 0))
    return pl.pallas_call(
        functools.partial(_ffn_kernel, alpha=alpha, n_chunks=n_chunks),
        grid=(b // nb, t // tt),
        in_specs=[row3, row3, _mod_spec(nb, d, layer, 5), const2(wg), const2(wu), const2(wd), vec, vec],
        out_specs=row3,
        out_shape=jax.ShapeDtypeStruct((b, t, d), F32),
        compiler_params=_cparams(("arbitrary", "arbitrary")),
        name="ffn_dense",
    )(x1, h2, mod, wg, wu, wd, ln_g.reshape(1, d), ln_b.reshape(1, d))


def _moe_plan(cnt, tm, n_tiles_max):
    n_experts = cnt.shape[1]
    pc = (cnt + MOE_CHUNK - 1) // MOE_CHUNK * MOE_CHUNK
    seg_start = jnp.cumsum(pc, axis=1) - pc
    base = jnp.cumsum(pc, axis=0) - pc
    total = jnp.sum(pc, axis=0)
    tiles = (total + tm - 1) // tm
    ends = jnp.cumsum(tiles)
    starts = ends - tiles
    n_active = ends[-1]
    tile_ids = jnp.minimum(jnp.arange(n_tiles_max, dtype=I32), jnp.maximum(n_active - 1, 0))
    tile_expert = jnp.sum((tile_ids[:, None] >= ends[None, :]).astype(I32), axis=1)
    flat = lambda a: a.reshape(-1).astype(I32)
    return dict(
        seg_start=seg_start.astype(I32),
        seg=flat(seg_start), nch=flat(pc // MOE_CHUNK), gst=flat(starts[None, :] * tm + base),
        tot=flat(jnp.sum(pc // MOE_CHUNK, axis=1)), total=flat(total),
        last=flat((starts + jnp.maximum(tiles - 1, 0)) * tm),
        tile_expert=flat(jnp.minimum(tile_expert, n_experts - 1)), n_active=flat(n_active),
    )


def _dispatch_kernel(seg_ref, nch_ref, gst_ref, tot_ref, total_ref, last_ref, lpos_ref, h_ref, xs_ref, buf_ref,
                     zero_ref, sem, zsem, *, n_experts, tm, zrows):
    i = pl.program_id(0)
    n = pl.num_programs(0)
    slot = jnp.bitwise_and(i, 1)
    lrows = buf_ref.shape[1]
    rows = h_ref.shape[0]

    def chunk_copy(s, src_row, dst_row):
        return pltpu.make_async_copy(
            buf_ref.at[s, pl.ds(src_row, MOE_CHUNK)], xs_ref.at[pl.ds(dst_row, MOE_CHUNK)], sem.at[s])

    def drain(step):
        def body(c, carry):
            chunk_copy(jnp.bitwise_and(step, 1), 0, 0).wait()
            return carry

        lax.fori_loop(0, tot_ref[step], body, 0)

    @pl.when(i == 0)
    def _():
        zero_ref[...] = jnp.zeros(zero_ref.shape, BF16)

        def zero_copy(e, c):
            row = pl.multiple_of(last_ref[e] + c * zrows, zrows)
            return pltpu.make_async_copy(zero_ref, xs_ref.at[pl.ds(row, zrows)], zsem)

        for e in range(n_experts):
            @pl.when(total_ref[e] > 0)
            def _():
                for c in range(tm // zrows):
                    zero_copy(e, c).start()
        for e in range(n_experts):
            @pl.when(total_ref[e] > 0)
            def _():
                for c in range(tm // zrows):
                    zero_copy(e, c).wait()

    @pl.when(i >= 2)
    def _():
        drain(i - 2)

    lp = lpos_ref[0]
    row = lax.broadcasted_iota(I32, (lrows, rows), 0)
    sel = jnp.logical_or(row == lp[0:1, :], row == lp[1:2, :]).astype(BF16)
    buf_ref[slot] = jnp.dot(sel, h_ref[...], preferred_element_type=F32).astype(BF16)

    for e in range(n_experts):
        idx = i * n_experts + e
        src0 = seg_ref[idx]
        dst0 = gst_ref[idx]

        def body(c, carry, src0=src0, dst0=dst0):
            chunk_copy(slot, pl.multiple_of(src0 + c * MOE_CHUNK, MOE_CHUNK),
                       pl.multiple_of(dst0 + c * MOE_CHUNK, MOE_CHUNK)).start()
            return carry

        lax.fori_loop(0, nch_ref[idx], body, 0)

    @pl.when(i == n - 1)
    def _():
        @pl.when(i >= 1)
        def _():
            drain(i - 1)

        drain(i)


def _dispatch(h2, lpos_t, plan, rows, tm, n_rows_total):
    n, d = h2.shape
    n_tiles = n // rows
    n_experts = plan["total"].shape[0]
    zrows = min(tm, 256)
    lrows = _local_rows(rows, n_experts)
    grid_spec = pltpu.PrefetchScalarGridSpec(
        num_scalar_prefetch=6,
        grid=(n_tiles,),
        in_specs=[
            pl.BlockSpec((1, 2, rows), lambda i, *_: (i, 0, 0)),
            pl.BlockSpec((rows, d), lambda i, *_: (i, 0)),
        ],
        out_specs=pl.BlockSpec(memory_space=pl.ANY),
        scratch_shapes=[
            pltpu.VMEM((2, lrows, d), BF16), pltpu.VMEM((zrows, d), BF16),
            pltpu.SemaphoreType.DMA((2,)), pltpu.SemaphoreType.DMA(()),
        ],
    )
    return pl.pallas_call(
        functools.partial(_dispatch_kernel, n_experts=n_experts, tm=tm, zrows=zrows),
        grid_spec=grid_spec,
        out_shape=jax.ShapeDtypeStruct((n_rows_total, d), BF16),
        compiler_params=_cparams(("arbitrary",)),
        name="moe_dispatch",
    )(plan["seg"], plan["nch"], plan["gst"], plan["tot"], plan["total"], plan["last"], lpos_t, h2)


def _moe_ffn_kernel(te_ref, na_ref, xs_ref, wg_ref, wu_ref, wd_ref, y_ref, acc_ref):
    i = pl.program_id(0)
    j = pl.program_id(1)

    @pl.when(i < na_ref[0])
    def _():
        @pl.when(j == 0)
        def _():
            acc_ref[...] = jnp.zeros(acc_ref.shape, F32)

        xb = xs_ref[...]
        g = jnp.dot(xb, wg_ref[...], preferred_element_type=F32)
        u = jnp.dot(xb, wu_ref[...], preferred_element_type=F32)
        a = (g * jax.nn.sigmoid(g) * u).astype(BF16)
        acc_ref[...] += jnp.dot(a, wd_ref[...], preferred_element_type=F32)

        @pl.when(j == pl.num_programs(1) - 1)
        def _():
            y_ref[...] = acc_ref[...].astype(BF16)


def _moe_ffn(xs, tile_expert, n_active, wg, wu, wd, tm):
    r, d = xs.shape
    f = wg.shape[2]
    fc = MOE_FC
    assert f % fc == 0 and r % tm == 0
    nj = f // fc

    def row_map(i, j, te, na):
        return (jnp.minimum(i, jnp.maximum(na[0] - 1, 0)), 0)

    def jj(i, j, na):
        return jnp.where(i < na[0], j, nj - 1)

    grid_spec = pltpu.PrefetchScalarGridSpec(
        num_scalar_prefetch=2,
        grid=(r // tm, nj),
        in_specs=[
            pl.BlockSpec((tm, d), row_map),
            pl.BlockSpec((None, d, fc), lambda i, j, te, na: (te[i], 0, jj(i, j, na))),
            pl.BlockSpec((None, d, fc), lambda i, j, te, na: (te[i], 0, jj(i, j, na))),
            pl.BlockSpec((None, fc, d), lambda i, j, te, na: (te[i], jj(i, j, na), 0)),
        ],
        out_specs=pl.BlockSpec((tm, d), row_map),
        scratch_shapes=[pltpu.VMEM((tm, d), F32)],
    )
    return pl.pallas_call(
        _moe_ffn_kernel,
        grid_spec=grid_spec,
        out_shape=jax.ShapeDtypeStruct((r, d), BF16),
        compiler_params=_cparams(("arbitrary", "arbitrary")),
        name="moe_ffn",
    )(tile_expert, n_active, xs, wg, wu, wd)


def _combine_kernel(seg_ref, nch_ref, gst_ref, tot_ref, y_ref, x1_ref, g2_ref, lpos_ref, cw_ref, lng_ref, lnb_ref,
                    o_ref, buf_ref, sem, *, alpha, n_experts, n_tiles):
    nb, tt, d = x1_ref.shape
    rows = nb * tt
    lrows = buf_ref.shape[1]
    tile = pl.program_id(0) * pl.num_programs(1) + pl.program_id(1)
    slot = jnp.bitwise_and(tile, 1)

    def chunk_copy(s, src_row, dst_row):
        return pltpu.make_async_copy(
            y_ref.at[pl.ds(src_row, MOE_CHUNK)], buf_ref.at[s, pl.ds(dst_row, MOE_CHUNK)], sem.at[s])

    def fetch(step):
        s = jnp.bitwise_and(step, 1)
        buf_ref[s] = jnp.zeros((lrows, d), BF16)
        for e in range(n_experts):
            idx = step * n_experts + e
            src0 = gst_ref[idx]
            dst0 = seg_ref[idx]

            def body(c, carry, src0=src0, dst0=dst0):
                chunk_copy(s, pl.multiple_of(src0 + c * MOE_CHUNK, MOE_CHUNK),
                           pl.multiple_of(dst0 + c * MOE_CHUNK, MOE_CHUNK)).start()
                return carry

            lax.fori_loop(0, nch_ref[idx], body, 0)

    @pl.when(tile == 0)
    def _():
        fetch(tile)

    @pl.when(tile + 1 < n_tiles)
    def _():
        fetch(tile + 1)

    def wait(c, carry):
        chunk_copy(slot, 0, 0).wait()
        return carry

    lax.fori_loop(0, tot_ref[tile], wait, 0)
    lp = lpos_ref[...]
    lane = lax.broadcasted_iota(I32, (rows, lrows), 1)
    pick = jnp.concatenate([lane == lp[:, 0:1], lane == lp[:, 1:2]], axis=0).astype(BF16)
    g = jnp.dot(pick, buf_ref[slot], preferred_element_type=F32)
    cw = cw_ref[...]
    f = cw[:, 0:1] * g[:rows] + cw[:, 1:2] * g[rows:]
    y = alpha * x1_ref[...] + g2_ref[...] * f.reshape(nb, tt, d)
    o_ref[...] = _layer_norm(y, lng_ref[...], lnb_ref[...])


def _combine(y, x1, mod, layer, lpos, cw, plan, ln_g, ln_b, alpha):
    b, t, d = x1.shape
    nb, tt = _row_block(b, t)
    rows = nb * tt
    tpb = t // tt
    n_tiles = (b // nb) * tpb
    n_experts = plan["total"].shape[0]
    row3 = pl.BlockSpec((nb, tt, d), lambda bi, ti, *_: (bi, ti, 0))
    tile2 = pl.BlockSpec((rows, 2), lambda bi, ti, *_: (bi * tpb + ti, 0))
    vec = _resident((1, d), lambda bi, ti, *_: (0, 0))
    grid_spec = pltpu.PrefetchScalarGridSpec(
        num_scalar_prefetch=4,
        grid=(b // nb, tpb),
        in_specs=[
            pl.BlockSpec(memory_space=pl.ANY), row3,
            pl.BlockSpec((None, nb, 1, d), lambda bi, ti, *_: (layer, bi, 0, 5)), tile2, tile2, vec, vec,
        ],
        out_specs=row3,
        scratch_shapes=[pltpu.VMEM((2, _local_rows(rows, n_experts), d), BF16), pltpu.SemaphoreType.DMA((2,))],
    )
    return pl.pallas_call(
        functools.partial(_combine_kernel, alpha=alpha, n_experts=n_experts, n_tiles=n_tiles),
        grid_spec=grid_spec,
        out_shape=jax.ShapeDtypeStruct((b, t, d), F32),
        compiler_params=_cparams(("arbitrary", "arbitrary")),
        name="moe_combine",
    )(plan["seg"], plan["nch"], plan["gst"], plan["tot"], y, x1, mod, lpos, cw, ln_g.reshape(1, d),
      ln_b.reshape(1, d))


def _local_rows(rows, n_experts):
    need = 2 * rows + n_experts * (MOE_CHUNK - 1)
    return (need + LANES - 1) // LANES * LANES


def _moe(x1, h2, mod, layer, eidx, lrank, cw, cnt_tile, wg, wu, wd, ln_g, ln_b, alpha):
    b, t, d = x1.shape
    n = b * t
    nb, tt = _row_block(b, t)
    rows = nb * tt
    n_tiles = n // rows
    n_experts = wg.shape[0]
    tm = MOE_TM_LARGE if 2 * n >= 8 * n_experts * MOE_TM_LARGE else MOE_TM_SMALL
    n_tiles_max = -(-(2 * n + n_tiles * n_experts * (MOE_CHUNK - 1)) // tm) + n_experts
    plan = _moe_plan(cnt_tile[:, 0, :n_experts], tm, n_tiles_max)
    e3 = eidx.reshape(n_tiles, rows, 2)
    onehot = e3[..., None] == jnp.arange(n_experts, dtype=I32)
    seg_of_pair = jnp.sum(jnp.where(onehot, plan["seg_start"][:, None, None, :], 0), axis=-1)
    lpos = lrank.reshape(n_tiles, rows, 2) + seg_of_pair
    xs = _dispatch(h2.reshape(n, d), jnp.transpose(lpos, (0, 2, 1)), plan, rows, tm, n_tiles_max * tm)
    y = _moe_ffn(xs, plan["tile_expert"], plan["n_active"], wg, wu, wd, tm)
    return _combine(y, x1, mod, layer, lpos.reshape(n, 2), cw, plan, ln_g, ln_b, alpha)


def _trunk(x, mod, pos0, pool_hist, k_past, v_past, biases, W, depth):
    b, t, d = x.shape
    n_heads = d // HEAD_WIDTH
    qkw = n_heads * HEAD_WIDTH
    alpha = (2.0 * depth) ** 0.25
    k5 = v5 = None
    pool_states = []
    o4 = POOL_WIDTH + 3 * qkw
    for l in range(depth):
        lam_init = 0.8 - 0.6 * math.exp(-0.3 * l)
        w_a = W["w_in"][l, :, :o4].astype(BF16)
        w_gate = W["w_in"][l, :, o4:].astype(BF16)
        hist = None if pool_hist is None else pool_hist[l]
        q, kb, vb, k5, v5, po, pst = _inproj(
            x, mod, l, w_a, W["pool_w"][l].astype(BF16), W["pool_scale"][l], hist, k5, v5, depth, pos0)
        pool_states.append(pst)
        if k_past is None:
            ao = _attn_prompt(q, kb, vb, W["lam_qk"][l], W["subln_g"][l], biases[0], biases[1], lam_init)
        else:
            ao = _attn_sample(q, kb, vb, k_past, v_past, l, W["lam_qk"][l], W["subln_g"][l], biases[0], biases[1],
                              lam_init)
        moe = l % 2 == 1
        i = l // 2
        w_router_t = jnp.transpose(W["w_router"][i]) if moe else None
        outs = _post(x, mod, l, po, ao, w_gate, W["b_gate"][l], W["w_pool_up"][l].astype(BF16),
                     W["w_attn_up"][l].astype(BF16), W["w_o"][l].astype(BF16), W["ln_g"][l, 0], W["ln_b"][l, 0],
                     w_router_t, alpha)
        if not moe:
            x1, h2 = outs
            x = _ffn_dense(x1, h2, mod, l, W["w_ffn_gate"][i].astype(BF16), W["w_ffn_up"][i].astype(BF16),
                           W["w_ffn_down"][i].astype(BF16), W["ln_g"][l, 1], W["ln_b"][l, 1], alpha)
        else:
            x1, h2, eidx, rank, cw, cnt = outs
            x = _moe(x1, h2, mod, l, eidx, rank, cw, cnt, W["w_exp_gate"][i].astype(BF16),
                     W["w_exp_up"][i].astype(BF16), W["w_exp_down"][i].astype(BF16), W["ln_g"][l, 1], W["ln_b"][l, 1],
                     alpha)
    return x, k5, v5, jnp.stack(pool_states)


def kernel(x_prompt, x_sample, cache_k, cache_v, state_pool, c_prompt, c_sample, rel_bias, w_ada, b_ada, w_in, b_gate,
           pool_w, pool_scale, lam_qk, subln_g, w_pool_up, w_attn_up, w_o, ln_g, ln_b, w_ffn_gate, w_ffn_up,
           w_ffn_down, w_router, w_exp_gate, w_exp_up, w_exp_down):
    W = dict(w_in=w_in, b_gate=b_gate, pool_w=pool_w, pool_scale=pool_scale, lam_qk=lam_qk, subln_g=subln_g,
             w_pool_up=w_pool_up, w_attn_up=w_attn_up, w_o=w_o, ln_g=ln_g, ln_b=ln_b, w_ffn_gate=w_ffn_gate,
             w_ffn_up=w_ffn_up, w_ffn_down=w_ffn_down, w_router=w_router, w_exp_gate=w_exp_gate, w_exp_up=w_exp_up,
             w_exp_down=w_exp_down)
    depth, d, _ = w_in.shape
    bp, s, _ = x_prompt.shape
    bs, ts, _ = x_sample.shape
    p = cache_k.shape[2]

    mod = _ada(jnp.concatenate([c_prompt, c_sample], axis=0), w_ada, b_ada)
    mod_p = mod[:, :bp].reshape(depth, bp, 1, 6 * d)
    mod_s = mod[:, bp:].reshape(depth, bs, 1, 6 * d)

    tq = ATTN_TQ
    q_pos = np.arange(tq, 2 * tq)
    bias_prev = _bias_table(rel_bias, _bucket_table(q_pos, np.arange(0, tq)))
    bias_diag = _bias_table(rel_bias, _bucket_table(q_pos, q_pos))
    s_pos = p + np.arange(ts)
    bias_past = _bias_table(rel_bias, _bucket_table(s_pos, np.arange(p)))
    bias_new = _bias_table(rel_bias, _bucket_table(s_pos, s_pos))

    y_p, k_p, v_p, pool_p = _trunk(x_prompt, mod_p, 0, None, None, None, (bias_prev, bias_diag), W, depth)
    y_s, k_s, v_s, pool_s = _trunk(x_sample, mod_s, p, state_pool, cache_k, cache_v, (bias_past, bias_new), W, depth)
    return (y_p, y_s, k_p, v_p, pool_p, k_s, v_s, pool_s)
```

```python
import functools
import math

import numpy as np
import jax
import jax.numpy as jnp
from jax import lax
from jax.experimental import pallas as pl
from jax.experimental.pallas import tpu as pltpu

F32 = jnp.float32
BF16 = jnp.bfloat16
I32 = jnp.int32

CHUNK = 64
HEAD_DIM = 64
HEAD_WIDTH = 2 * HEAD_DIM
POOL_WINDOWS = (2, 4, 8, 16)
POOL_GROUP_DIM = 128
POOL_WIDTH = len(POOL_WINDOWS) * POOL_GROUP_DIM
POOL_HIST = max(POOL_WINDOWS) - 1
HIST_ROWS = 16
N_BUCKETS = 32
MAX_DISTANCE = 128
FAR_BUCKET = N_BUCKETS // 2 - 1
LN_EPS = 1e-5
RMS_EPS = 1e-5
MASKED = -1e30
LOG2E = math.log2(math.e)

LANES = 128
MXU_WIDTH = 256
ROW_TILE = 512
ATTN_TQ = 512
ATTN_TK = 512
ATTN_HEADS_PER_STEP = 8
MOE_CHUNK = 16
MOE_TM_LARGE = 1024
MOE_TM_SMALL = 256
MOE_FC = 512
VMEM_LIMIT = 56 * 1024 * 1024


def _cparams(sem):
    return pltpu.CompilerParams(dimension_semantics=sem, vmem_limit_bytes=VMEM_LIMIT)


def _resident(shape, index_map):
    return pl.BlockSpec(shape, index_map, pipeline_mode=pl.Buffered(1))


def _row_block(b, t):
    if t >= ROW_TILE:
        assert t % ROW_TILE == 0
        return 1, ROW_TILE
    nb = min(b, ROW_TILE // t)
    assert b % nb == 0 and t % 8 == 0
    return nb, t


def _rows2d(ref):
    nb, tt, w = ref.shape
    return ref[0] if nb == 1 else ref[...].reshape(nb * tt, w)


def _layer_norm(y, g, b):
    mu = jnp.mean(y, axis=-1, keepdims=True)
    yc = y - mu
    var = jnp.mean(yc * yc, axis=-1, keepdims=True)
    return yc * lax.rsqrt(var + LN_EPS) * g + b


def _ada_kernel(c_ref, w_ref, b_ref, o_ref):
    c = c_ref[...]
    s = c * jax.nn.sigmoid(c)
    o_ref[0] = jnp.dot(s.astype(BF16), w_ref[0].astype(BF16), preferred_element_type=F32) + b_ref[0]


def _ada(c_all, w_ada, b_ada):
    depth, d, n6 = w_ada.shape
    bc = c_all.shape[0]
    tn = 1536 if n6 % 1536 == 0 else n6
    return pl.pallas_call(
        _ada_kernel,
        grid=(depth, n6 // tn),
        in_specs=[
            pl.BlockSpec((bc, d), lambda l, j: (0, 0)),
            pl.BlockSpec((1, d, tn), lambda l, j: (l, 0, j)),
            pl.BlockSpec((1, 1, tn), lambda l, j: (l, 0, j)),
        ],
        out_specs=pl.BlockSpec((1, bc, tn), lambda l, j: (l, 0, j)),
        out_shape=jax.ShapeDtypeStruct((depth, bc, n6), F32),
        compiler_params=_cparams(("arbitrary", "arbitrary")),
        name="ada",
    )(c_all, w_ada, b_ada.reshape(depth, 1, n6))


def _mod_spec(nb, d, layer, chunk):
    return pl.BlockSpec((None, nb, 1, d), lambda bi, ti: (layer, bi, 0, chunk))


def _rel_bucket_np(rel):
    nb = N_BUCKETS // 2
    max_exact = nb // 2
    n = np.abs(rel)
    large = max_exact + (
        np.log(np.maximum(n, 1).astype(np.float32) / np.float32(max_exact))
        / np.float32(math.log(MAX_DISTANCE / max_exact))
        * np.float32(nb - max_exact)
    ).astype(np.int32)
    large = np.minimum(large, nb - 1)
    return np.where(rel > 0, nb, 0) + np.where(n < max_exact, n, large)


def _bucket_table(q_pos, k_pos):
    rel = k_pos[None, :] - q_pos[:, None]
    allowed = (k_pos[None, :] // CHUNK) <= (q_pos[:, None] // CHUNK)
    return np.where(allowed, _rel_bucket_np(rel), -1).astype(np.int32)


def _bias_kernel(rb_ref, bk_ref, o_ref):
    h = pl.program_id(0)
    bk = bk_ref[...]
    far = rb_ref[FAR_BUCKET, h]
    acc = jnp.zeros(bk.shape, F32)
    for b in range(N_BUCKETS):
        acc = jnp.where(bk == b, (rb_ref[b, h] - far) * LOG2E, acc)
    o_ref[0] = jnp.where(bk < 0, MASKED, acc)


def _bias_table(rel_bias, bucket_np):
    rows, cols = bucket_np.shape
    n_heads = rel_bias.shape[1]
    return pl.pallas_call(
        _bias_kernel,
        grid=(n_heads,),
        in_specs=[
            pl.BlockSpec(memory_space=pltpu.SMEM),
            pl.BlockSpec((rows, cols), lambda h: (0, 0)),
        ],
        out_specs=pl.BlockSpec((1, rows, cols), lambda h: (h, 0, 0)),
        out_shape=jax.ShapeDtypeStruct((n_heads, rows, cols), F32),
        compiler_params=_cparams(("arbitrary",)),
        name="bias_table",
    )(rel_bias, jnp.asarray(bucket_np))


def _inproj_kernel(*refs, pos0, has_hist, has_alias, n_heads):
    it = iter(refs)
    x_ref, sh_ref, sc_ref, w_ref, pw_ref, ps_ref = (next(it) for _ in range(6))
    hist_ref = next(it) if has_hist else None
    if has_alias:
        next(it)
        next(it)
    q_ref, kb_ref, vb_ref, k_ref, v_ref, po_ref, pst_ref, ext_ref = (next(it) for _ in range(8))

    ti = pl.program_id(1)
    nb, tt, d = x_ref.shape
    rows = nb * tt
    qkw = n_heads * HEAD_WIDTH
    o1 = POOL_WIDTH
    o2 = o1 + qkw
    o3 = o2 + qkw
    o4 = o3 + qkw

    @pl.when(ti == 0)
    def _():
        if has_hist:
            ext_ref[:, 0:1, :] = jnp.zeros((nb, 1, POOL_WIDTH), F32)
            ext_ref[:, 1:HIST_ROWS, :] = hist_ref[...]
        else:
            ext_ref[:, 0:HIST_ROWS, :] = jnp.zeros((nb, HIST_ROWS, POOL_WIDTH), F32)

    h = x_ref[...] * (1.0 + sc_ref[...]) + sh_ref[...]
    hb = h.reshape(rows, d).astype(BF16)

    u = jnp.dot(hb, w_ref[:, 0:o1], preferred_element_type=F32)
    q = jnp.dot(hb, w_ref[:, o1:o2], preferred_element_type=F32)
    q_ref[...] = (q * (HEAD_DIM ** -0.5 * LOG2E)).reshape(nb, tt, qkw).astype(BF16)
    k = jnp.dot(hb, w_ref[:, o2:o3], preferred_element_type=F32)
    kb_ref[...] = k.reshape(nb, tt, qkw).astype(BF16)
    v = jnp.dot(hb, w_ref[:, o3:o4], preferred_element_type=F32)
    vb_ref[...] = v.reshape(nb, tt, qkw).astype(BF16)
    k_ref[...] = k.reshape(nb, tt, n_heads, HEAD_WIDTH)
    v_ref[...] = v.reshape(nb, tt, n_heads, HEAD_WIDTH)

    ext_ref[:, HIST_ROWS:HIST_ROWS + tt, :] = u.reshape(nb, tt, POOL_WIDTH)
    pos = pos0 + ti * tt + lax.broadcasted_iota(I32, (1, tt, 1), 1)
    outs = []
    for gi, w in enumerate(POOL_WINDOWS):
        ls = slice(gi * POOL_GROUP_DIM, (gi + 1) * POOL_GROUP_DIM)
        tot = ext_ref[:, HIST_ROWS:HIST_ROWS + tt, ls]
        for j in range(1, w):
            tot = tot + ext_ref[:, HIST_ROWS - j:HIST_ROWS - j + tt, ls]
        cnt = jnp.minimum(pos + 1, w).astype(F32)
        dlt = (tot / cnt - ext_ref[:, HIST_ROWS:HIST_ROWS + tt, ls]).reshape(rows, POOL_GROUP_DIM)
        outs.append(jnp.dot(dlt.astype(BF16), pw_ref[gi], preferred_element_type=F32))
    po = jnp.concatenate(outs, axis=-1) * ps_ref[...]
    po_ref[...] = po.reshape(nb, tt, POOL_WIDTH).astype(BF16)
    pst_ref[...] = ext_ref[:, tt + 1:tt + HIST_ROWS, :]
    ext_ref[:, 0:HIST_ROWS, :] = ext_ref[:, tt:tt + HIST_ROWS, :]


def _inproj(x, mod, layer, w_a, pool_w, pool_scale, hist, k5, v5, depth, pos0):
    b, t, d = x.shape
    nb, tt = _row_block(b, t)
    n_heads = d // HEAD_WIDTH
    qkw = n_heads * HEAD_WIDTH
    has_hist = hist is not None
    has_alias = k5 is not None
    row3 = lambda w: pl.BlockSpec((nb, tt, w), lambda bi, ti: (bi, ti, 0))
    cache_spec = pl.BlockSpec((None, nb, tt, n_heads, HEAD_WIDTH), lambda bi, ti: (layer, bi, ti, 0, 0))
    in_specs = [
        row3(d),
        _mod_spec(nb, d, layer, 0),
        _mod_spec(nb, d, layer, 1),
        _resident(w_a.shape, lambda bi, ti: (0, 0)),
        _resident(pool_w.shape, lambda bi, ti: (0, 0, 0)),
        _resident((1, POOL_WIDTH), lambda bi, ti: (0, 0)),
    ]
    args = [x, mod, mod, w_a, pool_w, pool_scale.reshape(1, POOL_WIDTH)]
    if has_hist:
        in_specs.append(pl.BlockSpec((nb, POOL_HIST, POOL_WIDTH), lambda bi, ti: (bi, 0, 0)))
        args.append(hist)
    aliases = {}
    if has_alias:
        aliases = {len(args): 3, len(args) + 1: 4}
        in_specs += [pl.BlockSpec(memory_space=pl.ANY), pl.BlockSpec(memory_space=pl.ANY)]
        args += [k5, v5]
    cache_shape = jax.ShapeDtypeStruct((depth, b, t, n_heads, HEAD_WIDTH), F32)
    outs = pl.pallas_call(
        functools.partial(_inproj_kernel, pos0=pos0, has_hist=has_hist, has_alias=has_alias, n_heads=n_heads),
        grid=(b // nb, t // tt),
        in_specs=in_specs,
        out_specs=[
            row3(qkw), row3(qkw), row3(qkw), cache_spec, cache_spec, row3(POOL_WIDTH),
            pl.BlockSpec((nb, POOL_HIST, POOL_WIDTH), lambda bi, ti: (bi, 0, 0)),
        ],
        out_shape=[
            jax.ShapeDtypeStruct((b, t, qkw), BF16),
            jax.ShapeDtypeStruct((b, t, qkw), BF16),
            jax.ShapeDtypeStruct((b, t, qkw), BF16),
            cache_shape, cache_shape,
            jax.ShapeDtypeStruct((b, t, POOL_WIDTH), BF16),
            jax.ShapeDtypeStruct((b, POOL_HIST, POOL_WIDTH), F32),
        ],
        scratch_shapes=[pltpu.VMEM((nb, tt + HIST_ROWS, POOL_WIDTH), F32)],
        input_output_aliases=aliases,
        compiler_params=_cparams(("arbitrary", "arbitrary")),
        name="inproj",
    )(*args)
    return outs


def _lambda(lam_ref, lam_init):
    lq = lam_ref[...]
    s01 = jnp.sum(lq[0:1, :] * lq[1:2, :], axis=-1, keepdims=True)
    s23 = jnp.sum(lq[2:3, :] * lq[3:4, :], axis=-1, keepdims=True)
    return jnp.exp(s01) - jnp.exp(s23) + lam_init


def _stack_maps(q):
    lane = lax.broadcasted_iota(I32, q.shape, 1)
    zero = jnp.zeros_like(q)
    return jnp.concatenate([jnp.where(lane < HEAD_DIM, q, zero), jnp.where(lane >= HEAD_DIM, q, zero)], axis=0)


def _diff_out(acc, l, lam, g, lam_init, rows):
    o = acc / l
    o = o[:rows] - lam * o[rows:]
    o = o * lax.rsqrt(jnp.mean(o * o, axis=-1, keepdims=True) + RMS_EPS) * g * (1.0 - lam_init)
    return o.astype(BF16)


def _nt_dot(a, b):
    return lax.dot_general(a, b, (((1,), (1,)), ((), ())), preferred_element_type=F32)


def _attn_prompt_kernel(lam_ref, g_ref, q_ref, k_ref, v_ref, bprev_ref, bdiag_ref, o_ref, qs_ref, m_ref, acc_ref,
                        *, lam_init, hg):
    qi = pl.program_id(2)
    tq = q_ref.shape[1]
    tk = ATTN_TK
    for h in range(hg):
        qs_ref[h] = _stack_maps(q_ref[0, :, h * HEAD_WIDTH:(h + 1) * HEAD_WIDTH])
    hq = tq // 2
    lower = (slice(hq, tq), slice(tq + hq, 2 * tq))

    def update(start, width, bias, first=False, lower_half=False):
        ones = jnp.ones((width, LANES), BF16)
        for h in range(hg):
            sl = slice(h * HEAD_WIDTH, (h + 1) * HEAD_WIDTH)
            lhs = jnp.concatenate([qs_ref[h, r] for r in lower], axis=0) if lower_half else qs_ref[h]
            s = _nt_dot(lhs, k_ref[0, pl.ds(start, width), sl])
            if bias is not None:
                tile_bias = bias(h)
                s = s + jnp.concatenate([tile_bias, tile_bias], axis=0)
            m_new = jnp.broadcast_to(jnp.max(s, axis=-1, keepdims=True), (s.shape[0], LANES))
            if not first:
                m_prev = jnp.concatenate([m_ref[h, r] for r in lower], axis=0) if lower_half else m_ref[h]
                m_new = jnp.maximum(m_prev, m_new)
            pb = jnp.concatenate(
                [jnp.exp2(s[:, c * LANES:(c + 1) * LANES] - m_new).astype(BF16) for c in range(width // LANES)],
                axis=-1)
            v_ext = jnp.concatenate([v_ref[0, pl.ds(start, width), sl], ones], axis=-1)
            acc = jnp.dot(pb, v_ext, preferred_element_type=F32)
            if not first:
                alpha = jnp.exp2(m_prev - m_new)
                acc_prev = jnp.concatenate([acc_ref[h, r] for r in lower], axis=0) if lower_half else acc_ref[h]
                acc = jnp.concatenate([alpha, alpha], axis=-1) * acc_prev + acc
            if lower_half:
                for n, r in enumerate(lower):
                    m_ref[h, r] = m_new[n * hq:(n + 1) * hq]
                    acc_ref[h, r] = acc[n * hq:(n + 1) * hq]
            else:
                m_ref[h] = m_new
                acc_ref[h] = acc

    diag = pl.multiple_of(qi * tk, tk)
    update(diag, tk // 2, lambda h: bdiag_ref[h, :, 0:tk // 2], first=True)
    update(pl.multiple_of(diag + tk // 2, tk // 2), tk // 2, lambda h: bdiag_ref[h, hq:tq, tk // 2:tk],
           lower_half=True)

    @pl.when(qi >= 1)
    def _():
        prev = pl.multiple_of((qi - 1) * tk, tk)
        update(prev, tk // 2, lambda h: bprev_ref[h, :, 0:tk // 2])
        update(pl.multiple_of(prev + tk // 2, tk // 2), tk // 2, lambda h: bprev_ref[h, :, tk // 2:tk])

    def far_body(j, carry):
        far = pl.multiple_of(j * tk, tk)
        update(far, tk // 2, None)
        update(pl.multiple_of(far + tk // 2, tk // 2), tk // 2, None)
        return carry

    lax.fori_loop(0, jnp.maximum(qi - 1, 0), far_body, 0)
    lam = _lambda(lam_ref, lam_init)
    for h in range(hg):
        acc = acc_ref[h]
        o_ref[0, :, h * HEAD_WIDTH:(h + 1) * HEAD_WIDTH] = _diff_out(
            acc[:, :HEAD_WIDTH], acc[:, HEAD_WIDTH:], lam, g_ref[...], lam_init, tq)


def _attn_prompt(q, kb, vb, lam_qk, subln_g, bias_prev, bias_diag, lam_init):
    b, t, qkw = q.shape
    n_heads = qkw // HEAD_WIDTH
    tq = ATTN_TQ
    hg = math.gcd(n_heads, ATTN_HEADS_PER_STEP)
    gw = hg * HEAD_WIDTH
    assert tq == ATTN_TK and t % tq == 0 and (tq // 2) % CHUNK == 0
    return pl.pallas_call(
        functools.partial(_attn_prompt_kernel, lam_init=lam_init, hg=hg),
        grid=(n_heads // hg, b, t // tq),
        in_specs=[
            pl.BlockSpec(lam_qk.shape, lambda g, bi, qi: (0, 0)),
            pl.BlockSpec((1, HEAD_WIDTH), lambda g, bi, qi: (0, 0)),
            pl.BlockSpec((1, tq, gw), lambda g, bi, qi: (bi, qi, g)),
            pl.BlockSpec((1, t, gw), lambda g, bi, qi: (bi, 0, g)),
            pl.BlockSpec((1, t, gw), lambda g, bi, qi: (bi, 0, g)),
            pl.BlockSpec((hg, tq, ATTN_TK), lambda g, bi, qi: (g, 0, 0), pipeline_mode=pl.Buffered(1)),
            pl.BlockSpec((hg, tq, ATTN_TK), lambda g, bi, qi: (g, 0, 0), pipeline_mode=pl.Buffered(1)),
        ],
        out_specs=pl.BlockSpec((1, tq, gw), lambda g, bi, qi: (bi, qi, g)),
        out_shape=jax.ShapeDtypeStruct((b, t, qkw), BF16),
        scratch_shapes=[
            pltpu.VMEM((hg, 2 * tq, HEAD_WIDTH), BF16),
            pltpu.VMEM((hg, 2 * tq, LANES), F32),
            pltpu.VMEM((hg, 2 * tq, HEAD_WIDTH + LANES), F32),
        ],
        compiler_params=_cparams(("arbitrary", "arbitrary", "arbitrary")),
        name="attn_prompt",
    )(lam_qk, subln_g.reshape(1, HEAD_WIDTH), q, kb, vb, bias_prev, bias_diag)


def _attn_sample_kernel(lam_ref, g_ref, q_ref, kn_ref, vn_ref, kp_ref, vp_ref, bpast_ref, bnew_ref, o_ref,
                        *, lam_init, n_heads):
    ts = q_ref.shape[1]
    lam = _lambda(lam_ref, lam_init)
    past_len = kp_ref.shape[1]
    kp_all = kp_ref[0].reshape(past_len, n_heads * HEAD_WIDTH).astype(BF16)
    vp_all = vp_ref[0].reshape(past_len, n_heads * HEAD_WIDTH).astype(BF16)
    for hh in range(n_heads):
        sl = slice(hh * HEAD_WIDTH, (hh + 1) * HEAD_WIDTH)
        qs = _stack_maps(q_ref[0, :, sl])
        kp = kp_all[:, sl]
        vp = vp_all[:, sl]
        bp = bpast_ref[hh]
        bn = bnew_ref[hh]
        s_p = _nt_dot(qs, kp) + jnp.concatenate([bp, bp], axis=0)
        s_n = _nt_dot(qs, kn_ref[0, :, sl]) + jnp.concatenate([bn, bn], axis=0)
        m = jnp.maximum(jnp.max(s_p, axis=-1, keepdims=True), jnp.max(s_n, axis=-1, keepdims=True))
        p_p = jnp.exp2(s_p - m)
        p_n = jnp.exp2(s_n - m)
        l = jnp.sum(p_p, axis=-1, keepdims=True) + jnp.sum(p_n, axis=-1, keepdims=True)
        acc = jnp.dot(p_p.astype(BF16), vp, preferred_element_type=F32) + jnp.dot(
            p_n.astype(BF16), vn_ref[0, :, sl], preferred_element_type=F32)
        o_ref[0, :, sl] = _diff_out(acc, l, lam, g_ref[...], lam_init, ts)


def _attn_sample(q, kb, vb, k_past, v_past, layer, lam_qk, subln_g, bias_past, bias_new, lam_init):
    b, ts, qkw = q.shape
    n_heads = qkw // HEAD_WIDTH
    p = k_past.shape[2]
    row = pl.BlockSpec((1, ts, qkw), lambda bi: (bi, 0, 0))
    past = pl.BlockSpec((None, 1, p, n_heads, HEAD_WIDTH), lambda bi: (layer, bi, 0, 0, 0))
    return pl.pallas_call(
        functools.partial(_attn_sample_kernel, lam_init=lam_init, n_heads=n_heads),
        grid=(b,),
        in_specs=[
            pl.BlockSpec(lam_qk.shape, lambda bi: (0, 0)),
            pl.BlockSpec((1, HEAD_WIDTH), lambda bi: (0, 0)),
            row, row, row, past, past,
            pl.BlockSpec(bias_past.shape, lambda bi: (0, 0, 0)),
            pl.BlockSpec(bias_new.shape, lambda bi: (0, 0, 0)),
        ],
        out_specs=row,
        out_shape=jax.ShapeDtypeStruct((b, ts, qkw), BF16),
        compiler_params=_cparams(("arbitrary",)),
        name="attn_sample",
    )(lam_qk, subln_g.reshape(1, HEAD_WIDTH), q, kb, vb, k_past, v_past, bias_past, bias_new)


def _post_kernel(*refs, alpha, n_experts):
    moe = n_experts > 0
    it = iter(refs)
    (x_ref, sh1_ref, sc1_ref, g1_ref, sh2_ref, sc2_ref, po_ref, ao_ref, wgate_ref, bgate_ref, wpu_ref, wau_ref,
     wo_ref, lng_ref, lnb_ref) = (next(it) for _ in range(15))
    wr_ref = next(it) if moe else None
    x1_ref = next(it)
    h2_ref = next(it)
    if moe:
        eidx_ref, rank_ref, cw_ref, cnt_ref, hprev_ref = (next(it) for _ in range(5))

    nb, tt, d = x_ref.shape
    rows = nb * tt
    if moe:
        @pl.when(pl.program_id(0) == 0)
        def _():
            hprev_ref[...] = jnp.zeros(hprev_ref.shape, F32)

        h2r = hprev_ref[...]
    x = x_ref[...]
    hb = (x * (1.0 + sc1_ref[...]) + sh1_ref[...]).reshape(rows, d).astype(BF16)
    gates = jax.nn.sigmoid(jnp.dot(hb, wgate_ref[...], preferred_element_type=F32) + bgate_ref[...])
    pu = jnp.dot(_rows2d(po_ref), wpu_ref[...], preferred_element_type=F32)
    au = jnp.dot(_rows2d(ao_ref), wau_ref[...], preferred_element_type=F32)
    merged = gates[:, :d] * pu + gates[:, d:] * au
    mix = jnp.dot(merged.astype(BF16), wo_ref[...], preferred_element_type=F32)
    y = alpha * x + g1_ref[...] * mix.reshape(nb, tt, d)
    x1 = _layer_norm(y, lng_ref[...], lnb_ref[...])
    x1_ref[...] = x1
    h2 = x1 * (1.0 + sc2_ref[...]) + sh2_ref[...]
    h2_ref[...] = h2.astype(BF16)
    if not moe:
        return
    hprev_ref[...] = h2.reshape(rows, d)

    lane = lax.broadcasted_iota(I32, (rows, LANES), 1)
    lanef = lane.astype(F32)
    logits = jnp.full((rows, LANES), -jnp.inf, F32)
    for e in range(n_experts):
        col = jnp.sum(h2r * wr_ref[e:e + 1, :], axis=-1, keepdims=True)
        logits = jnp.where(lane == e, col, logits)
    m1 = jnp.max(logits, axis=-1, keepdims=True)
    i1 = jnp.min(jnp.where(logits == m1, lanef, float(LANES)), axis=-1, keepdims=True)
    rest = jnp.where(lanef == i1, -jnp.inf, logits)
    m2 = jnp.max(rest, axis=-1, keepdims=True)
    i2 = jnp.min(jnp.where(rest == m2, lanef, float(LANES)), axis=-1, keepdims=True)
    t2 = jnp.exp(m2 - m1)
    w1 = 1.0 / (1.0 + t2)
    w2 = t2 / (1.0 + t2)
    oh1 = lanef == i1
    oh2 = lanef == i2
    oh = jnp.logical_or(oh1, oh2).astype(F32)
    tri = (lax.broadcasted_iota(I32, (rows, rows), 0) >= lax.broadcasted_iota(I32, (rows, rows), 1)).astype(BF16)
    before = jnp.dot(tri, oh.astype(BF16), preferred_element_type=F32) - oh
    r1 = jnp.sum(jnp.where(oh1, before, 0.0), axis=-1, keepdims=True)
    r2 = jnp.sum(jnp.where(oh2, before, 0.0), axis=-1, keepdims=True)
    cnt_ref[0] = jnp.sum(oh, axis=0, keepdims=True).astype(I32)
    slot = lax.broadcasted_iota(I32, (rows, 2), 1)
    eidx_ref[...] = jnp.where(slot == 0, i1, i2).astype(I32)
    rank_ref[...] = jnp.where(slot == 0, r1, r2).astype(I32)
    cw_ref[...] = jnp.where(slot == 0, w1, w2)


def _post(x, mod, layer, po, ao, w_gate, b_gate, w_pu, w_au, w_o, ln_g, ln_b, w_router_t, alpha):
    b, t, d = x.shape
    nb, tt = _row_block(b, t)
    rows = nb * tt
    assert nb == 1 or tt == t
    moe = w_router_t is not None
    n_experts = w_router_t.shape[0] if moe else 0
    tpb = t // tt
    n_tiles = (b // nb) * tpb
    lag = 1 if moe else 0
    tile = lambda i: jnp.minimum(i, n_tiles - 1)
    row3 = lambda w: pl.BlockSpec((nb, tt, w), lambda i: (tile(i) // tpb, tile(i) % tpb, 0))
    modc = lambda c: pl.BlockSpec((None, nb, 1, d), lambda i: (layer, tile(i) // tpb, 0, c))
    const2 = lambda a: _resident(a.shape, lambda i: (0, 0))
    tile2 = pl.BlockSpec((rows, 2), lambda i: (jnp.maximum(i - lag, 0), 0))
    in_specs = [row3(d)] + [modc(c) for c in (0, 1, 2, 3, 4)] + [
        row3(po.shape[-1]), row3(ao.shape[-1]),
        const2(w_gate), _resident((1, 2 * d), lambda i: (0, 0)), const2(w_pu), const2(w_au), const2(w_o),
        _resident((1, d), lambda i: (0, 0)), _resident((1, d), lambda i: (0, 0)),
    ]
    args = [x, mod, mod, mod, mod, mod, po, ao, w_gate, b_gate.reshape(1, 2 * d), w_pu, w_au, w_o,
            ln_g.reshape(1, d), ln_b.reshape(1, d)]
    out_specs = [row3(d), row3(d)]
    out_shape = [jax.ShapeDtypeStruct((b, t, d), F32), jax.ShapeDtypeStruct((b, t, d), BF16)]
    scratch = []
    if moe:
        in_specs.append(const2(w_router_t))
        args.append(w_router_t)
        out_specs += [tile2, tile2, tile2, pl.BlockSpec((1, 1, LANES), lambda i: (jnp.maximum(i - lag, 0), 0, 0))]
        out_shape += [
            jax.ShapeDtypeStruct((n_tiles * rows, 2), I32),
            jax.ShapeDtypeStruct((n_tiles * rows, 2), I32),
            jax.ShapeDtypeStruct((n_tiles * rows, 2), F32),
            jax.ShapeDtypeStruct((n_tiles, 1, LANES), I32),
        ]
        scratch = [pltpu.VMEM((rows, d), F32)]
    return pl.pallas_call(
        functools.partial(_post_kernel, alpha=alpha, n_experts=n_experts),
        grid=(n_tiles + lag,),
        in_specs=in_specs,
        out_specs=out_specs,
        out_shape=out_shape,
        scratch_shapes=scratch,
        compiler_params=_cparams(("arbitrary",)),
        name="post_moe" if moe else "post",
    )(*args)


def _ffn_kernel(x1_ref, h2_ref, g2_ref, wg_ref, wu_ref, wd_ref, lng_ref, lnb_ref, o_ref, *, alpha, n_chunks):
    nb, tt, d = x1_ref.shape
    rows = nb * tt
    hb = _rows2d(h2_ref)
    f = wg_ref.shape[1]
    fc = f // n_chunks
    acc = None
    for c in range(n_chunks):
        sl = slice(c * fc, (c + 1) * fc)
        g = jnp.dot(hb, wg_ref[:, sl], preferred_element_type=F32)
        u = jnp.dot(hb, wu_ref[:, sl], preferred_element_type=F32)
        a = (g * jax.nn.sigmoid(g) * u).astype(BF16)
        part = jnp.dot(a, wd_ref[sl, :], preferred_element_type=F32)
        acc = part if acc is None else acc + part
    y = alpha * x1_ref[...] + g2_ref[...] * acc.reshape(nb, tt, d)
    o_ref[...] = _layer_norm(y, lng_ref[...], lnb_ref[...])


def _ffn_dense(x1, h2, mod, layer, wg, wu, wd, ln_g, ln_b, alpha):
    b, t, d = x1.shape
    nb, tt = _row_block(b, t)
    f = wg.shape[1]
    n_chunks = 2 if f % (2 * MXU_WIDTH) == 0 else 1
    row3 = pl.BlockSpec((nb, tt, d), lambda bi, ti: (bi, ti, 0))
    const2 = lambda a: _resident(a.shape, lambda bi, ti: (0, 0))
    vec = _resident((1, d), lambda bi, ti: (0, 0))
    return pl.pallas_call(
        functools.partial(_ffn_kernel, alpha=alpha, n_chunks=n_chunks),
        grid=(b // nb, t // tt),
        in_specs=[row3, row3, _mod_spec(nb, d, layer, 5), const2(wg), const2(wu), const2(wd), vec, vec],
        out_specs=row3,
        out_shape=jax.ShapeDtypeStruct((b, t, d), F32),
        compiler_params=_cparams(("arbitrary", "arbitrary")),
        name="ffn_dense",
    )(x1, h2, mod, wg, wu, wd, ln_g.reshape(1, d), ln_b.reshape(1, d))


def _moe_plan(cnt, tm, n_tiles_max):
    n_experts = cnt.shape[1]
    pc = (cnt + MOE_CHUNK - 1) // MOE_CHUNK * MOE_CHUNK
    seg_start = jnp.cumsum(pc, axis=1) - pc
    base = jnp.cumsum(pc, axis=0) - pc
    total = jnp.sum(pc, axis=0)
    tiles = (total + tm - 1) // tm
    ends = jnp.cumsum(tiles)
    starts = ends - tiles
    n_active = ends[-1]
    tile_ids = jnp.minimum(jnp.arange(n_tiles_max, dtype=I32), jnp.maximum(n_active - 1, 0))
    tile_expert = jnp.sum((tile_ids[:, None] >= ends[None, :]).astype(I32), axis=1)
    flat = lambda a: a.reshape(-1).astype(I32)
    return dict(
        seg_start=seg_start.astype(I32),
        seg=flat(seg_start), nch=flat(pc // MOE_CHUNK), gst=flat(starts[None, :] * tm + base),
        tot=flat(jnp.sum(pc // MOE_CHUNK, axis=1)), total=flat(total),
        last=flat((starts + jnp.maximum(tiles - 1, 0)) * tm),
        tile_expert=flat(jnp.minimum(tile_expert, n_experts - 1)), n_active=flat(n_active),
    )


def _dispatch_kernel(seg_ref, nch_ref, gst_ref, tot_ref, total_ref, last_ref, lpos_ref, h_ref, xs_ref, buf_ref,
                     zero_ref, sem, zsem, *, n_experts, tm, zrows):
    i = pl.program_id(0)
    n = pl.num_programs(0)
    slot = jnp.bitwise_and(i, 1)
    lrows = buf_ref.shape[1]
    rows = h_ref.shape[0]

    def chunk_copy(s, src_row, dst_row):
        return pltpu.make_async_copy(
            buf_ref.at[s, pl.ds(src_row, MOE_CHUNK)], xs_ref.at[pl.ds(dst_row, MOE_CHUNK)], sem.at[s])

    def drain(step):
        def body(c, carry):
            chunk_copy(jnp.bitwise_and(step, 1), 0, 0).wait()
            return carry

        lax.fori_loop(0, tot_ref[step], body, 0)

    @pl.when(i == 0)
    def _():
        zero_ref[...] = jnp.zeros(zero_ref.shape, BF16)

        def zero_copy(e, c):
            row = pl.multiple_of(last_ref[e] + c * zrows, zrows)
            return pltpu.make_async_copy(zero_ref, xs_ref.at[pl.ds(row, zrows)], zsem)

        for e in range(n_experts):
            @pl.when(total_ref[e] > 0)
            def _():
                for c in range(tm // zrows):
                    zero_copy(e, c).start()
        for e in range(n_experts):
            @pl.when(total_ref[e] > 0)
            def _():
                for c in range(tm // zrows):
                    zero_copy(e, c).wait()

    @pl.when(i >= 2)
    def _():
        drain(i - 2)

    lp = lpos_ref[0]
    row = lax.broadcasted_iota(I32, (lrows, rows), 0)
    sel = jnp.logical_or(row == lp[0:1, :], row == lp[1:2, :]).astype(BF16)
    buf_ref[slot] = jnp.dot(sel, h_ref[...], preferred_element_type=F32).astype(BF16)

    for e in range(n_experts):
        idx = i * n_experts + e
        src0 = seg_ref[idx]
        dst0 = gst_ref[idx]

        def body(c, carry, src0=src0, dst0=dst0):
            chunk_copy(slot, pl.multiple_of(src0 + c * MOE_CHUNK, MOE_CHUNK),
                       pl.multiple_of(dst0 + c * MOE_CHUNK, MOE_CHUNK)).start()
            return carry

        lax.fori_loop(0, nch_ref[idx], body, 0)

    @pl.when(i == n - 1)
    def _():
        @pl.when(i >= 1)
        def _():
            drain(i - 1)

        drain(i)


def _dispatch(h2, lpos_t, plan, rows, tm, n_rows_total):
    n, d = h2.shape
    n_tiles = n // rows
    n_experts = plan["total"].shape[0]
    zrows = min(tm, 256)
    lrows = _local_rows(rows, n_experts)
    grid_spec = pltpu.PrefetchScalarGridSpec(
        num_scalar_prefetch=6,
        grid=(n_tiles,),
        in_specs=[
            pl.BlockSpec((1, 2, rows), lambda i, *_: (i, 0, 0)),
            pl.BlockSpec((rows, d), lambda i, *_: (i, 0)),
        ],
        out_specs=pl.BlockSpec(memory_space=pl.ANY),
        scratch_shapes=[
            pltpu.VMEM((2, lrows, d), BF16), pltpu.VMEM((zrows, d), BF16),
            pltpu.SemaphoreType.DMA((2,)), pltpu.SemaphoreType.DMA(()),
        ],
    )
    return pl.pallas_call(
        functools.partial(_dispatch_kernel, n_experts=n_experts, tm=tm, zrows=zrows),
        grid_spec=grid_spec,
        out_shape=jax.ShapeDtypeStruct((n_rows_total, d), BF16),
        compiler_params=_cparams(("arbitrary",)),
        name="moe_dispatch",
    )(plan["seg"], plan["nch"], plan["gst"], plan["tot"], plan["total"], plan["last"], lpos_t, h2)


def _moe_ffn_kernel(te_ref, na_ref, xs_ref, wg_ref, wu_ref, wd_ref, y_ref, acc_ref):
    i = pl.program_id(0)
    j = pl.program_id(1)

    last = pl.num_programs(1) - 1

    def partial():
        xb = xs_ref[...]
        g = jnp.dot(xb, wg_ref[...], preferred_element_type=F32)
        u = jnp.dot(xb, wu_ref[...], preferred_element_type=F32)
        a = (g * jax.nn.sigmoid(g) * u).astype(BF16)
        return jnp.dot(a, wd_ref[...], preferred_element_type=F32)

    active = i < na_ref[0]

    @pl.when(jnp.logical_and(active, j == 0))
    def _():
        acc_ref[...] = partial()

    @pl.when(jnp.logical_and(active, jnp.logical_and(j > 0, j < last)))
    def _():
        acc_ref[...] += partial()

    @pl.when(jnp.logical_and(active, j == last))
    def _():
        y_ref[...] = (acc_ref[...] + partial()).astype(BF16)


def _moe_ffn(xs, tile_expert, n_active, wg, wu, wd, tm):
    r, d = xs.shape
    f = wg.shape[2]
    fc = MOE_FC
    assert f % fc == 0 and f // fc >= 2 and r % tm == 0
    nj = f // fc

    def row_map(i, j, te, na):
        return (jnp.minimum(i, jnp.maximum(na[0] - 1, 0)), 0)

    def jj(i, j, na):
        return jnp.where(i < na[0], j, nj - 1)

    grid_spec = pltpu.PrefetchScalarGridSpec(
        num_scalar_prefetch=2,
        grid=(r // tm, nj),
        in_specs=[
            pl.BlockSpec((tm, d), row_map),
            pl.BlockSpec((None, d, fc), lambda i, j, te, na: (te[i], 0, jj(i, j, na))),
            pl.BlockSpec((None, d, fc), lambda i, j, te, na: (te[i], 0, jj(i, j, na))),
            pl.BlockSpec((None, fc, d), lambda i, j, te, na: (te[i], jj(i, j, na), 0)),
        ],
        out_specs=pl.BlockSpec((tm, d), row_map),
        scratch_shapes=[pltpu.VMEM((tm, d), F32)],
    )
    return pl.pallas_call(
        _moe_ffn_kernel,
        grid_spec=grid_spec,
        out_shape=jax.ShapeDtypeStruct((r, d), BF16),
        compiler_params=_cparams(("arbitrary", "arbitrary")),
        name="moe_ffn",
    )(tile_expert, n_active, xs, wg, wu, wd)


def _combine_kernel(seg_ref, nch_ref, gst_ref, tot_ref, y_ref, x1_ref, g2_ref, lpos_ref, cw_ref, lng_ref, lnb_ref,
                    o_ref, buf_ref, sem, *, alpha, n_experts, n_tiles):
    nb, tt, d = x1_ref.shape
    rows = nb * tt
    lrows = buf_ref.shape[1]
    tile = pl.program_id(0) * pl.num_programs(1) + pl.program_id(1)
    slot = jnp.bitwise_and(tile, 1)

    def chunk_copy(s, src_row, dst_row):
        return pltpu.make_async_copy(
            y_ref.at[pl.ds(src_row, MOE_CHUNK)], buf_ref.at[s, pl.ds(dst_row, MOE_CHUNK)], sem.at[s])

    def fetch(step):
        s = jnp.bitwise_and(step, 1)
        buf_ref[s] = jnp.zeros((lrows, d), BF16)
        for e in range(n_experts):
            idx = step * n_experts + e
            src0 = gst_ref[idx]
            dst0 = seg_ref[idx]

            def body(c, carry, src0=src0, dst0=dst0):
                chunk_copy(s, pl.multiple_of(src0 + c * MOE_CHUNK, MOE_CHUNK),
                           pl.multiple_of(dst0 + c * MOE_CHUNK, MOE_CHUNK)).start()
                return carry

            lax.fori_loop(0, nch_ref[idx], body, 0)

    @pl.when(tile == 0)
    def _():
        fetch(tile)

    @pl.when(tile + 1 < n_tiles)
    def _():
        fetch(tile + 1)

    def wait(c, carry):
        chunk_copy(slot, 0, 0).wait()
        return carry

    lax.fori_loop(0, tot_ref[tile], wait, 0)
    lp = lpos_ref[...]
    lane = lax.broadcasted_iota(I32, (rows, lrows), 1)
    pick = jnp.concatenate([lane == lp[:, 0:1], lane == lp[:, 1:2]], axis=0).astype(BF16)
    g = jnp.dot(pick, buf_ref[slot], preferred_element_type=F32)
    cw = cw_ref[...]
    f = cw[:, 0:1] * g[:rows] + cw[:, 1:2] * g[rows:]
    y = alpha * x1_ref[...] + g2_ref[...] * f.reshape(nb, tt, d)
    o_ref[...] = _layer_norm(y, lng_ref[...], lnb_ref[...])


def _combine(y, x1, mod, layer, lpos, cw, plan, ln_g, ln_b, alpha):
    b, t, d = x1.shape
    nb, tt = _row_block(b, t)
    rows = nb * tt
    tpb = t // tt
    n_tiles = (b // nb) * tpb
    n_experts = plan["total"].shape[0]
    row3 = pl.BlockSpec((nb, tt, d), lambda bi, ti, *_: (bi, ti, 0))
    tile2 = pl.BlockSpec((rows, 2), lambda bi, ti, *_: (bi * tpb + ti, 0))
    vec = _resident((1, d), lambda bi, ti, *_: (0, 0))
    grid_spec = pltpu.PrefetchScalarGridSpec(
        num_scalar_prefetch=4,
        grid=(b // nb, tpb),
        in_specs=[
            pl.BlockSpec(memory_space=pl.ANY), row3,
            pl.BlockSpec((None, nb, 1, d), lambda bi, ti, *_: (layer, bi, 0, 5)), tile2, tile2, vec, vec,
        ],
        out_specs=row3,
        scratch_shapes=[pltpu.VMEM((2, _local_rows(rows, n_experts), d), BF16), pltpu.SemaphoreType.DMA((2,))],
    )
    return pl.pallas_call(
        functools.partial(_combine_kernel, alpha=alpha, n_experts=n_experts, n_tiles=n_tiles),
        grid_spec=grid_spec,
        out_shape=jax.ShapeDtypeStruct((b, t, d), F32),
        compiler_params=_cparams(("arbitrary", "arbitrary")),
        name="moe_combine",
    )(plan["seg"], plan["nch"], plan["gst"], plan["tot"], y, x1, mod, lpos, cw, ln_g.reshape(1, d),
      ln_b.reshape(1, d))


def _local_rows(rows, n_experts):
    need = 2 * rows + n_experts * (MOE_CHUNK - 1)
    return (need + LANES - 1) // LANES * LANES


def _moe(x1, h2, mod, layer, eidx, lrank, cw, cnt_tile, wg, wu, wd, ln_g, ln_b, alpha):
    b, t, d = x1.shape
    n = b * t
    nb, tt = _row_block(b, t)
    rows = nb * tt
    n_tiles = n // rows
    n_experts = wg.shape[0]
    tm = MOE_TM_LARGE if 2 * n >= 8 * n_experts * MOE_TM_LARGE else MOE_TM_SMALL
    n_tiles_max = -(-(2 * n + n_tiles * n_experts * (MOE_CHUNK - 1)) // tm) + n_experts
    plan = _moe_plan(cnt_tile[:, 0, :n_experts], tm, n_tiles_max)
    e3 = eidx.reshape(n_tiles, rows, 2)
    onehot = e3[..., None] == jnp.arange(n_experts, dtype=I32)
    seg_of_pair = jnp.sum(jnp.where(onehot, plan["seg_start"][:, None, None, :], 0), axis=-1)
    lpos = lrank.reshape(n_tiles, rows, 2) + seg_of_pair
    xs = _dispatch(h2.reshape(n, d), jnp.transpose(lpos, (0, 2, 1)), plan, rows, tm, n_tiles_max * tm)
    y = _moe_ffn(xs, plan["tile_expert"], plan["n_active"], wg, wu, wd, tm)
    return _combine(y, x1, mod, layer, lpos.reshape(n, 2), cw, plan, ln_g, ln_b, alpha)


def _trunk(x, mod, pos0, pool_hist, k_past, v_past, biases, W, depth):
    b, t, d = x.shape
    n_heads = d // HEAD_WIDTH
    qkw = n_heads * HEAD_WIDTH
    alpha = (2.0 * depth) ** 0.25
    k5 = v5 = None
    pool_states = []
    o4 = POOL_WIDTH + 3 * qkw
    for l in range(depth):
        lam_init = 0.8 - 0.6 * math.exp(-0.3 * l)
        w_a = W["w_in"][l, :, :o4].astype(BF16)
        w_gate = W["w_in"][l, :, o4:].astype(BF16)
        hist = None if pool_hist is None else pool_hist[l]
        q, kb, vb, k5, v5, po, pst = _inproj(
            x, mod, l, w_a, W["pool_w"][l].astype(BF16), W["pool_scale"][l], hist, k5, v5, depth, pos0)
        pool_states.append(pst)
        if k_past is None:
            ao = _attn_prompt(q, kb, vb, W["lam_qk"][l], W["subln_g"][l], biases[0], biases[1], lam_init)
        else:
            ao = _attn_sample(q, kb, vb, k_past, v_past, l, W["lam_qk"][l], W["subln_g"][l], biases[0], biases[1],
                              lam_init)
        moe = l % 2 == 1
        i = l // 2
        w_router_t = jnp.transpose(W["w_router"][i]) if moe else None
        outs = _post(x, mod, l, po, ao, w_gate, W["b_gate"][l], W["w_pool_up"][l].astype(BF16),
                     W["w_attn_up"][l].astype(BF16), W["w_o"][l].astype(BF16), W["ln_g"][l, 0], W["ln_b"][l, 0],
                     w_router_t, alpha)
        if not moe:
            x1, h2 = outs
            x = _ffn_dense(x1, h2, mod, l, W["w_ffn_gate"][i].astype(BF16), W["w_ffn_up"][i].astype(BF16),
                           W["w_ffn_down"][i].astype(BF16), W["ln_g"][l, 1], W["ln_b"][l, 1], alpha)
        else:
            x1, h2, eidx, rank, cw, cnt = outs
            x = _moe(x1, h2, mod, l, eidx, rank, cw, cnt, W["w_exp_gate"][i].astype(BF16),
                     W["w_exp_up"][i].astype(BF16), W["w_exp_down"][i].astype(BF16), W["ln_g"][l, 1], W["ln_b"][l, 1],
                     alpha)
    return x, k5, v5, jnp.stack(pool_states)


def kernel(x_prompt, x_sample, cache_k, cache_v, state_pool, c_prompt, c_sample, rel_bias, w_ada, b_ada, w_in, b_gate,
           pool_w, pool_scale, lam_qk, subln_g, w_pool_up, w_attn_up, w_o, ln_g, ln_b, w_ffn_gate, w_ffn_up,
           w_ffn_down, w_router, w_exp_gate, w_exp_up, w_exp_down):
    W = dict(w_in=w_in, b_gate=b_gate, pool_w=pool_w, pool_scale=pool_scale, lam_qk=lam_qk, subln_g=subln_g,
             w_pool_up=w_pool_up, w_attn_up=w_attn_up, w_o=w_o, ln_g=ln_g, ln_b=ln_b, w_ffn_gate=w_ffn_gate,
             w_ffn_up=w_ffn_up, w_ffn_down=w_ffn_down, w_router=w_router, w_exp_gate=w_exp_gate, w_exp_up=w_exp_up,
             w_exp_down=w_exp_down)
    depth, d, _ = w_in.shape
    bp, s, _ = x_prompt.shape
    bs, ts, _ = x_sample.shape
    p = cache_k.shape[2]

    mod = _ada(jnp.concatenate([c_prompt, c_sample], axis=0), w_ada, b_ada)
    mod_p = mod[:, :bp].reshape(depth, bp, 1, 6 * d)
    mod_s = mod[:, bp:].reshape(depth, bs, 1, 6 * d)

    tq = ATTN_TQ
    q_pos = np.arange(tq, 2 * tq)
    bias_prev = _bias_table(rel_bias, _bucket_table(q_pos, np.arange(0, tq)))
    bias_diag = _bias_table(rel_bias, _bucket_table(q_pos, q_pos))
    s_pos = p + np.arange(ts)
    bias_past = _bias_table(rel_bias, _bucket_table(s_pos, np.arange(p)))
    bias_new = _bias_table(rel_bias, _bucket_table(s_pos, s_pos))

    y_p, k_p, v_p, pool_p = _trunk(x_prompt, mod_p, 0, None, None, None, (bias_prev, bias_diag), W, depth)
    y_s, k_s, v_s, pool_s = _trunk(x_sample, mod_s, p, state_pool, cache_k, cache_v, (bias_past, bias_new), W, depth)
    return (y_p, y_s, k_p, v_p, pool_p, k_s, v_s, pool_s)
```
